```python
import math
import jax, jax.numpy as jnp
from jax import lax
import numpy as np

D_MODEL = 1024
BATCH = 4
SEQ = 4096
DEPTH = 2
DEC_BATCH = 128
DEC_SEQ = 4
PAST_LEN = 2048
PAGE_SIZE = 128

N_M_LAYERS = (DEPTH + 1) // 2
N_A_LAYERS = DEPTH // 2

M_HEADS = 4
M_QK_DIM = D_MODEL // (2 * M_HEADS)
M_V_DIM = D_MODEL // M_HEADS
M_CHUNK = 64
M_QK_W = M_HEADS * M_QK_DIM
M_V_W = M_HEADS * M_V_DIM
M_PROJ = 2 * M_QK_W + 2 * M_V_W + 2 * M_HEADS

A_HEADS = 8
A_HEAD_DIM = D_MODEL // A_HEADS
MOBA_BLOCK = 256
MOBA_TOPK = 3
Q_BLOCK = 128
ROPE_THETA = 10000.0

FFN_HIDDEN = -(-8 * D_MODEL // 768) * 256
EPS = 1e-6

kernel_name = "mlstm_moba_hybrid_step"


def _rms_norm(x, g):
    x32 = x.astype(jnp.float32)
    y = x32 * lax.rsqrt(jnp.mean(x32 * x32, axis=-1, keepdims=True) + EPS)
    return (y * g.astype(jnp.float32)).astype(x.dtype)


def _rope(x, pos):
    half = x.shape[-1] // 2
    inv_freq = ROPE_THETA ** (-jnp.arange(half, dtype=jnp.float32) / half)
    ang = pos.astype(jnp.float32)[:, None] * inv_freq[None, :]
    cos = jnp.cos(ang)[None, :, None, :]
    sin = jnp.sin(ang)[None, :, None, :]
    x32 = x.astype(jnp.float32)
    x1, x2 = x32[..., :half], x32[..., half:]
    return jnp.concatenate([x1 * cos - x2 * sin, x2 * cos + x1 * sin], axis=-1).astype(x.dtype)


def _swiglu(h, w_gu, w_down):
    g, u = jnp.split(jnp.einsum('bsd,df->bsf', h, w_gu), 2, axis=-1)
    return jnp.einsum('bsf,fd->bsd', jax.nn.silu(g) * u, w_down)


def _mlstm_chunk(carry, xs):
    C, n, m = carry
    q, k, v, ig, lf = xs
    L = q.shape[2]
    b = jnp.cumsum(lf, axis=-1)
    causal = jnp.tril(jnp.ones((L, L), dtype=bool))
    d = jnp.where(causal, b[..., :, None] - b[..., None, :] + ig[..., None, :], -jnp.inf)
    m_inter = b + m[..., None]
    m_t = jnp.maximum(m_inter, jnp.max(d, axis=-1))
    w_inter = jnp.exp(m_inter - m_t)
    a = jnp.einsum('bhtd,bhsd->bhts', q, k) * jnp.exp(d - m_t[..., None])
    num = w_inter[..., None] * jnp.einsum('bhtd,bhde->bhte', q, C) + jnp.einsum('bhts,bhse->bhte', a, v)
    den = w_inter * jnp.einsum('bhtd,bhd->bht', q, n) + jnp.sum(a, axis=-1)
    h = num / jnp.maximum(jnp.abs(den), jnp.exp(-m_t))[..., None]
    m_new = m_t[..., -1]
    w_state = jnp.exp(b[..., -1] + m - m_new)
    w_s = jnp.exp(b[..., -1:] - b + ig - m_new[..., None])
    C_new = w_state[..., None, None] * C + jnp.einsum('bhs,bhsd,bhse->bhde', w_s, k, v)
    n_new = w_state[..., None] * n + jnp.einsum('bhs,bhsd->bhd', w_s, k)
    return (C_new, n_new, m_new), h


def _mlstm_mixer(h, w_in, b_gate, g_out, w_out, C0, n0, m0):
    B, S, _ = h.shape
    proj = jnp.einsum('bsd,de->bse', h, w_in)
    q, k, v, o, gates = jnp.split(proj, [M_QK_W, 2 * M_QK_W, 2 * M_QK_W + M_V_W, 2 * M_QK_W + 2 * M_V_W], axis=-1)

    def heads(t, dh):
        return t.reshape(B, S, M_HEADS, dh).transpose(0, 2, 1, 3).astype(jnp.float32)

    q = heads(q, M_QK_DIM) * (M_QK_DIM ** -0.5)
    k = heads(k, M_QK_DIM)
    v = heads(v, M_V_DIM)
    gates = (gates.astype(jnp.float32) + b_gate.astype(jnp.float32)).transpose(0, 2, 1)
    ig = gates[:, :M_HEADS]
    lf = jax.nn.log_sigmoid(gates[:, M_HEADS:])
    L = math.gcd(S, M_CHUNK)
    nc = S // L

    def chunks(t):
        return jnp.moveaxis(t.reshape((B, M_HEADS, nc, L) + t.shape[3:]), 2, 0)

    carry0 = (C0.astype(jnp.float32), n0.astype(jnp.float32), m0.astype(jnp.float32))
    (C, n, m), hs = lax.scan(_mlstm_chunk, carry0, (chunks(q), chunks(k), chunks(v), chunks(ig), chunks(lf)))
    hs = jnp.moveaxis(hs, 0, 2).reshape(B, M_HEADS, S, M_V_DIM).transpose(0, 2, 1, 3)
    hn = hs * lax.rsqrt(jnp.mean(hs * hs, axis=-1, keepdims=True) + EPS)
    hn = hn * g_out.astype(jnp.float32).reshape(M_HEADS, M_V_DIM)
    y = hn.reshape(B, S, M_V_W) * jax.nn.sigmoid(o.astype(jnp.float32))
    out = jnp.einsum('bse,ed->bsd', y.astype(h.dtype), w_out)
    return out, C, n, m


def _moba_attention(q, k_all, v_all, pos0):
    B, S, H, D = q.shape
    T = k_all.shape[1]
    nb = -(-T // MOBA_BLOCK)
    pad = nb * MOBA_BLOCK - T
    kb = jnp.pad(k_all, ((0, 0), (0, pad), (0, 0), (0, 0))).reshape(B, nb, MOBA_BLOCK, H, D)
    vb = jnp.pad(v_all, ((0, 0), (0, pad), (0, 0), (0, 0))).reshape(B, nb, MOBA_BLOCK, H, D)
    kmean = jnp.mean(kb, axis=2, dtype=jnp.float32)
    k_sel = min(MOBA_TOPK, nb)
    qb_len = math.gcd(S, Q_BLOCK)
    nq = S // qb_len
    q_blocks = jnp.moveaxis(q.reshape(B, nq, qb_len, H, D), 1, 0)
    pos_blocks = (pos0 + jnp.arange(S, dtype=jnp.int32)).reshape(nq, qb_len)
    bi = jnp.arange(B)[:, None, None]
    hi = jnp.arange(H)[None, :, None]
    offs = jnp.arange(MOBA_BLOCK, dtype=jnp.int32)
    scale = D ** -0.5

    def attend_block(args):
        qb, pos = args
        own = pos // MOBA_BLOCK
        scores = jnp.einsum('bqhd,bnhd->bhqn', qb.astype(jnp.float32), kmean)
        past = jnp.arange(nb)[None, :] < own[:, None]
        _, sel = lax.top_k(jnp.where(past, scores, -jnp.inf), k_sel)
        valid = sel < own[:, None]
        idx = jnp.concatenate([sel, jnp.broadcast_to(own[:, None], (B, H, qb_len, 1))], axis=-1)
        masks = [valid[..., j:j + 1] for j in range(k_sel)]
        masks.append((own[:, None] * MOBA_BLOCK + offs[None, :]) <= pos[:, None])
        logits = []
        for j in range(k_sel + 1):
            kj = kb[bi, idx[..., j], :, hi]
            lj = jnp.einsum('bqhd,bhqkd->bhqk', qb, kj, preferred_element_type=jnp.float32) * scale
            logits.append(jnp.where(masks[j], lj, -jnp.inf))
        p = jax.nn.softmax(jnp.concatenate(logits, axis=-1), axis=-1).astype(vb.dtype)
        out = jnp.zeros((B, qb_len, H, D), vb.dtype)
        for j in range(k_sel + 1):
            vj = vb[bi, idx[..., j], :, hi]
            out = out + jnp.einsum('bhqk,bhqkd->bqhd', p[..., j * MOBA_BLOCK:(j + 1) * MOBA_BLOCK], vj)
        return out

    out = lax.map(attend_block, (q_blocks, pos_blocks))
    return jnp.moveaxis(out, 0, 1).reshape(B, S, H, D)


def _moba_mixer(h, w_qkv, g_q, g_k, w_out, pos0, k_past, v_past):
    B, S, _ = h.shape
    qkv = jnp.einsum('bsd,de->bse', h, w_qkv).reshape(B, S, 3, A_HEADS, A_HEAD_DIM)
    q, k, v = qkv[:, :, 0], qkv[:, :, 1], qkv[:, :, 2]
    pos = pos0 + jnp.arange(S, dtype=jnp.int32)
    q = _rope(_rms_norm(q, g_q), pos)
    k = _rope(_rms_norm(k, g_k), pos)
    if k_past is None:
        k_all, v_all = k, v
    else:
        k_all = jnp.concatenate([k_past, k], axis=1)
        v_all = jnp.concatenate([v_past, v], axis=1)
    o = _moba_attention(q, k_all, v_all, pos0)
    out = jnp.einsum('bse,ed->bsd', o.reshape(B, S, A_HEADS * A_HEAD_DIM), w_out)
    return out, k, v


def _trunk(x, pos0, C0, n0, m0, cache_k, cache_v, page_table, norm_mix, norm_ffn, w_in_m, b_gate_m,
           norm_h_m, w_out_m, w_qkv_a, norm_q_a, norm_k_a, w_out_a, w_gu, w_down):
    new_k, new_v, new_C, new_n, new_m = [], [], [], [], []
    for i in range(DEPTH):
        j = i // 2
        h = _rms_norm(x, norm_mix[i])
        if i % 2 == 0:
            y, C, n, m = _mlstm_mixer(h, w_in_m[j], b_gate_m[j], norm_h_m[j], w_out_m[j], C0[j], n0[j], m0[j])
            new_C.append(C)
            new_n.append(n)
            new_m.append(m)
        else:
            if page_table is None:
                k_past, v_past = None, None
            else:
                k_past = cache_k[j, page_table].reshape(x.shape[0], -1, A_HEADS, A_HEAD_DIM)
                v_past = cache_v[j, page_table].reshape(x.shape[0], -1, A_HEADS, A_HEAD_DIM)
            y, k, v = _moba_mixer(h, w_qkv_a[j], norm_q_a[j], norm_k_a[j], w_out_a[j], pos0, k_past, v_past)
            new_k.append(k)
            new_v.append(v)
        x = x + y
        x = x + _swiglu(_rms_norm(x, norm_ffn[i]), w_gu[i], w_down[i])
    return x, jnp.stack(new_k), jnp.stack(new_v), jnp.stack(new_C), jnp.stack(new_n), jnp.stack(new_m)


def setup_inputs(seed: int = 0) -> dict:
    key = jax.random.key(seed)
    ks = jax.random.split(key, 24)
    f32 = jnp.float32
    n_pages = PAST_LEN // PAGE_SIZE
    used = DEC_BATCH * n_pages
    n_phys = used + max(1, used // 4)
    nrm = lambda k, s: jax.random.normal(k, s, f32)
    x_prompt = nrm(ks[0], (BATCH, SEQ, D_MODEL))
    x_sample = nrm(ks[1], (DEC_BATCH, DEC_SEQ, D_MODEL))
    state_C = 0.1 * nrm(ks[2], (N_M_LAYERS, DEC_BATCH, M_HEADS, M_QK_DIM, M_V_DIM))
    state_n = 0.1 * nrm(ks[3], (N_M_LAYERS, DEC_BATCH, M_HEADS, M_QK_DIM))
    state_m = nrm(ks[4], (N_M_LAYERS, DEC_BATCH, M_HEADS))
    cache_k = nrm(ks[5], (N_A_LAYERS, n_phys, PAGE_SIZE, A_HEADS, A_HEAD_DIM))
    cache_v = nrm(ks[6], (N_A_LAYERS, n_phys, PAGE_SIZE, A_HEADS, A_HEAD_DIM))
    page_table = jax.random.permutation(ks[7], n_phys)[:used].reshape(DEC_BATCH, n_pages).astype(jnp.int32)
    norm_mix = 1.0 + 0.05 * nrm(ks[8], (DEPTH, D_MODEL))
    norm_ffn = 1.0 + 0.05 * nrm(ks[9], (DEPTH, D_MODEL))
    w_in_m = nrm(ks[10], (N_M_LAYERS, D_MODEL, M_PROJ)) * D_MODEL ** -0.5
    b_in = 0.1 * nrm(ks[11], (N_M_LAYERS, M_HEADS))
    b_f = jnp.linspace(3.0, 6.0, M_HEADS, dtype=f32)[None, :] + 0.1 * nrm(ks[12], (N_M_LAYERS, M_HEADS))
    b_gate_m = jnp.concatenate([b_in, b_f], axis=-1)
    norm_h_m = 1.0 + 0.05 * nrm(ks[13], (N_M_LAYERS, M_V_W))
    w_out_m = nrm(ks[14], (N_M_LAYERS, M_V_W, D_MODEL)) * M_V_W ** -0.5
    w_qkv_a = nrm(ks[15], (N_A_LAYERS, D_MODEL, 3 * A_HEADS * A_HEAD_DIM)) * D_MODEL ** -0.5
    norm_q_a = 1.0 + 0.05 * nrm(ks[16], (N_A_LAYERS, A_HEAD_DIM))
    norm_k_a = 1.0 + 0.05 * nrm(ks[17], (N_A_LAYERS, A_HEAD_DIM))
    w_out_a = nrm(ks[18], (N_A_LAYERS, A_HEADS * A_HEAD_DIM, D_MODEL)) * (A_HEADS * A_HEAD_DIM) ** -0.5
    w_gu = nrm(ks[19], (DEPTH, D_MODEL, 2 * FFN_HIDDEN)) * D_MODEL ** -0.5
    w_down = nrm(ks[20], (DEPTH, FFN_HIDDEN, D_MODEL)) * FFN_HIDDEN ** -0.5
    return {"x_prompt": x_prompt, "x_sample": x_sample, "state_C": state_C, "state_n": state_n,
            "state_m": state_m, "cache_k": cache_k, "cache_v": cache_v, "page_table": page_table,
            "norm_mix": norm_mix, "norm_ffn": norm_ffn, "w_in_m": w_in_m, "b_gate_m": b_gate_m,
            "norm_h_m": norm_h_m, "w_out_m": w_out_m, "w_qkv_a": w_qkv_a, "norm_q_a": norm_q_a,
            "norm_k_a": norm_k_a, "w_out_a": w_out_a, "w_gu": w_gu, "w_down": w_down}


def reference(x_prompt, x_sample, state_C, state_n, state_m, cache_k, cache_v, page_table, norm_mix,
              norm_ffn, w_in_m, b_gate_m, norm_h_m, w_out_m, w_qkv_a, norm_q_a, norm_k_a, w_out_a, w_gu, w_down):
    b_p = x_prompt.shape[0]
    C0 = jnp.zeros((N_M_LAYERS, b_p) + state_C.shape[2:], jnp.float32)
    n0 = jnp.zeros((N_M_LAYERS, b_p) + state_n.shape[2:], jnp.float32)
    m0 = jnp.zeros((N_M_LAYERS, b_p) + state_m.shape[2:], jnp.float32)
    y_prompt, prompt_k, prompt_v, prompt_C, prompt_n, prompt_m = _trunk(
        x_prompt, 0, C0, n0, m0, None, None, None, norm_mix, norm_ffn, w_in_m, b_gate_m, norm_h_m,
        w_out_m, w_qkv_a, norm_q_a, norm_k_a, w_out_a, w_gu, w_down)
    past_len = page_table.shape[1] * cache_k.shape[2]
    y_sample, sample_k, sample_v, sample_C, sample_n, sample_m = _trunk(
        x_sample, past_len, state_C, state_n, state_m, cache_k, cache_v, page_table, norm_mix, norm_ffn,
        w_in_m, b_gate_m, norm_h_m, w_out_m, w_qkv_a, norm_q_a, norm_k_a, w_out_a, w_gu, w_down)
    sd = state_C.dtype
    return (y_prompt, y_sample, prompt_k, prompt_v, prompt_C.astype(sd), prompt_n.astype(sd),
            prompt_m.astype(sd), sample_k, sample_v, sample_C.astype(sd), sample_n.astype(sd),
            sample_m.astype(sd))
```

```python
import functools
import math

import jax
import jax.numpy as jnp
from jax import lax
from jax.experimental import pallas as pl
from jax.experimental.pallas import tpu as pltpu

F32 = jnp.float32
BF16 = jnp.bfloat16

D_MODEL = 1024
M_HEADS = 4
M_QK_DIM = 128
M_V_DIM = 256
M_QK_W = M_HEADS * M_QK_DIM
M_V_W = M_HEADS * M_V_DIM
A_HEADS = 8
A_HEAD_DIM = 128
MOBA_BLOCK = 256
MOBA_TOPK = 3
ROPE_THETA = 10000.0
FFN_HIDDEN = 2816
EPS = 1e-6

LANES = 128
SUBLANES = 8
VMEM_LIMIT_BYTES = 56 * 1024 * 1024

MLSTM_CHUNK = 128
DEC_PAD = SUBLANES
PAGES_PER_STEP = 4
NEG_INF = float("-inf")

_HIGHEST = lax.Precision.HIGHEST
_NT = (((1,), (1,)), ((), ()))
_TN = (((0,), (0,)), ((), ()))


def _params(*sem):
    return pltpu.CompilerParams(dimension_semantics=sem, vmem_limit_bytes=VMEM_LIMIT_BYTES)


def _const_spec(shape):
    nd = len(shape)
    return pl.BlockSpec(shape, lambda *_: (0,) * nd, pipeline_mode=pl.Buffered(1))


def _rms(x, g):
    return x * lax.rsqrt(jnp.mean(x * x, axis=-1, keepdims=True) + EPS) * g


def _norm_proj_kernel(x_ref, g_ref, *refs, n_out):
    w_refs, o_refs = refs[:n_out], refs[n_out:]
    hb = _rms(x_ref[...], g_ref[...]).astype(BF16)
    for w_ref, o_ref in zip(w_refs, o_refs):
        o_ref[...] = jnp.dot(hb, w_ref[...], preferred_element_type=F32)


def _norm_proj(x, g, ws, tm):
    n, d = x.shape
    return pl.pallas_call(
        functools.partial(_norm_proj_kernel, n_out=len(ws)),
        grid=(n // tm,),
        in_specs=[pl.BlockSpec((tm, d), lambda i: (i, 0)), _const_spec((1, d))]
        + [_const_spec(w.shape) for w in ws],
        out_specs=[pl.BlockSpec((tm, w.shape[1]), lambda i: (i, 0)) for w in ws],
        out_shape=[jax.ShapeDtypeStruct((n, w.shape[1]), F32) for w in ws],
        compiler_params=_params("parallel"),
        name="norm_proj",
    )(x, g, *ws)


def _mlstm_kernel(q_ref, k_ref, v_ref, gt_ref, bias_ref, c0_ref, n0_ref, m0_ref,
                  hs_ref, c_ref, n_ref, m_ref, *pad_refs, rows, valid, chunk):
    ci = pl.program_id(1)

    @pl.when(ci == 0)
    def _():
        c_ref[...] = c0_ref[...]
        n_ref[...] = n0_ref[...]
        m_ref[...] = m0_ref[...]

    if rows == chunk:
        q_all, k_all, v_all, gt_all = q_ref[...], k_ref[...], v_ref[...], gt_ref[...]
    else:
        qp, kp, vp, gp = pad_refs
        for p in pad_refs:
            p[...] = jnp.zeros(p.shape, p.dtype)
        qp[0:rows, :], kp[0:rows, :], vp[0:rows, :], gp[0:rows, :] = (
            q_ref[...], k_ref[...], v_ref[...], gt_ref[...])
        q_all, k_all, v_all, gt_all = qp[...], kp[...], vp[...], gp[...]

    L = chunk
    row_l = lax.broadcasted_iota(jnp.int32, (L, LANES), 0)
    lane_l = lax.broadcasted_iota(jnp.int32, (L, LANES), 1)
    is_ig = lane_l < M_HEADS
    real = row_l < valid
    gc = gt_all + bias_ref[...]
    lfc = jnp.where(real, jax.nn.log_sigmoid(gc), 0.0)
    igc = jnp.where(real, gc, NEG_INF)
    gsel = jnp.where(is_ig, igc, lfc)

    eye8 = (lax.broadcasted_iota(jnp.int32, (SUBLANES, LANES), 0)
            == lax.broadcasted_iota(jnp.int32, (SUBLANES, LANES), 1)).astype(F32)
    gsel_fin = jnp.where(real, gsel, 0.0)
    rows_g = lax.dot_general(eye8, gsel_fin, _NT, precision=_HIGHEST, preferred_element_type=F32)
    col_real = lax.broadcasted_iota(jnp.int32, (SUBLANES, L), 1) < valid
    rows_is_ig = lax.broadcasted_iota(jnp.int32, (SUBLANES, L), 0) < M_HEADS
    rows_g = jnp.where(jnp.logical_and(rows_is_ig, jnp.logical_not(col_real)), NEG_INF, rows_g)

    tt = lax.broadcasted_iota(jnp.int32, (L, L), 0)
    ss = lax.broadcasted_iota(jnp.int32, (L, L), 1)
    causal = ss <= tt
    tril = causal.astype(F32)
    triu = (tt <= ss).astype(F32)
    b_cols = jnp.dot(tril, lfc, precision=_HIGHEST, preferred_element_type=F32)
    lf_rows = jnp.where(rows_is_ig, 0.0, rows_g)
    b_rows = jnp.dot(lf_rows, triu, precision=_HIGHEST, preferred_element_type=F32)

    m_prev_row = m_ref[0]
    lane_1 = lax.broadcasted_iota(jnp.int32, (1, LANES), 1)
    m_new_row = m_prev_row
    scale = M_QK_DIM ** -0.5
    for h in range(M_HEADS):
        q = q_all[:, h * M_QK_DIM:(h + 1) * M_QK_DIM] * scale
        k = k_all[:, h * M_QK_DIM:(h + 1) * M_QK_DIM]
        v = v_all[:, h * M_V_DIM:(h + 1) * M_V_DIM]
        ig_c = gsel[:, h:h + 1]
        b_c = b_cols[:, M_HEADS + h:M_HEADS + h + 1]
        ig_r = rows_g[h:h + 1, :]
        b_r = b_rows[M_HEADS + h:M_HEADS + h + 1, :]
        m_prev = m_prev_row[:, h:h + 1]
        c_old = c_ref[0, h]
        n_old = n_ref[0, h]

        d = jnp.where(causal, b_c - b_r + ig_r, NEG_INF)
        m_inter = b_c + m_prev
        m_t = jnp.maximum(m_inter, jnp.max(d, axis=-1, keepdims=True))
        w_inter = jnp.exp(m_inter - m_t)
        qb, kb, vb = q.astype(BF16), k.astype(BF16), v.astype(BF16)
        a = lax.dot_general(qb, kb, _NT, preferred_element_type=F32) * jnp.exp(d - m_t)
        num = (w_inter * jnp.dot(qb, c_old.astype(BF16), preferred_element_type=F32)
               + jnp.dot(a.astype(BF16), vb, preferred_element_type=F32))
        den = w_inter * jnp.sum(q * n_old, axis=-1, keepdims=True) + jnp.sum(a, axis=-1, keepdims=True)
        hval = num / jnp.maximum(jnp.abs(den), jnp.exp(-m_t))
        hs_ref[:, h * M_V_DIM:(h + 1) * M_V_DIM] = hval[0:rows, :]

        m_new = m_t[L - 1:L, :]
        b_last = b_c[L - 1:L, :]
        w_state = jnp.exp(b_last + m_prev - m_new)
        w_s = jnp.exp(b_last - b_c + ig_c - m_new)
        kw = k * w_s
        c_ref[0, h] = w_state * c_old + lax.dot_general(kw.astype(BF16), vb, _TN, preferred_element_type=F32)
        n_ref[0, h] = w_state * n_old + jnp.sum(kw, axis=0, keepdims=True)
        m_new_row = jnp.where(lane_1 == h, m_new, m_new_row)
    m_ref[0] = m_new_row


def _mlstm(q, k, v, gates, bias, c0, n0, m0, valid):
    b, s, _ = q.shape
    if s >= MLSTM_CHUNK:
        rows, nc = MLSTM_CHUNK, s // MLSTM_CHUNK
        assert valid == s and s % MLSTM_CHUNK == 0
        scratch = []
    else:
        rows, nc = s, 1
        scratch = [pltpu.VMEM((MLSTM_CHUNK, w), F32) for w in (M_QK_W, M_QK_W, M_V_W, LANES)]
    tok = lambda w: pl.BlockSpec((None, rows, w), lambda bi, ci: (bi, ci, 0))
    st_c = pl.BlockSpec((1, M_HEADS, M_QK_DIM, M_V_DIM), lambda bi, ci: (bi, 0, 0, 0))
    st_n = pl.BlockSpec((1, M_HEADS, 1, M_QK_DIM), lambda bi, ci: (bi, 0, 0, 0))
    st_m = pl.BlockSpec((1, 1, LANES), lambda bi, ci: (bi, 0, 0))
    return pl.pallas_call(
        functools.partial(_mlstm_kernel, rows=rows, valid=valid, chunk=MLSTM_CHUNK),
        grid=(b, nc),
        in_specs=[tok(M_QK_W), tok(M_QK_W), tok(M_V_W), tok(LANES), _const_spec((1, LANES)), st_c, st_n, st_m],
        out_specs=[tok(M_V_W), st_c, st_n, st_m],
        out_shape=[jax.ShapeDtypeStruct((b, s, M_V_W), F32), jax.ShapeDtypeStruct(c0.shape, F32),
                   jax.ShapeDtypeStruct(n0.shape, F32), jax.ShapeDtypeStruct(m0.shape, F32)],
        scratch_shapes=scratch,
        compiler_params=_params("parallel", "arbitrary"),
        name="mlstm_chunks",
    )(q, k, v, gates, bias, c0, n0, m0)


def _mlstm_out_kernel(hs_ref, o_ref, x_ref, g_ref, w_ref, out_ref):
    parts = []
    for h in range(M_HEADS):
        sl = slice(h * M_V_DIM, (h + 1) * M_V_DIM)
        parts.append(_rms(hs_ref[:, sl], g_ref[:, sl]))
    y = jnp.concatenate(parts, axis=-1) * jax.nn.sigmoid(o_ref[...])
    out_ref[...] = x_ref[...] + jnp.dot(y.astype(BF16), w_ref[...], preferred_element_type=F32)


def _mlstm_out(hs, o, x, g, w, tm):
    n, d = x.shape
    row = lambda wd: pl.BlockSpec((tm, wd), lambda i: (i, 0))
    return pl.pallas_call(
        _mlstm_out_kernel,
        grid=(n // tm,),
        in_specs=[row(M_V_W), row(M_V_W), row(d), _const_spec((1, M_V_W)), _const_spec(w.shape)],
        out_specs=row(d),
        out_shape=jax.ShapeDtypeStruct((n, d), F32),
        compiler_params=_params("parallel"),
        name="mlstm_out",
    )(hs, o, x, g, w)


def _proj_res_kernel(a_ref, x_ref, w_ref, out_ref):
    out_ref[...] = x_ref[...] + jnp.dot(a_ref[...].astype(BF16), w_ref[...], preferred_element_type=F32)


def _proj_res(a, x, w, tm):
    n, d = x.shape
    return pl.pallas_call(
        _proj_res_kernel,
        grid=(n // tm,),
        in_specs=[pl.BlockSpec((tm, a.shape[1]), lambda i: (i, 0)), pl.BlockSpec((tm, d), lambda i: (i, 0)),
                  _const_spec(w.shape)],
        out_specs=pl.BlockSpec((tm, d), lambda i: (i, 0)),
        out_shape=jax.ShapeDtypeStruct((n, d), F32),
        compiler_params=_params("parallel"),
        name="proj_res",
    )(a, x, w)


FFN_CHUNKS = 2


def _ffn_kernel(x_ref, g_ref, wg_ref, wu_ref, wd_ref, out_ref):
    x = x_ref[...]
    hb = _rms(x, g_ref[...]).astype(BF16)
    acc = x
    cw = FFN_HIDDEN // FFN_CHUNKS
    for c in range(FFN_CHUNKS):
        sl = slice(c * cw, (c + 1) * cw)
        gg = jnp.dot(hb, wg_ref[:, sl], preferred_element_type=F32)
        uu = jnp.dot(hb, wu_ref[:, sl], preferred_element_type=F32)
        act = (jax.nn.silu(gg) * uu).astype(BF16)
        acc = acc + jnp.dot(act, wd_ref[sl, :], preferred_element_type=F32)
    out_ref[...] = acc


def _ffn(x, g, wg, wu, wd, tm):
    n, d = x.shape
    return pl.pallas_call(
        _ffn_kernel,
        grid=(n // tm,),
        in_specs=[pl.BlockSpec((tm, d), lambda i: (i, 0)), _const_spec((1, d)),
                  _const_spec(wg.shape), _const_spec(wu.shape), _const_spec(wd.shape)],
        out_specs=pl.BlockSpec((tm, d), lambda i: (i, 0)),
        out_shape=jax.ShapeDtypeStruct((n, d), F32),
        compiler_params=_params("parallel"),
        name="ffn",
    )(x, g, wg, wu, wd)


def _qkv_kernel(x_ref, g_ref, wq_ref, wk_ref, wv_ref, gq_ref, gk_ref, cos_ref, sin_ref,
                q_ref, k_ref, v_ref):
    hb = _rms(x_ref[...], g_ref[...]).astype(BF16)
    v_ref[...] = jnp.dot(hb, wv_ref[...], preferred_element_type=F32)
    cos, sin = cos_ref[...], sin_ref[...]
    for w_ref, gh_ref, o_ref in ((wq_ref, gq_ref, q_ref), (wk_ref, gk_ref, k_ref)):
        t = jnp.dot(hb, w_ref[...], preferred_element_type=F32)
        for h in range(A_HEADS):
            sl = slice(h * A_HEAD_DIM, (h + 1) * A_HEAD_DIM)
            y = _rms(t[:, sl], gh_ref[...])
            o_ref[:, sl] = y * cos + pltpu.roll(y, A_HEAD_DIM // 2, 1) * sin


def _qkv(x, g, wq, wk, wv, gq, gk, cos, sin, tm):
    n, d = x.shape
    e = wq.shape[1]
    n_tab = cos.shape[0] // tm
    row = lambda wd: pl.BlockSpec((tm, wd), lambda i: (i, 0))
    tab = pl.BlockSpec((tm, A_HEAD_DIM), lambda i: (i % n_tab, 0))
    return pl.pallas_call(
        _qkv_kernel,
        grid=(n // tm,),
        in_specs=[row(d), _const_spec((1, d)), _const_spec(wq.shape), _const_spec(wk.shape),
                  _const_spec(wv.shape), _const_spec((1, A_HEAD_DIM)), _const_spec((1, A_HEAD_DIM)), tab, tab],
        out_specs=[row(e), row(e), row(e)],
        out_shape=[jax.ShapeDtypeStruct((n, e), F32)] * 3,
        compiler_params=_params("parallel"),
        name="qkv_rope",
    )(x, g, wq, wk, wv, gq, gk, cos, sin)


def _rope_tables(pos):
    half = A_HEAD_DIM // 2
    inv_freq = ROPE_THETA ** (-jnp.arange(half, dtype=F32) / half)
    ang = pos.astype(F32)[:, None] * inv_freq[None, :]
    c, s = jnp.cos(ang), jnp.sin(ang)
    return jnp.concatenate([c, c], axis=-1), jnp.concatenate([-s, s], axis=-1)


def _top_blocks(scores, allowed, n_lanes_rows):
    lane = lax.broadcasted_iota(jnp.int32, (n_lanes_rows, LANES), 1)
    s = jnp.where(allowed, scores, NEG_INF)
    picked = jnp.zeros((n_lanes_rows, LANES), F32)
    for _ in range(MOBA_TOPK):
        mx = jnp.max(s, axis=-1, keepdims=True)
        cand = jnp.where(jnp.logical_and(s == mx, s > NEG_INF), lane, LANES)
        first = jnp.min(cand, axis=-1, keepdims=True)
        hit = lane == first
        picked = jnp.where(hit, 1.0, picked)
        s = jnp.where(hit, NEG_INF, s)
    return picked


def _moba_prompt_kernel(q_ref, k_ref, v_ref, o_ref, kb_ref, vb_ref, kmean_ref, *, n_blocks):
    qi = pl.program_id(2)
    blk = MOBA_BLOCK

    @pl.when(qi == 0)
    def _():
        kmean_ref[...] = jnp.zeros(kmean_ref.shape, F32)
        for j in range(n_blocks):
            kj = k_ref[j * blk:(j + 1) * blk, :]
            kmean_ref[j:j + 1, :] = jnp.mean(kj, axis=0, keepdims=True)
            kb_ref[j * blk:(j + 1) * blk, :] = kj.astype(BF16)
            vb_ref[j * blk:(j + 1) * blk, :] = v_ref[j * blk:(j + 1) * blk, :].astype(BF16)

    q = q_ref[...]
    qb = q.astype(BF16)
    scale = A_HEAD_DIM ** -0.5
    own = qi

    scores = lax.dot_general(q, kmean_ref[...], _NT, precision=_HIGHEST, preferred_element_type=F32)
    lane = lax.broadcasted_iota(jnp.int32, (blk, LANES), 1)
    picked = _top_blocks(scores, lane < own, blk)

    start = pl.multiple_of(own * blk, blk)
    s0 = lax.dot_general(qb, kb_ref[pl.ds(start, blk), :], _NT, preferred_element_type=F32) * scale
    rr = lax.broadcasted_iota(jnp.int32, (blk, blk), 0)
    cc = lax.broadcasted_iota(jnp.int32, (blk, blk), 1)
    s0 = jnp.where(cc <= rr, s0, NEG_INF)
    m0 = jnp.max(s0, axis=-1, keepdims=True)
    p0 = jnp.exp(s0 - m0)
    l0 = jnp.sum(p0, axis=-1, keepdims=True)
    acc0 = jnp.dot(p0.astype(BF16), vb_ref[pl.ds(start, blk), :], preferred_element_type=F32)

    def body(j, carry):
        m, l, acc = carry
        st = pl.multiple_of(j * blk, blk)
        keep = jnp.sum(jnp.where(lane == j, picked, 0.0), axis=-1, keepdims=True) > 0.0
        s = lax.dot_general(qb, kb_ref[pl.ds(st, blk), :], _NT, preferred_element_type=F32) * scale
        s = jnp.where(keep, s, NEG_INF)
        m_new = jnp.maximum(m, jnp.max(s, axis=-1, keepdims=True))
        alpha = jnp.exp(m - m_new)
        p = jnp.exp(s - m_new)
        l = alpha * l + jnp.sum(p, axis=-1, keepdims=True)
        acc = alpha * acc + jnp.dot(p.astype(BF16), vb_ref[pl.ds(st, blk), :], preferred_element_type=F32)
        return m_new, l, acc

    m, l, acc = lax.fori_loop(0, own, body, (m0, l0, acc0))
    o_ref[...] = acc / l


def _moba_prompt(q, k, v):
    b, s, e = q.shape
    nb = s // MOBA_BLOCK
    assert s % MOBA_BLOCK == 0 and nb <= LANES
    tile = pl.BlockSpec((None, MOBA_BLOCK, A_HEAD_DIM), lambda bi, h, qi: (bi, qi, h))
    full = pl.BlockSpec((None, s, A_HEAD_DIM), lambda bi, h, qi: (bi, 0, h))
    return pl.pallas_call(
        functools.partial(_moba_prompt_kernel, n_blocks=nb),
        grid=(b, A_HEADS, nb),
        in_specs=[tile, full, full],
        out_specs=tile,
        out_shape=jax.ShapeDtypeStruct((b, s, e), F32),
        scratch_shapes=[pltpu.VMEM((s, A_HEAD_DIM), BF16), pltpu.VMEM((s, A_HEAD_DIM), BF16),
                        pltpu.VMEM((LANES, A_HEAD_DIM), F32)],
        compiler_params=_params("parallel", "parallel", "arbitrary"),
        name="moba_prompt",
    )(q, k, v)


def _moba_decode_kernel(pt_ref, q_ref, kn_ref, vn_ref, *refs, n_pages, page_rows):
    pg = PAGES_PER_STEP
    k_refs, v_refs = refs[:pg], refs[pg:2 * pg]
    o_ref = refs[2 * pg]
    qs_ref, logit_ref, p_ref, ksum_ref, acc_ref, l_ref, pn_ref = refs[2 * pg + 1:]
    step = pl.program_id(1)
    n_k = n_pages // pg
    R = A_HEADS * DEC_PAD
    cols = page_rows * A_HEADS
    scale = A_HEAD_DIM ** -0.5
    pages_per_block = MOBA_BLOCK // page_rows
    n_blocks = n_pages // pages_per_block

    def rows_by_head(ref):
        return jnp.concatenate([ref[:, h * A_HEAD_DIM:(h + 1) * A_HEAD_DIM] for h in range(A_HEADS)], axis=0)

    @pl.when(step == 0)
    def _():
        qs_ref[...] = rows_by_head(q_ref)

    @pl.when(step < n_k)
    def _():
        qb = qs_ref[...].astype(BF16)
        for p in range(pg):
            kp = k_refs[p][...]
            lg = lax.dot_general(qb, kp.astype(BF16), _NT, preferred_element_type=F32) * scale
            g = step * pg + p
            logit_ref[g] = lg
            part = jnp.sum(kp.reshape(page_rows, A_HEADS, A_HEAD_DIM), axis=0)
            bidx = (step * pg + p) // pages_per_block
            if p % pages_per_block == 0:
                ksum_ref[bidx] = part
            else:
                ksum_ref[bidx] = ksum_ref[bidx] + part

    @pl.when(step == n_k - 1)
    def _():
        q = qs_ref[...]
        kmean = jnp.concatenate([ksum_ref[bi] for bi in range(n_blocks)], axis=0) * (1.0 / MOBA_BLOCK)
        kmean = jnp.concatenate([kmean, jnp.zeros((LANES - n_blocks * A_HEADS, A_HEAD_DIM), F32)], axis=0)
        scores = lax.dot_general(q, kmean, _NT, precision=_HIGHEST, preferred_element_type=F32)
        row = lax.broadcasted_iota(jnp.int32, (R, LANES), 0)
        lane = lax.broadcasted_iota(jnp.int32, (R, LANES), 1)
        row_head = row // DEC_PAD
        allowed = jnp.logical_and(lane % A_HEADS == row_head, lane < n_blocks * A_HEADS)
        picked = _top_blocks(scores, allowed, R)
        keep_blk = [jnp.sum(jnp.where(lane // A_HEADS == bi, picked, 0.0), axis=-1, keepdims=True) > 0.0
                    for bi in range(n_blocks)]

        kn = rows_by_head(kn_ref).astype(BF16)
        ln = lax.dot_general(q.astype(BF16), kn, _NT, preferred_element_type=F32) * scale
        rn = lax.broadcasted_iota(jnp.int32, (R, R), 0)
        cn = lax.broadcasted_iota(jnp.int32, (R, R), 1)
        ok_n = jnp.logical_and(rn // DEC_PAD == cn // DEC_PAD, cn % DEC_PAD <= rn % DEC_PAD)
        ln = jnp.where(ok_n, ln, NEG_INF)

        rp = lax.broadcasted_iota(jnp.int32, (R, cols), 0)
        cp = lax.broadcasted_iota(jnp.int32, (R, cols), 1)
        same_head = cp % A_HEADS == rp // DEC_PAD
        m = jnp.max(ln, axis=-1, keepdims=True)
        for g in range(n_pages):
            ok = jnp.logical_and(same_head, keep_blk[g // pages_per_block])
            m = jnp.maximum(m, jnp.max(jnp.where(ok, logit_ref[g], NEG_INF), axis=-1, keepdims=True))
        pn = jnp.exp(ln - m)
        l = jnp.sum(pn, axis=-1, keepdims=True)
        for g in range(n_pages):
            ok = jnp.logical_and(same_head, keep_blk[g // pages_per_block])
            pe = jnp.exp(jnp.where(ok, logit_ref[g], NEG_INF) - m)
            l = l + jnp.sum(pe, axis=-1, keepdims=True)
            p_ref[g] = pe.astype(BF16)
        l_ref[...] = jnp.broadcast_to(l, l_ref.shape)
        pn_ref[...] = pn
        acc_ref[...] = jnp.zeros(acc_ref.shape, F32)

    @pl.when(step >= n_k)
    def _():
        acc = acc_ref[...]
        for p in range(pg):
            g = (step - n_k) * pg + p
            acc = acc + jnp.dot(p_ref[g], v_refs[p][...].astype(BF16), preferred_element_type=F32)
        acc_ref[...] = acc

    @pl.when(step == 2 * n_k - 1)
    def _():
        vn = rows_by_head(vn_ref).astype(BF16)
        acc = acc_ref[...] + jnp.dot(pn_ref[...].astype(BF16), vn, preferred_element_type=F32)
        res = acc / l_ref[:, 0:1]
        for h in range(A_HEADS):
            o_ref[:, h * A_HEAD_DIM:(h + 1) * A_HEAD_DIM] = res[h * DEC_PAD:(h + 1) * DEC_PAD, :]


def _moba_decode(q, kn, vn, cache_k, cache_v, page_table):
    b, t, e = q.shape
    n_phys, page_rows = cache_k.shape[0], cache_k.shape[1]
    n_pages = page_table.shape[1]
    pg = PAGES_PER_STEP
    n_k = n_pages // pg
    assert t == DEC_PAD and n_pages % pg == 0 and MOBA_BLOCK % page_rows == 0 and pg % (MOBA_BLOCK // page_rows) == 0
    cols = page_rows * A_HEADS
    ck = cache_k.reshape(n_phys, cols, A_HEAD_DIM)
    cv = cache_v.reshape(n_phys, cols, A_HEAD_DIM)
    pt = page_table.reshape(-1).astype(jnp.int32)
    R = A_HEADS * DEC_PAD

    tok = pl.BlockSpec((None, t, e), lambda bi, s, pt_: (bi, 0, 0))

    def k_spec(p):
        return pl.BlockSpec((None, cols, A_HEAD_DIM),
                            lambda bi, s, pt_: (pt_[bi * n_pages + jnp.minimum(s, n_k - 1) * pg + p], 0, 0))

    def v_spec(p):
        return pl.BlockSpec((None, cols, A_HEAD_DIM),
                            lambda bi, s, pt_: (pt_[bi * n_pages + jnp.maximum(s - n_k, 0) * pg + p], 0, 0))

    grid_spec = pltpu.PrefetchScalarGridSpec(
        num_scalar_prefetch=1,
        grid=(b, 2 * n_k),
        in_specs=[tok, tok, tok] + [k_spec(p) for p in range(pg)] + [v_spec(p) for p in range(pg)],
        out_specs=tok,
        scratch_shapes=[pltpu.VMEM((R, A_HEAD_DIM), F32),
                        pltpu.VMEM((n_pages, R, cols), F32),
                        pltpu.VMEM((n_pages, R, cols), BF16),
                        pltpu.VMEM((n_pages * page_rows // MOBA_BLOCK, A_HEADS, A_HEAD_DIM), F32),
                        pltpu.VMEM((R, A_HEAD_DIM), F32),
                        pltpu.VMEM((R, LANES), F32),
                        pltpu.VMEM((R, R), F32)],
    )
    return pl.pallas_call(
        functools.partial(_moba_decode_kernel, n_pages=n_pages, page_rows=page_rows),
        grid_spec=grid_spec,
        out_shape=jax.ShapeDtypeStruct((b, t, e), F32),
        compiler_params=_params("parallel", "arbitrary"),
        name="moba_decode",
    )(pt, q, kn, vn, *([ck] * pg), *([cv] * pg))


def _trunk(x, valid, pos0, c0, n0, m0, past, w, tm):
    b, s, d = x.shape
    n = b * s
    tm = min(tm, n)
    assert n % tm == 0 and (tm % s == 0 or s % tm == 0)
    xf = x.reshape(n, d)

    q, k, v, o, gates = _norm_proj(xf, w["norm_mix"][0], [w["wq_m"], w["wk_m"], w["wv_m"], w["wo_m"], w["wgate_m"]], tm)
    r3 = lambda t: t.reshape(b, s, t.shape[-1])
    hs, c_new, n_new, m_new = _mlstm(r3(q), r3(k), r3(v), r3(gates), w["b_gate"], c0, n0, m0, valid)
    xf = _mlstm_out(hs.reshape(n, M_V_W), o, xf, w["norm_h_m"], w["w_out_m"], tm)
    xf = _ffn(xf, w["norm_ffn"][0], w["wg"][0], w["wu"][0], w["wd"][0], tm)

    pos = pos0 + jnp.arange(s, dtype=jnp.int32)
    cos, sin = _rope_tables(pos)
    if s < tm:
        cos, sin = jnp.tile(cos, (tm // s, 1)), jnp.tile(sin, (tm // s, 1))
    qa, ka, va = _qkv(xf, w["norm_mix"][1], w["wq_a"], w["wk_a"], w["wv_a"], w["norm_q_a"], w["norm_k_a"], cos, sin, tm)
    if past is None:
        att = _moba_prompt(r3(qa), r3(ka), r3(va))
    else:
        att = _moba_decode(r3(qa), r3(ka), r3(va), *past)
    xf = _proj_res(att.reshape(n, -1), xf, w["w_out_a"], tm)
    xf = _ffn(xf, w["norm_ffn"][1], w["wg"][1], w["wu"][1], w["wd"][1], tm)

    heads = lambda t: t.reshape(b, s, A_HEADS, A_HEAD_DIM)[:, :valid]
    return xf.reshape(b, s, d)[:, :valid], heads(ka), heads(va), c_new, n_new, m_new


def kernel(x_prompt, x_sample, state_C, state_n, state_m, cache_k, cache_v, page_table, norm_mix, norm_ffn,
           w_in_m, b_gate_m, norm_h_m, w_out_m, w_qkv_a, norm_q_a, norm_k_a, w_out_a, w_gu, w_down):
    assert norm_mix.shape[0] == 2 and w_in_m.shape[0] == 1 and w_qkv_a.shape[0] == 1
    bf = lambda t: t.astype(BF16)
    row = lambda t: t.reshape(1, -1).astype(F32)
    w_in = w_in_m[0]
    e_a = A_HEADS * A_HEAD_DIM
    gate_w = jnp.pad(w_in[:, 2 * M_QK_W + 2 * M_V_W:], ((0, 0), (0, LANES - 2 * M_HEADS)))
    w = {
        "norm_mix": [row(norm_mix[0]), row(norm_mix[1])],
        "norm_ffn": [row(norm_ffn[0]), row(norm_ffn[1])],
        "wq_m": bf(w_in[:, :M_QK_W]), "wk_m": bf(w_in[:, M_QK_W:2 * M_QK_W]),
        "wv_m": bf(w_in[:, 2 * M_QK_W:2 * M_QK_W + M_V_W]),
        "wo_m": bf(w_in[:, 2 * M_QK_W + M_V_W:2 * M_QK_W + 2 * M_V_W]),
        "wgate_m": bf(gate_w),
        "b_gate": jnp.pad(b_gate_m[0].astype(F32), (0, LANES - 2 * M_HEADS)).reshape(1, LANES),
        "norm_h_m": row(norm_h_m[0]), "w_out_m": bf(w_out_m[0]),
        "wq_a": bf(w_qkv_a[0][:, :e_a]), "wk_a": bf(w_qkv_a[0][:, e_a:2 * e_a]), "wv_a": bf(w_qkv_a[0][:, 2 * e_a:]),
        "norm_q_a": row(norm_q_a[0]), "norm_k_a": row(norm_k_a[0]), "w_out_a": bf(w_out_a[0]),
        "wg": [bf(w_gu[i][:, :FFN_HIDDEN]) for i in range(2)],
        "wu": [bf(w_gu[i][:, FFN_HIDDEN:]) for i in range(2)],
        "wd": [bf(w_down[i]) for i in range(2)],
    }

    def state(c, nn, mm):
        bsz = c.shape[0]
        m_row = jnp.pad(mm.astype(F32), ((0, 0), (0, LANES - M_HEADS))).reshape(bsz, 1, LANES)
        return c.astype(F32), nn.astype(F32).reshape(bsz, M_HEADS, 1, M_QK_DIM), m_row

    def unstate(c, nn, mm, dt):
        bsz = c.shape[0]
        return (c[None].astype(dt), nn.reshape(1, bsz, M_HEADS, M_QK_DIM).astype(dt),
                mm.reshape(bsz, LANES)[None, :, :M_HEADS].astype(dt))

    sd = state_C.dtype
    bp, sp, _ = x_prompt.shape
    zc = jnp.zeros((bp,) + state_C.shape[2:], F32)
    zn = jnp.zeros((bp,) + state_n.shape[2:], F32)
    zm = jnp.zeros((bp,) + state_m.shape[2:], F32)
    yp, pk, pv, pc, pn, pm = _trunk(x_prompt, sp, 0, *state(zc, zn, zm), None, w, 512)

    bs, ss, _ = x_sample.shape
    xs = jnp.pad(x_sample, ((0, 0), (0, DEC_PAD - ss), (0, 0)))
    past_len = page_table.shape[1] * cache_k.shape[2]
    ys, sk, sv, sc, sn, sm = _trunk(xs, ss, past_len, *state(state_C[0], state_n[0], state_m[0]),
                                    (cache_k[0], cache_v[0], page_table), w, 512)

    return (yp, ys, pk[None], pv[None], *unstate(pc, pn, pm, sd), sk[None], sv[None], *unstate(sc, sn, sm, sd))
```

```python
import functools
import math

import jax
import jax.numpy as jnp
from jax import lax
from jax.experimental import pallas as pl
from jax.experimental.pallas import tpu as pltpu

F32 = jnp.float32
BF16 = jnp.bfloat16

D_MODEL = 1024
M_HEADS = 4
M_QK_DIM = 128
M_V_DIM = 256
M_QK_W = M_HEADS * M_QK_DIM
M_V_W = M_HEADS * M_V_DIM
A_HEADS = 8
A_HEAD_DIM = 128
MOBA_BLOCK = 256
MOBA_TOPK = 3
ROPE_THETA = 10000.0
FFN_HIDDEN = 2816
EPS = 1e-6

LANES = 128
SUBLANES = 8
BF16_ROWS = 16
VT_ROWS = A_HEAD_DIM + BF16_ROWS
VMEM_LIMIT_BYTES = 56 * 1024 * 1024

MLSTM_CHUNK = 128
DEC_PAD = SUBLANES
PAGES_PER_STEP = 8
NEG_INF = float("-inf")

_HIGHEST = lax.Precision.HIGHEST
_NT = (((1,), (1,)), ((), ()))
_TN = (((0,), (0,)), ((), ()))


def _params(*sem):
    return pltpu.CompilerParams(dimension_semantics=sem, vmem_limit_bytes=VMEM_LIMIT_BYTES)


def _const_spec(shape):
    nd = len(shape)
    return pl.BlockSpec(shape, lambda *_: (0,) * nd, pipeline_mode=pl.Buffered(1))


def _rms(x, g):
    return x * lax.rsqrt(jnp.mean(x * x, axis=-1, keepdims=True) + EPS) * g


def _norm_proj_kernel(x_ref, g_ref, *refs, n_out):
    w_refs, o_refs = refs[:n_out], refs[n_out:]
    hb = _rms(x_ref[...], g_ref[...]).astype(BF16)
    for w_ref, o_ref in zip(w_refs, o_refs):
        o_ref[...] = jnp.dot(hb, w_ref[...], preferred_element_type=F32)


def _norm_proj(x, g, ws, tm):
    n, d = x.shape
    return pl.pallas_call(
        functools.partial(_norm_proj_kernel, n_out=len(ws)),
        grid=(n // tm,),
        in_specs=[pl.BlockSpec((tm, d), lambda i: (i, 0)), _const_spec((1, d))]
        + [_const_spec(w.shape) for w in ws],
        out_specs=[pl.BlockSpec((tm, w.shape[1]), lambda i: (i, 0)) for w in ws],
        out_shape=[jax.ShapeDtypeStruct((n, w.shape[1]), F32) for w in ws],
        compiler_params=_params("parallel"),
        name="norm_proj",
    )(x, g, *ws)


def _mlstm_kernel(q_ref, k_ref, v_ref, gt_ref, bias_ref, c0_ref, n0_ref, m0_ref,
                  hs_ref, c_ref, n_ref, m_ref, *pad_refs, rows, valid, chunk):
    ci = pl.program_id(1)

    @pl.when(ci == 0)
    def _():
        c_ref[...] = c0_ref[...]
        n_ref[...] = n0_ref[...]
        m_ref[...] = m0_ref[...]

    if rows == chunk:
        q_all, k_all, v_all, gt_all = q_ref[...], k_ref[...], v_ref[...], gt_ref[...]
    else:
        qp, kp, vp, gp = pad_refs
        for p in pad_refs:
            p[...] = jnp.zeros(p.shape, p.dtype)
        qp[0:rows, :], kp[0:rows, :], vp[0:rows, :], gp[0:rows, :] = (
            q_ref[...], k_ref[...], v_ref[...], gt_ref[...])
        q_all, k_all, v_all, gt_all = qp[...], kp[...], vp[...], gp[...]

    L = chunk
    row_l = lax.broadcasted_iota(jnp.int32, (L, LANES), 0)
    lane_l = lax.broadcasted_iota(jnp.int32, (L, LANES), 1)
    is_ig = lane_l < M_HEADS
    real = row_l < valid
    gc = gt_all + bias_ref[...]
    lfc = jnp.where(real, jax.nn.log_sigmoid(gc), 0.0)
    igc = jnp.where(real, gc, NEG_INF)
    gsel = jnp.where(is_ig, igc, lfc)

    eye8 = (lax.broadcasted_iota(jnp.int32, (SUBLANES, LANES), 0)
            == lax.broadcasted_iota(jnp.int32, (SUBLANES, LANES), 1)).astype(F32)
    gsel_fin = jnp.where(real, gsel, 0.0)
    rows_g = lax.dot_general(eye8, gsel_fin, _NT, precision=_HIGHEST, preferred_element_type=F32)
    col_real = lax.broadcasted_iota(jnp.int32, (SUBLANES, L), 1) < valid
    rows_is_ig = lax.broadcasted_iota(jnp.int32, (SUBLANES, L), 0) < M_HEADS
    rows_g = jnp.where(jnp.logical_and(rows_is_ig, jnp.logical_not(col_real)), NEG_INF, rows_g)

    tt = lax.broadcasted_iota(jnp.int32, (L, L), 0)
    ss = lax.broadcasted_iota(jnp.int32, (L, L), 1)
    causal = ss <= tt
    tril = causal.astype(F32)
    triu = (tt <= ss).astype(F32)
    b_cols = jnp.dot(tril, lfc, precision=_HIGHEST, preferred_element_type=F32)
    lf_rows = jnp.where(rows_is_ig, 0.0, rows_g)
    b_rows = jnp.dot(lf_rows, triu, precision=_HIGHEST, preferred_element_type=F32)

    m_prev_row = m_ref[0]
    lane_1 = lax.broadcasted_iota(jnp.int32, (1, LANES), 1)
    m_new_row = m_prev_row
    scale = M_QK_DIM ** -0.5
    for h in range(M_HEADS):
        q = q_all[:, h * M_QK_DIM:(h + 1) * M_QK_DIM] * scale
        k = k_all[:, h * M_QK_DIM:(h + 1) * M_QK_DIM]
        v = v_all[:, h * M_V_DIM:(h + 1) * M_V_DIM]
        ig_c = gsel[:, h:h + 1]
        b_c = b_cols[:, M_HEADS + h:M_HEADS + h + 1]
        ig_r = rows_g[h:h + 1, :]
        b_r = b_rows[M_HEADS + h:M_HEADS + h + 1, :]
        m_prev = m_prev_row[:, h:h + 1]
        c_old = c_ref[0, h]
        n_old = n_ref[0, h]

        d = jnp.where(causal, b_c - b_r + ig_r, NEG_INF)
        m_inter = b_c + m_prev
        m_t = jnp.maximum(m_inter, jnp.max(d, axis=-1, keepdims=True))
        w_inter = jnp.exp(m_inter - m_t)
        qb, kb, vb = q.astype(BF16), k.astype(BF16), v.astype(BF16)
        a = lax.dot_general(qb, kb, _NT, preferred_element_type=F32) * jnp.exp(d - m_t)
        num = (w_inter * jnp.dot(qb, c_old.astype(BF16), preferred_element_type=F32)
               + jnp.dot(a.astype(BF16), vb, preferred_element_type=F32))
        den = w_inter * jnp.sum(q * n_old, axis=-1, keepdims=True) + jnp.sum(a, axis=-1, keepdims=True)
        hval = num / jnp.maximum(jnp.abs(den), jnp.exp(-m_t))
        hs_ref[:, h * M_V_DIM:(h + 1) * M_V_DIM] = hval[0:rows, :]

        m_new = m_t[L - 1:L, :]
        b_last = b_c[L - 1:L, :]
        w_state = jnp.exp(b_last + m_prev - m_new)
        w_s = jnp.exp(b_last - b_c + ig_c - m_new)
        kw = k * w_s
        c_ref[0, h] = w_state * c_old + lax.dot_general(kw.astype(BF16), vb, _TN, preferred_element_type=F32)
        n_ref[0, h] = w_state * n_old + jnp.sum(kw, axis=0, keepdims=True)
        m_new_row = jnp.where(lane_1 == h, m_new, m_new_row)
    m_ref[0] = m_new_row


def _mlstm(q, k, v, gates, bias, c0, n0, m0, valid):
    b, s, _ = q.shape
    if s >= MLSTM_CHUNK:
        rows, nc = MLSTM_CHUNK, s // MLSTM_CHUNK
        assert valid == s and s % MLSTM_CHUNK == 0
        scratch = []
    else:
        rows, nc = s, 1
        scratch = [pltpu.VMEM((MLSTM_CHUNK, w), F32) for w in (M_QK_W, M_QK_W, M_V_W, LANES)]
    tok = lambda w: pl.BlockSpec((None, rows, w), lambda bi, ci: (bi, ci, 0))
    st_c = pl.BlockSpec((1, M_HEADS, M_QK_DIM, M_V_DIM), lambda bi, ci: (bi, 0, 0, 0))
    st_n = pl.BlockSpec((1, M_HEADS, 1, M_QK_DIM), lambda bi, ci: (bi, 0, 0, 0))
    st_m = pl.BlockSpec((1, 1, LANES), lambda bi, ci: (bi, 0, 0))
    return pl.pallas_call(
        functools.partial(_mlstm_kernel, rows=rows, valid=valid, chunk=MLSTM_CHUNK),
        grid=(b, nc),
        in_specs=[tok(M_QK_W), tok(M_QK_W), tok(M_V_W), tok(LANES), _const_spec((1, LANES)), st_c, st_n, st_m],
        out_specs=[tok(M_V_W), st_c, st_n, st_m],
        out_shape=[jax.ShapeDtypeStruct((b, s, M_V_W), F32), jax.ShapeDtypeStruct(c0.shape, F32),
                   jax.ShapeDtypeStruct(n0.shape, F32), jax.ShapeDtypeStruct(m0.shape, F32)],
        scratch_shapes=scratch,
        compiler_params=_params("parallel", "arbitrary"),
        name="mlstm_chunks",
    )(q, k, v, gates, bias, c0, n0, m0)


def _mlstm_out_kernel(hs_ref, o_ref, x_ref, g_ref, w_ref, out_ref):
    parts = []
    for h in range(M_HEADS):
        sl = slice(h * M_V_DIM, (h + 1) * M_V_DIM)
        parts.append(_rms(hs_ref[:, sl], g_ref[:, sl]))
    y = jnp.concatenate(parts, axis=-1) * jax.nn.sigmoid(o_ref[...])
    out_ref[...] = x_ref[...] + jnp.dot(y.astype(BF16), w_ref[...], preferred_element_type=F32)


def _mlstm_out(hs, o, x, g, w, tm):
    n, d = x.shape
    row = lambda wd: pl.BlockSpec((tm, wd), lambda i: (i, 0))
    return pl.pallas_call(
        _mlstm_out_kernel,
        grid=(n // tm,),
        in_specs=[row(M_V_W), row(M_V_W), row(d), _const_spec((1, M_V_W)), _const_spec(w.shape)],
        out_specs=row(d),
        out_shape=jax.ShapeDtypeStruct((n, d), F32),
        compiler_params=_params("parallel"),
        name="mlstm_out",
    )(hs, o, x, g, w)


def _proj_res_kernel(a_ref, x_ref, w_ref, out_ref):
    out_ref[...] = x_ref[...] + jnp.dot(a_ref[...].astype(BF16), w_ref[...], preferred_element_type=F32)


def _proj_res(a, x, w, tm):
    n, d = x.shape
    return pl.pallas_call(
        _proj_res_kernel,
        grid=(n // tm,),
        in_specs=[pl.BlockSpec((tm, a.shape[1]), lambda i: (i, 0)), pl.BlockSpec((tm, d), lambda i: (i, 0)),
                  _const_spec(w.shape)],
        out_specs=pl.BlockSpec((tm, d), lambda i: (i, 0)),
        out_shape=jax.ShapeDtypeStruct((n, d), F32),
        compiler_params=_params("parallel"),
        name="proj_res",
    )(a, x, w)


FFN_CHUNKS = 2


def _ffn_kernel(x_ref, g_ref, wg_ref, wu_ref, wd_ref, out_ref):
    x = x_ref[...]
    hb = _rms(x, g_ref[...]).astype(BF16)
    acc = x
    cw = FFN_HIDDEN // FFN_CHUNKS
    for c in range(FFN_CHUNKS):
        sl = slice(c * cw, (c + 1) * cw)
        gg = jnp.dot(hb, wg_ref[:, sl], preferred_element_type=F32)
        uu = jnp.dot(hb, wu_ref[:, sl], preferred_element_type=F32)
        act = (jax.nn.silu(gg) * uu).astype(BF16)
        acc = acc + jnp.dot(act, wd_ref[sl, :], preferred_element_type=F32)
    out_ref[...] = acc


def _ffn(x, g, wg, wu, wd, tm):
    n, d = x.shape
    return pl.pallas_call(
        _ffn_kernel,
        grid=(n // tm,),
        in_specs=[pl.BlockSpec((tm, d), lambda i: (i, 0)), _const_spec((1, d)),
                  _const_spec(wg.shape), _const_spec(wu.shape), _const_spec(wd.shape)],
        out_specs=pl.BlockSpec((tm, d), lambda i: (i, 0)),
        out_shape=jax.ShapeDtypeStruct((n, d), F32),
        compiler_params=_params("parallel"),
        name="ffn",
    )(x, g, wg, wu, wd)


def _qkv_kernel(x_ref, g_ref, wq_ref, wk_ref, wv_ref, gq_ref, gk_ref, cos_ref, sin_ref,
                q_ref, k_ref, v_ref, *blk_refs):
    hb = _rms(x_ref[...], g_ref[...]).astype(BF16)
    v_ref[...] = jnp.dot(hb, wv_ref[...], preferred_element_type=F32)
    cos, sin = cos_ref[...], sin_ref[...]
    for w_ref, gh_ref, o_ref in ((wq_ref, gq_ref, q_ref), (wk_ref, gk_ref, k_ref)):
        t = jnp.dot(hb, w_ref[...], preferred_element_type=F32)
        for h in range(A_HEADS):
            sl = slice(h * A_HEAD_DIM, (h + 1) * A_HEAD_DIM)
            y = _rms(t[:, sl], gh_ref[...])
            o_ref[:, sl] = y * cos + pltpu.roll(y, A_HEAD_DIM // 2, 1) * sin
    if blk_refs:
        kb_ref, vt_ref, km_ref = blk_refs
        kb_ref[...] = k_ref[...].astype(BF16)
        for j in range(km_ref.shape[0]):
            rows = slice(j * MOBA_BLOCK, (j + 1) * MOBA_BLOCK)
            mean = jnp.mean(k_ref[rows, :], axis=0, keepdims=True)
            km_ref[j] = jnp.broadcast_to(mean, km_ref.shape[1:])
            for h in range(A_HEADS):
                sl = slice(h * A_HEAD_DIM, (h + 1) * A_HEAD_DIM)
                r0 = h * VT_ROWS
                vt_ref[j, r0:r0 + A_HEAD_DIM, :] = v_ref[rows, sl].T.astype(BF16)
                vt_ref[j, r0 + A_HEAD_DIM:r0 + VT_ROWS, :] = jnp.ones((BF16_ROWS, MOBA_BLOCK), BF16)


def _qkv(x, g, wq, wk, wv, gq, gk, cos, sin, tm, with_blocks):
    n, d = x.shape
    e = wq.shape[1]
    n_tab = cos.shape[0] // tm
    row = lambda wd: pl.BlockSpec((tm, wd), lambda i: (i, 0))
    tab = pl.BlockSpec((tm, A_HEAD_DIM), lambda i: (i % n_tab, 0))
    out_specs = [row(e), row(e), row(e)]
    out_shape = [jax.ShapeDtypeStruct((n, e), F32)] * 3
    if with_blocks:
        assert tm % MOBA_BLOCK == 0
        gpt = tm // MOBA_BLOCK
        vrows = A_HEADS * VT_ROWS
        out_specs += [row(e), pl.BlockSpec((gpt, vrows, MOBA_BLOCK), lambda i: (i, 0, 0)),
                      pl.BlockSpec((gpt, SUBLANES, e), lambda i: (i, 0, 0))]
        out_shape += [jax.ShapeDtypeStruct((n, e), BF16),
                      jax.ShapeDtypeStruct((n // MOBA_BLOCK, vrows, MOBA_BLOCK), BF16),
                      jax.ShapeDtypeStruct((n // MOBA_BLOCK, SUBLANES, e), F32)]
    return pl.pallas_call(
        _qkv_kernel,
        grid=(n // tm,),
        in_specs=[row(d), _const_spec((1, d)), _const_spec(wq.shape), _const_spec(wk.shape),
                  _const_spec(wv.shape), _const_spec((1, A_HEAD_DIM)), _const_spec((1, A_HEAD_DIM)), tab, tab],
        out_specs=out_specs,
        out_shape=out_shape,
        compiler_params=_params("parallel"),
        name="qkv_rope",
    )(x, g, wq, wk, wv, gq, gk, cos, sin)


def _rope_tables(pos):
    half = A_HEAD_DIM // 2
    inv_freq = ROPE_THETA ** (-jnp.arange(half, dtype=F32) / half)
    ang = pos.astype(F32)[:, None] * inv_freq[None, :]
    c, s = jnp.cos(ang), jnp.sin(ang)
    return jnp.concatenate([c, c], axis=-1), jnp.concatenate([-s, s], axis=-1)


def _top_blocks(scores, allowed, n_lanes_rows):
    lane = lax.broadcasted_iota(jnp.int32, (n_lanes_rows, LANES), 1)
    s = jnp.where(allowed, scores, NEG_INF)
    picked = jnp.zeros((n_lanes_rows, LANES), F32)
    for _ in range(MOBA_TOPK):
        mx = jnp.max(s, axis=-1, keepdims=True)
        cand = jnp.where(jnp.logical_and(s == mx, s > NEG_INF), lane, LANES)
        first = jnp.min(cand, axis=-1, keepdims=True)
        hit = lane == first
        picked = jnp.where(hit, 1.0, picked)
        s = jnp.where(hit, NEG_INF, s)
    return picked


MOBA_HEADS_PER_STEP = 4
MASK_BIG = 2.0 ** 100


def _top_blocks_t(scores_t, allowed):
    row = lax.broadcasted_iota(jnp.int32, scores_t.shape, 0)
    s = jnp.where(allowed, scores_t, NEG_INF)
    picked = jnp.zeros(scores_t.shape, F32)
    for _ in range(MOBA_TOPK):
        mx = jnp.max(s, axis=0, keepdims=True)
        cand = jnp.where(jnp.logical_and(s == mx, s > NEG_INF), row, scores_t.shape[0])
        hit = row == jnp.min(cand, axis=0, keepdims=True)
        picked = jnp.where(hit, 1.0, picked)
        s = jnp.where(hit, NEG_INF, s)
    return picked


def _moba_prompt_kernel(q_ref, kb_ref, vt_ref, km_ref, o_ref, sa_ref, sb_ref, *, n_blocks, hp):
    own = pl.program_id(2)
    blk = MOBA_BLOCK
    scale = A_HEAD_DIM ** -0.5
    nbp = -(-n_blocks // BF16_ROWS) * BF16_ROWS
    lane = lax.broadcasted_iota(jnp.int32, (blk, LANES), 1)
    blk_i = lax.broadcasted_iota(jnp.int32, (nbp, blk), 0)
    heads = [slice(h * A_HEAD_DIM, (h + 1) * A_HEAD_DIM) for h in range(hp)]
    vrows = [slice(h * VT_ROWS, (h + 1) * VT_ROWS) for h in range(hp)]

    q_t = [q_ref[:, sl].T for sl in heads]
    kms = []
    for sl in heads:
        km = km_ref[:, sl]
        if nbp > n_blocks:
            km = jnp.concatenate([km, jnp.zeros((nbp - n_blocks, A_HEAD_DIM), F32)], axis=0)
        kms.append(km)
    scores = [jnp.dot(km, qt, precision=_HIGHEST, preferred_element_type=F32) for km, qt in zip(kms, q_t)]
    q_ext = []
    for qt, sc in zip(q_t, scores):
        picked = _top_blocks_t(sc, blk_i < own)
        unmasked = jnp.where(blk_i == own, 1.0, picked)
        pad = jnp.zeros((LANES - nbp, blk), BF16)
        q_ext.append(jnp.concatenate([(qt * scale).astype(BF16), (1.0 - unmasked).astype(BF16), pad], axis=0))

    def block_logits(j):
        st = pl.multiple_of(j * blk, blk)
        mask_cols = jnp.where(lane == j, -MASK_BIG, 0.0).astype(BF16)
        return tuple(
            jnp.dot(jnp.concatenate([kb_ref[pl.ds(st, blk), sl], mask_cols], axis=1), qe, preferred_element_type=F32)
            for sl, qe in zip(heads, q_ext))

    def stage_logits(ref, j):
        for h, s in enumerate(block_logits(j)):
            ref[h] = s

    def accumulate(j, logits, state):
        probs, stats = [], []
        for s, (m, _) in zip(logits, state):
            m_new = jnp.maximum(m, jnp.max(s, axis=0, keepdims=True))
            probs.append(jnp.exp(s - m_new).astype(BF16))
            stats.append((m_new, jnp.exp(m - m_new)))
        return tuple(
            (m_new, alpha * acc + jnp.dot(vt_ref[j, vr, :], pb, preferred_element_type=F32))
            for vr, pb, (m_new, alpha), (_, acc) in zip(vrows, probs, stats, state))

    def accumulate_staged(j, ref, state):
        return accumulate(j, tuple(ref[h] for h in range(hp)), state)

    state = tuple((jnp.full((1, blk), -MASK_BIG, F32), jnp.zeros((VT_ROWS, blk), F32)) for _ in heads)

    @pl.when(own > 0)
    def _():
        stage_logits(sa_ref, 0)

    def pair(i, state):
        j = 2 * i
        stage_logits(sb_ref, j + 1)
        state = accumulate_staged(j, sa_ref, state)
        stage_logits(sa_ref, j + 2)
        return accumulate_staged(j + 1, sb_ref, state)

    state = lax.fori_loop(0, lax.shift_right_logical(own, 1), pair, state)
    state = lax.fori_loop(0, jnp.bitwise_and(own, 1), lambda _, st: accumulate_staged(own - 1, sa_ref, st), state)

    key_i = lax.broadcasted_iota(jnp.int32, (blk, blk), 0)
    qry_i = lax.broadcasted_iota(jnp.int32, (blk, blk), 1)
    logits = tuple(jnp.where(key_i <= qry_i, s, NEG_INF) for s in block_logits(own))
    state = accumulate(own, logits, state)
    for sl, (_, acc) in zip(heads, state):
        o_ref[:, sl] = (acc[0:A_HEAD_DIM, :] / acc[A_HEAD_DIM:A_HEAD_DIM + 1, :]).T


def _moba_prompt(q, kb, vt, kmean):
    b, s, e = q.shape
    nb = s // MOBA_BLOCK
    hp = MOBA_HEADS_PER_STEP
    assert s % MOBA_BLOCK == 0 and nb <= LANES and A_HEADS % hp == 0
    w = hp * A_HEAD_DIM
    tile = pl.BlockSpec((None, MOBA_BLOCK, w), lambda bi, h, qi: (bi, qi, h))
    return pl.pallas_call(
        functools.partial(_moba_prompt_kernel, n_blocks=nb, hp=hp),
        grid=(b, A_HEADS // hp, nb),
        in_specs=[tile, pl.BlockSpec((None, s, w), lambda bi, h, qi: (bi, 0, h)),
                  pl.BlockSpec((None, nb, hp * VT_ROWS, MOBA_BLOCK), lambda bi, h, qi: (bi, 0, h, 0)),
                  pl.BlockSpec((None, nb, w), lambda bi, h, qi: (bi, 0, h))],
        out_specs=tile,
        out_shape=jax.ShapeDtypeStruct((b, s, e), F32),
        scratch_shapes=[pltpu.VMEM((hp, MOBA_BLOCK, MOBA_BLOCK), F32)] * 2,
        compiler_params=_params("parallel", "parallel", "arbitrary"),
        name="moba_prompt",
    )(q, kb, vt, kmean)


def _moba_decode_kernel(pt_ref, q_ref, kn_ref, vn_ref, *refs, n_pages, page_rows):
    pg = PAGES_PER_STEP
    k_refs, v_refs = refs[:pg], refs[pg:2 * pg]
    o_ref = refs[2 * pg]
    qs_ref, logit_ref, ksum_ref, acc_ref, l_ref, pn_ref = refs[2 * pg + 1:]
    step = pl.program_id(1)
    n_k = n_pages // pg
    R = A_HEADS * DEC_PAD
    scale = A_HEAD_DIM ** -0.5
    pages_per_block = MOBA_BLOCK // page_rows
    n_blocks = n_pages // pages_per_block

    def rows_by_head(ref):
        return jnp.concatenate([ref[:, h * A_HEAD_DIM:(h + 1) * A_HEAD_DIM] for h in range(A_HEADS)], axis=0)

    @pl.when(step == 0)
    def _():
        qs_ref[...] = rows_by_head(q_ref)

    def by_head(ref):
        return jnp.concatenate([ref[pl.ds(h, page_rows, stride=A_HEADS), :].astype(BF16) for h in range(A_HEADS)],
                               axis=0)

    @pl.when(step < n_k)
    def _():
        qb = qs_ref[...].astype(BF16)
        for p in range(pg):
            g = step * pg + p
            lg = lax.dot_general(qb, by_head(k_refs[p]), _NT, preferred_element_type=F32)
            for h in range(A_HEADS):
                hr = slice(h * DEC_PAD, (h + 1) * DEC_PAD)
                logit_ref[g, hr, :] = lg[hr, h * page_rows:(h + 1) * page_rows] * scale
            part = jnp.sum(k_refs[p][...].reshape(page_rows, A_HEADS, A_HEAD_DIM), axis=0)
            bidx = g // pages_per_block
            if p % pages_per_block == 0:
                ksum_ref[bidx] = part
            else:
                ksum_ref[bidx] = ksum_ref[bidx] + part

    @pl.when(step == n_k - 1)
    def _():
        q = qs_ref[...]
        kmean = jnp.concatenate([ksum_ref[bi] for bi in range(n_blocks)], axis=0) * (1.0 / MOBA_BLOCK)
        kmean = jnp.concatenate([kmean, jnp.zeros((LANES - n_blocks * A_HEADS, A_HEAD_DIM), F32)], axis=0)
        scores = lax.dot_general(q, kmean, _NT, precision=_HIGHEST, preferred_element_type=F32)
        row = lax.broadcasted_iota(jnp.int32, (R, LANES), 0)
        lane = lax.broadcasted_iota(jnp.int32, (R, LANES), 1)
        row_head = row // DEC_PAD
        allowed = jnp.logical_and(lane % A_HEADS == row_head, lane < n_blocks * A_HEADS)
        picked = _top_blocks(scores, allowed, R)
        keep_blk = [jnp.sum(jnp.where(lane // A_HEADS == bi, picked, 0.0), axis=-1, keepdims=True) > 0.0
                    for bi in range(n_blocks)]

        kn = rows_by_head(kn_ref).astype(BF16)
        ln = lax.dot_general(q.astype(BF16), kn, _NT, preferred_element_type=F32) * scale
        rn = lax.broadcasted_iota(jnp.int32, (R, R), 0)
        cn = lax.broadcasted_iota(jnp.int32, (R, R), 1)
        ok_n = jnp.logical_and(rn // DEC_PAD == cn // DEC_PAD, cn % DEC_PAD <= rn % DEC_PAD)
        ln = jnp.where(ok_n, ln, NEG_INF)

        masked = [jnp.where(keep_blk[g // pages_per_block], logit_ref[g], NEG_INF) for g in range(n_pages)]
        mm = masked[0]
        for g in range(1, n_pages):
            mm = jnp.maximum(mm, masked[g])
        m = jnp.maximum(jnp.max(ln, axis=-1, keepdims=True), jnp.max(mm, axis=-1, keepdims=True))
        pn = jnp.exp(ln - m)
        lsum = jnp.zeros((R, page_rows), F32)
        for g in range(n_pages):
            pe = jnp.exp(masked[g] - m)
            lsum = lsum + pe
            logit_ref[g] = pe
        l = jnp.sum(pn, axis=-1, keepdims=True) + jnp.sum(lsum, axis=-1, keepdims=True)
        l_ref[...] = jnp.broadcast_to(l, l_ref.shape)
        pn_ref[...] = pn
        acc_ref[...] = jnp.zeros(acc_ref.shape, F32)

    @pl.when(step >= n_k)
    def _():
        acc = acc_ref[...]
        zero = jnp.zeros((DEC_PAD, page_rows), F32)
        for p in range(pg):
            g = (step - n_k) * pg + p
            pe = logit_ref[g]
            p_bd = jnp.concatenate(
                [jnp.concatenate([pe[h * DEC_PAD:(h + 1) * DEC_PAD, :] if hh == h else zero for hh in range(A_HEADS)],
                                 axis=1) for h in range(A_HEADS)], axis=0)
            acc = acc + jnp.dot(p_bd.astype(BF16), by_head(v_refs[p]), preferred_element_type=F32)
        acc_ref[...] = acc

    @pl.when(step == 2 * n_k - 1)
    def _():
        vn = rows_by_head(vn_ref).astype(BF16)
        acc = acc_ref[...] + jnp.dot(pn_ref[...].astype(BF16), vn, preferred_element_type=F32)
        res = acc / l_ref[:, 0:1]
        for h in range(A_HEADS):
            o_ref[:, h * A_HEAD_DIM:(h + 1) * A_HEAD_DIM] = res[h * DEC_PAD:(h + 1) * DEC_PAD, :]


def _moba_decode(q, kn, vn, cache_k, cache_v, page_table):
    b, t, e = q.shape
    n_phys, page_rows = cache_k.shape[0], cache_k.shape[1]
    n_pages = page_table.shape[1]
    pg = PAGES_PER_STEP
    n_k = n_pages // pg
    assert t == DEC_PAD and n_pages % pg == 0 and MOBA_BLOCK % page_rows == 0 and pg % (MOBA_BLOCK // page_rows) == 0
    cols = page_rows * A_HEADS
    ck = cache_k.reshape(n_phys, cols, A_HEAD_DIM)
    cv = cache_v.reshape(n_phys, cols, A_HEAD_DIM)
    pt = page_table.reshape(-1).astype(jnp.int32)
    R = A_HEADS * DEC_PAD

    tok = pl.BlockSpec((None, t, e), lambda bi, s, pt_: (bi, 0, 0))

    def k_spec(p):
        return pl.BlockSpec((None, cols, A_HEAD_DIM),
                            lambda bi, s, pt_: (pt_[bi * n_pages + jnp.minimum(s, n_k - 1) * pg + p], 0, 0))

    def v_spec(p):
        return pl.BlockSpec((None, cols, A_HEAD_DIM),
                            lambda bi, s, pt_: (pt_[bi * n_pages + jnp.maximum(s - n_k, 0) * pg + p], 0, 0))

    grid_spec = pltpu.PrefetchScalarGridSpec(
        num_scalar_prefetch=1,
        grid=(b, 2 * n_k),
        in_specs=[tok, tok, tok] + [k_spec(p) for p in range(pg)] + [v_spec(p) for p in range(pg)],
        out_specs=tok,
        scratch_shapes=[pltpu.VMEM((R, A_HEAD_DIM), F32),
                        pltpu.VMEM((n_pages, R, page_rows), F32),
                        pltpu.VMEM((n_pages * page_rows // MOBA_BLOCK, A_HEADS, A_HEAD_DIM), F32),
                        pltpu.VMEM((R, A_HEAD_DIM), F32),
                        pltpu.VMEM((R, LANES), F32),
                        pltpu.VMEM((R, R), F32)],
    )
    return pl.pallas_call(
        functools.partial(_moba_decode_kernel, n_pages=n_pages, page_rows=page_rows),
        grid_spec=grid_spec,
        out_shape=jax.ShapeDtypeStruct((b, t, e), F32),
        compiler_params=_params("parallel", "arbitrary"),
        name="moba_decode",
    )(pt, q, kn, vn, *([ck] * pg), *([cv] * pg))


def _trunk(x, valid, pos0, c0, n0, m0, past, w, tm):
    b, s, d = x.shape
    n = b * s
    tm = min(tm, n)
    assert n % tm == 0 and (tm % s == 0 or s % tm == 0)
    xf = x.reshape(n, d)

    q, k, v, o, gates = _norm_proj(xf, w["norm_mix"][0], [w["wq_m"], w["wk_m"], w["wv_m"], w["wo_m"], w["wgate_m"]], tm)
    r3 = lambda t: t.reshape(b, s, t.shape[-1])
    hs, c_new, n_new, m_new = _mlstm(r3(q), r3(k), r3(v), r3(gates), w["b_gate"], c0, n0, m0, valid)
    xf = _mlstm_out(hs.reshape(n, M_V_W), o, xf, w["norm_h_m"], w["w_out_m"], tm)
    xf = _ffn(xf, w["norm_ffn"][0], w["wg"][0], w["wu"][0], w["wd"][0], tm)

    pos = pos0 + jnp.arange(s, dtype=jnp.int32)
    cos, sin = _rope_tables(pos)
    if s < tm:
        cos, sin = jnp.tile(cos, (tm // s, 1)), jnp.tile(sin, (tm // s, 1))
    qa, ka, va, *blocks = _qkv(xf, w["norm_mix"][1], w["wq_a"], w["wk_a"], w["wv_a"], w["norm_q_a"],
                               w["norm_k_a"], cos, sin, tm, with_blocks=past is None)
    if past is None:
        kb, vt, kmean = blocks
        nb = s // MOBA_BLOCK
        att = _moba_prompt(r3(qa), r3(kb), vt.reshape(b, nb, -1, MOBA_BLOCK), kmean[:, 0, :].reshape(b, nb, -1))
    else:
        att = _moba_decode(r3(qa), r3(ka), r3(va), *past)
    xf = _proj_res(att.reshape(n, -1), xf, w["w_out_a"], tm)
    xf = _ffn(xf, w["norm_ffn"][1], w["wg"][1], w["wu"][1], w["wd"][1], tm)

    heads = lambda t: t.reshape(b, s, A_HEADS, A_HEAD_DIM)[:, :valid]
    return xf.reshape(b, s, d)[:, :valid], heads(ka), heads(va), c_new, n_new, m_new


def kernel(x_prompt, x_sample, state_C, state_n, state_m, cache_k, cache_v, page_table, norm_mix, norm_ffn,
           w_in_m, b_gate_m, norm_h_m, w_out_m, w_qkv_a, norm_q_a, norm_k_a, w_out_a, w_gu, w_down):
    assert norm_mix.shape[0] == 2 and w_in_m.shape[0] == 1 and w_qkv_a.shape[0] == 1
    bf = lambda t: t.astype(BF16)
    row = lambda t: t.reshape(1, -1).astype(F32)
    w_in = w_in_m[0]
    e_a = A_HEADS * A_HEAD_DIM
    gate_w = jnp.pad(w_in[:, 2 * M_QK_W + 2 * M_V_W:], ((0, 0), (0, LANES - 2 * M_HEADS)))
    w = {
        "norm_mix": [row(norm_mix[0]), row(norm_mix[1])],
        "norm_ffn": [row(norm_ffn[0]), row(norm_ffn[1])],
        "wq_m": bf(w_in[:, :M_QK_W]), "wk_m": bf(w_in[:, M_QK_W:2 * M_QK_W]),
        "wv_m": bf(w_in[:, 2 * M_QK_W:2 * M_QK_W + M_V_W]),
        "wo_m": bf(w_in[:, 2 * M_QK_W + M_V_W:2 * M_QK_W + 2 * M_V_W]),
        "wgate_m": bf(gate_w),
        "b_gate": jnp.pad(b_gate_m[0].astype(F32), (0, LANES - 2 * M_HEADS)).reshape(1, LANES),
        "norm_h_m": row(norm_h_m[0]), "w_out_m": bf(w_out_m[0]),
        "wq_a": bf(w_qkv_a[0][:, :e_a]), "wk_a": bf(w_qkv_a[0][:, e_a:2 * e_a]), "wv_a": bf(w_qkv_a[0][:, 2 * e_a:]),
        "norm_q_a": row(norm_q_a[0]), "norm_k_a": row(norm_k_a[0]), "w_out_a": bf(w_out_a[0]),
        "wg": [bf(w_gu[i][:, :FFN_HIDDEN]) for i in range(2)],
        "wu": [bf(w_gu[i][:, FFN_HIDDEN:]) for i in range(2)],
        "wd": [bf(w_down[i]) for i in range(2)],
    }

    def state(c, nn, mm):
        bsz = c.shape[0]
        m_row = jnp.pad(mm.astype(F32), ((0, 0), (0, LANES - M_HEADS))).reshape(bsz, 1, LANES)
        return c.astype(F32), nn.astype(F32).reshape(bsz, M_HEADS, 1, M_QK_DIM), m_row

    def unstate(c, nn, mm, dt):
        bsz = c.shape[0]
        return (c[None].astype(dt), nn.reshape(1, bsz, M_HEADS, M_QK_DIM).astype(dt),
                mm.reshape(bsz, LANES)[None, :, :M_HEADS].astype(dt))

    sd = state_C.dtype
    bp, sp, _ = x_prompt.shape
    zc = jnp.zeros((bp,) + state_C.shape[2:], F32)
    zn = jnp.zeros((bp,) + state_n.shape[2:], F32)
    zm = jnp.zeros((bp,) + state_m.shape[2:], F32)
    yp, pk, pv, pc, pn, pm = _trunk(x_prompt, sp, 0, *state(zc, zn, zm), None, w, 512)

    bs, ss, _ = x_sample.shape
    xs = jnp.pad(x_sample, ((0, 0), (0, DEC_PAD - ss), (0, 0)))
    past_len = page_table.shape[1] * cache_k.shape[2]
    ys, sk, sv, sc, sn, sm = _trunk(xs, ss, past_len, *state(state_C[0], state_n[0], state_m[0]),
                                    (cache_k[0], cache_v[0], page_table), w, 512)

    return (yp, ys, pk[None], pv[None], *unstate(pc, pn, pm, sd), sk[None], sv[None], *unstate(sc, sn, sm, sd))
```

```python
import functools
import math

import jax
import jax.numpy as jnp
from jax import lax
from jax.experimental import pallas as pl
from jax.experimental.pallas import tpu as pltpu

F32 = jnp.float32
BF16 = jnp.bfloat16

D_MODEL = 1024
M_HEADS = 4
M_QK_DIM = 128
M_V_DIM = 256
M_QK_W = M_HEADS * M_QK_DIM
M_V_W = M_HEADS * M_V_DIM
A_HEADS = 8
A_HEAD_DIM = 128
MOBA_BLOCK = 256
MOBA_TOPK = 3
ROPE_THETA = 10000.0
FFN_HIDDEN = 2816
EPS = 1e-6

LANES = 128
SUBLANES = 8
BF16_ROWS = 16
VT_ROWS = A_HEAD_DIM + BF16_ROWS
VMEM_LIMIT_BYTES = 56 * 1024 * 1024

MLSTM_CHUNK = 128
DEC_PAD = SUBLANES
PAGES_PER_STEP = 8
NEG_INF = float("-inf")

_HIGHEST = lax.Precision.HIGHEST
_NT = (((1,), (1,)), ((), ()))
_TN = (((0,), (0,)), ((), ()))


def _params(*sem):
    return pltpu.CompilerParams(dimension_semantics=sem, vmem_limit_bytes=VMEM_LIMIT_BYTES)


def _const_spec(shape):
    nd = len(shape)
    return pl.BlockSpec(shape, lambda *_: (0,) * nd, pipeline_mode=pl.Buffered(1))


def _rms(x, g):
    return x * lax.rsqrt(jnp.mean(x * x, axis=-1, keepdims=True) + EPS) * g


def _norm_proj_kernel(x_ref, g_ref, *refs, n_out):
    w_refs, o_refs = refs[:n_out], refs[n_out:]
    hb = _rms(x_ref[...], g_ref[...]).astype(BF16)
    for w_ref, o_ref in zip(w_refs, o_refs):
        o_ref[...] = jnp.dot(hb, w_ref[...], preferred_element_type=F32)


def _norm_proj(x, g, ws, tm):
    n, d = x.shape
    return pl.pallas_call(
        functools.partial(_norm_proj_kernel, n_out=len(ws)),
        grid=(n // tm,),
        in_specs=[pl.BlockSpec((tm, d), lambda i: (i, 0)), _const_spec((1, d))]
        + [_const_spec(w.shape) for w in ws],
        out_specs=[pl.BlockSpec((tm, w.shape[1]), lambda i: (i, 0)) for w in ws],
        out_shape=[jax.ShapeDtypeStruct((n, w.shape[1]), F32) for w in ws],
        compiler_params=_params("parallel"),
        name="norm_proj",
    )(x, g, *ws)


def _mlstm_kernel(q_ref, k_ref, v_ref, gt_ref, bias_ref, c0_ref, n0_ref, m0_ref,
                  hs_ref, c_ref, n_ref, m_ref, *pad_refs, rows, valid, chunk):
    ci = pl.program_id(1)

    @pl.when(ci == 0)
    def _():
        c_ref[...] = c0_ref[...]
        n_ref[...] = n0_ref[...]
        m_ref[...] = m0_ref[...]

    if rows == chunk:
        q_all, k_all, v_all, gt_all = q_ref[...], k_ref[...], v_ref[...], gt_ref[...]
    else:
        qp, kp, vp, gp = pad_refs

        @pl.when(jnp.logical_and(pl.program_id(0) == 0, ci == 0))
        def _():
            for p in pad_refs:
                p[...] = jnp.zeros(p.shape, p.dtype)

        qp[0:rows, :], kp[0:rows, :], vp[0:rows, :], gp[0:rows, :] = (
            q_ref[...], k_ref[...], v_ref[...], gt_ref[...])
        q_all, k_all, v_all, gt_all = qp[...], kp[...], vp[...], gp[...]

    L = chunk
    row_l = lax.broadcasted_iota(jnp.int32, (L, LANES), 0)
    lane_l = lax.broadcasted_iota(jnp.int32, (L, LANES), 1)
    is_ig = lane_l < M_HEADS
    real = row_l < valid
    gc = gt_all + bias_ref[...]
    lfc = jnp.where(real, jax.nn.log_sigmoid(gc), 0.0)
    igc = jnp.where(real, gc, NEG_INF)
    gsel = jnp.where(is_ig, igc, lfc)

    eye8 = (lax.broadcasted_iota(jnp.int32, (SUBLANES, LANES), 0)
            == lax.broadcasted_iota(jnp.int32, (SUBLANES, LANES), 1)).astype(F32)
    gsel_fin = jnp.where(real, gsel, 0.0)
    rows_g = lax.dot_general(eye8, gsel_fin, _NT, precision=_HIGHEST, preferred_element_type=F32)
    col_real = lax.broadcasted_iota(jnp.int32, (SUBLANES, L), 1) < valid
    rows_is_ig = lax.broadcasted_iota(jnp.int32, (SUBLANES, L), 0) < M_HEADS
    rows_g = jnp.where(jnp.logical_and(rows_is_ig, jnp.logical_not(col_real)), NEG_INF, rows_g)

    tt = lax.broadcasted_iota(jnp.int32, (L, L), 0)
    ss = lax.broadcasted_iota(jnp.int32, (L, L), 1)
    causal = ss <= tt
    tril = causal.astype(F32)
    triu = (tt <= ss).astype(F32)
    b_cols = jnp.dot(tril, lfc, precision=_HIGHEST, preferred_element_type=F32)
    lf_rows = jnp.where(rows_is_ig, 0.0, rows_g)
    b_rows = jnp.dot(lf_rows, triu, precision=_HIGHEST, preferred_element_type=F32)

    m_prev_row = m_ref[0]
    lane_1 = lax.broadcasted_iota(jnp.int32, (1, LANES), 1)
    m_new_row = m_prev_row
    scale = M_QK_DIM ** -0.5
    hd = range(M_HEADS)
    qs = [q_all[:, h * M_QK_DIM:(h + 1) * M_QK_DIM] * scale for h in hd]
    ks = [k_all[:, h * M_QK_DIM:(h + 1) * M_QK_DIM] for h in hd]
    qbs = [q.astype(BF16) for q in qs]
    kbs = [k.astype(BF16) for k in ks]
    vbs = [v_all[:, h * M_V_DIM:(h + 1) * M_V_DIM].astype(BF16) for h in hd]
    c_olds = [c_ref[0, h] for h in hd]
    n_olds = [n_ref[0, h] for h in hd]
    b_cs = [b_cols[:, M_HEADS + h:M_HEADS + h + 1] for h in hd]
    m_prevs = [m_prev_row[:, h:h + 1] for h in hd]

    qk = [lax.dot_general(qb, kb, _NT, preferred_element_type=F32) for qb, kb in zip(qbs, kbs)]
    qc = [jnp.dot(qb, c.astype(BF16), preferred_element_type=F32) for qb, c in zip(qbs, c_olds)]

    a_s, m_ts, w_inters = [], [], []
    for h in hd:
        d = jnp.where(causal, b_cs[h] - b_rows[M_HEADS + h:M_HEADS + h + 1, :] + rows_g[h:h + 1, :], NEG_INF)
        m_inter = b_cs[h] + m_prevs[h]
        m_t = jnp.maximum(m_inter, jnp.max(d, axis=-1, keepdims=True))
        a_s.append(qk[h] * jnp.exp(d - m_t))
        m_ts.append(m_t)
        w_inters.append(jnp.exp(m_inter - m_t))
    av = [jnp.dot(a.astype(BF16), vb, preferred_element_type=F32) for a, vb in zip(a_s, vbs)]

    kws, w_states = [], []
    for h in hd:
        num = w_inters[h] * qc[h] + av[h]
        den = (w_inters[h] * jnp.sum(qs[h] * n_olds[h], axis=-1, keepdims=True)
               + jnp.sum(a_s[h], axis=-1, keepdims=True))
        hval = num / jnp.maximum(jnp.abs(den), jnp.exp(-m_ts[h]))
        hs_ref[:, h * M_V_DIM:(h + 1) * M_V_DIM] = hval[0:rows, :]
        m_new = m_ts[h][L - 1:L, :]
        b_last = b_cs[h][L - 1:L, :]
        w_states.append(jnp.exp(b_last + m_prevs[h] - m_new))
        kws.append(ks[h] * jnp.exp(b_last - b_cs[h] + gsel[:, h:h + 1] - m_new))
        m_new_row = jnp.where(lane_1 == h, m_new, m_new_row)
    kv = [lax.dot_general(kw.astype(BF16), vb, _TN, preferred_element_type=F32) for kw, vb in zip(kws, vbs)]
    for h in hd:
        c_ref[0, h] = w_states[h] * c_olds[h] + kv[h]
        n_ref[0, h] = w_states[h] * n_olds[h] + jnp.sum(kws[h], axis=0, keepdims=True)
    m_ref[0] = m_new_row


def _mlstm(q, k, v, gates, bias, c0, n0, m0, valid):
    b, s, _ = q.shape
    if s >= MLSTM_CHUNK:
        rows, nc = MLSTM_CHUNK, s // MLSTM_CHUNK
        assert valid == s and s % MLSTM_CHUNK == 0
        scratch = []
    else:
        rows, nc = s, 1
        scratch = [pltpu.VMEM((MLSTM_CHUNK, w), F32) for w in (M_QK_W, M_QK_W, M_V_W, LANES)]
    tok = lambda w: pl.BlockSpec((None, rows, w), lambda bi, ci: (bi, ci, 0))
    st_c = pl.BlockSpec((1, M_HEADS, M_QK_DIM, M_V_DIM), lambda bi, ci: (bi, 0, 0, 0))
    st_n = pl.BlockSpec((1, M_HEADS, 1, M_QK_DIM), lambda bi, ci: (bi, 0, 0, 0))
    st_m = pl.BlockSpec((1, 1, LANES), lambda bi, ci: (bi, 0, 0))
    return pl.pallas_call(
        functools.partial(_mlstm_kernel, rows=rows, valid=valid, chunk=MLSTM_CHUNK),
        grid=(b, nc),
        in_specs=[tok(M_QK_W), tok(M_QK_W), tok(M_V_W), tok(LANES), _const_spec((1, LANES)), st_c, st_n, st_m],
        out_specs=[tok(M_V_W), st_c, st_n, st_m],
        out_shape=[jax.ShapeDtypeStruct((b, s, M_V_W), F32), jax.ShapeDtypeStruct(c0.shape, F32),
                   jax.ShapeDtypeStruct(n0.shape, F32), jax.ShapeDtypeStruct(m0.shape, F32)],
        scratch_shapes=scratch,
        compiler_params=_params("arbitrary" if scratch else "parallel", "arbitrary"),
        name="mlstm_chunks",
    )(q, k, v, gates, bias, c0, n0, m0)


def _mlstm_out_kernel(hs_ref, o_ref, x_ref, g_ref, w_ref, out_ref):
    parts = []
    for h in range(M_HEADS):
        sl = slice(h * M_V_DIM, (h + 1) * M_V_DIM)
        parts.append(_rms(hs_ref[:, sl], g_ref[:, sl]))
    y = jnp.concatenate(parts, axis=-1) * jax.nn.sigmoid(o_ref[...])
    out_ref[...] = x_ref[...] + jnp.dot(y.astype(BF16), w_ref[...], preferred_element_type=F32)


def _mlstm_out(hs, o, x, g, w, tm):
    n, d = x.shape
    row = lambda wd: pl.BlockSpec((tm, wd), lambda i: (i, 0))
    return pl.pallas_call(
        _mlstm_out_kernel,
        grid=(n // tm,),
        in_specs=[row(M_V_W), row(M_V_W), row(d), _const_spec((1, M_V_W)), _const_spec(w.shape)],
        out_specs=row(d),
        out_shape=jax.ShapeDtypeStruct((n, d), F32),
        compiler_params=_params("parallel"),
        name="mlstm_out",
    )(hs, o, x, g, w)


def _proj_res_kernel(a_ref, x_ref, w_ref, out_ref):
    out_ref[...] = x_ref[...] + jnp.dot(a_ref[...].astype(BF16), w_ref[...], preferred_element_type=F32)


def _proj_res(a, x, w, tm):
    n, d = x.shape
    return pl.pallas_call(
        _proj_res_kernel,
        grid=(n // tm,),
        in_specs=[pl.BlockSpec((tm, a.shape[1]), lambda i: (i, 0)), pl.BlockSpec((tm, d), lambda i: (i, 0)),
                  _const_spec(w.shape)],
        out_specs=pl.BlockSpec((tm, d), lambda i: (i, 0)),
        out_shape=jax.ShapeDtypeStruct((n, d), F32),
        compiler_params=_params("parallel"),
        name="proj_res",
    )(a, x, w)


FFN_CHUNKS = 2


def _ffn_kernel(x_ref, g_ref, wg_ref, wu_ref, wd_ref, out_ref):
    x = x_ref[...]
    hb = _rms(x, g_ref[...]).astype(BF16)
    acc = x
    cw = FFN_HIDDEN // FFN_CHUNKS
    for c in range(FFN_CHUNKS):
        sl = slice(c * cw, (c + 1) * cw)
        gg = jnp.dot(hb, wg_ref[:, sl], preferred_element_type=F32)
        uu = jnp.dot(hb, wu_ref[:, sl], preferred_element_type=F32)
        act = (jax.nn.silu(gg) * uu).astype(BF16)
        acc = acc + jnp.dot(act, wd_ref[sl, :], preferred_element_type=F32)
    out_ref[...] = acc


def _ffn(x, g, wg, wu, wd, tm):
    n, d = x.shape
    return pl.pallas_call(
        _ffn_kernel,
        grid=(n // tm,),
        in_specs=[pl.BlockSpec((tm, d), lambda i: (i, 0)), _const_spec((1, d)),
                  _const_spec(wg.shape), _const_spec(wu.shape), _const_spec(wd.shape)],
        out_specs=pl.BlockSpec((tm, d), lambda i: (i, 0)),
        out_shape=jax.ShapeDtypeStruct((n, d), F32),
        compiler_params=_params("parallel"),
        name="ffn",
    )(x, g, wg, wu, wd)


def _qkv_kernel(x_ref, g_ref, wq_ref, wk_ref, wv_ref, gq_ref, gk_ref, cos_ref, sin_ref,
                q_ref, k_ref, v_ref, *blk_refs):
    hb = _rms(x_ref[...], g_ref[...]).astype(BF16)
    cos, sin = cos_ref[...], sin_ref[...]

    kb_ref, vt_ref, km_ref = blk_refs if blk_refs else (None, None, None)
    tm = x_ref.shape[0]
    rt = min(tm, MOBA_BLOCK)
    cw = 2 * A_HEAD_DIM

    def norm_rope(t, gh_ref, rows):
        y = _rms(t, gh_ref[...])
        return y * cos[rows, :] + pltpu.roll(y, A_HEAD_DIM // 2, 1) * sin[rows, :]

    for j in range(tm // rt):
        rows = slice(j * rt, (j + 1) * rt)
        hr = hb[rows, :]
        for c in range(A_HEADS // 2):
            t = jnp.dot(hr, wq_ref[:, c * cw:(c + 1) * cw], preferred_element_type=F32)
            for i in range(2):
                sl = slice((2 * c + i) * A_HEAD_DIM, (2 * c + i + 1) * A_HEAD_DIM)
                q_ref[rows, sl] = norm_rope(t[:, i * A_HEAD_DIM:(i + 1) * A_HEAD_DIM], gq_ref, rows)
        for c in range(A_HEADS // 2):
            t = jnp.dot(hr, wk_ref[:, c * cw:(c + 1) * cw], preferred_element_type=F32)
            for i in range(2):
                sl = slice((2 * c + i) * A_HEAD_DIM, (2 * c + i + 1) * A_HEAD_DIM)
                k = norm_rope(t[:, i * A_HEAD_DIM:(i + 1) * A_HEAD_DIM], gk_ref, rows)
                k_ref[rows, sl] = k
                if blk_refs:
                    kb_ref[rows, sl] = k.astype(BF16)
                    mean = jnp.mean(k, axis=0, keepdims=True)
                    km_ref[j, :, sl] = jnp.broadcast_to(mean, (SUBLANES, A_HEAD_DIM))
        for c in range(A_HEADS // 2):
            t = jnp.dot(hr, wv_ref[:, c * cw:(c + 1) * cw], preferred_element_type=F32)
            v_ref[rows, c * cw:(c + 1) * cw] = t
            if blk_refs:
                for i in range(2):
                    r0 = (2 * c + i) * VT_ROWS
                    vt_ref[j, r0:r0 + A_HEAD_DIM, :] = t[:, i * A_HEAD_DIM:(i + 1) * A_HEAD_DIM].T.astype(BF16)
                    vt_ref[j, r0 + A_HEAD_DIM:r0 + VT_ROWS, :] = jnp.ones((BF16_ROWS, MOBA_BLOCK), BF16)


def _qkv(x, g, wq, wk, wv, gq, gk, cos, sin, tm, with_blocks):
    n, d = x.shape
    e = wq.shape[1]
    n_tab = cos.shape[0] // tm
    row = lambda wd: pl.BlockSpec((tm, wd), lambda i: (i, 0))
    tab = pl.BlockSpec((tm, A_HEAD_DIM), lambda i: (i % n_tab, 0))
    out_specs = [row(e), row(e), row(e)]
    out_shape = [jax.ShapeDtypeStruct((n, e), F32)] * 3
    if with_blocks:
        assert tm % MOBA_BLOCK == 0
        gpt = tm // MOBA_BLOCK
        vrows = A_HEADS * VT_ROWS
        out_specs += [row(e), pl.BlockSpec((gpt, vrows, MOBA_BLOCK), lambda i: (i, 0, 0)),
                      pl.BlockSpec((gpt, SUBLANES, e), lambda i: (i, 0, 0))]
        out_shape += [jax.ShapeDtypeStruct((n, e), BF16),
                      jax.ShapeDtypeStruct((n // MOBA_BLOCK, vrows, MOBA_BLOCK), BF16),
                      jax.ShapeDtypeStruct((n // MOBA_BLOCK, SUBLANES, e), F32)]
    return pl.pallas_call(
        _qkv_kernel,
        grid=(n // tm,),
        in_specs=[row(d), _const_spec((1, d)), _const_spec(wq.shape), _const_spec(wk.shape),
                  _const_spec(wv.shape), _const_spec((1, A_HEAD_DIM)), _const_spec((1, A_HEAD_DIM)), tab, tab],
        out_specs=out_specs,
        out_shape=out_shape,
        compiler_params=_params("parallel"),
        name="qkv_rope",
    )(x, g, wq, wk, wv, gq, gk, cos, sin)


def _rope_tables(pos):
    half = A_HEAD_DIM // 2
    inv_freq = ROPE_THETA ** (-jnp.arange(half, dtype=F32) / half)
    ang = pos.astype(F32)[:, None] * inv_freq[None, :]
    c, s = jnp.cos(ang), jnp.sin(ang)
    return jnp.concatenate([c, c], axis=-1), jnp.concatenate([-s, s], axis=-1)


MOBA_HEADS_PER_STEP = 4
MASK_BIG = 2.0 ** 100


def _top_blocks_t(scores_t, allowed):
    row = lax.broadcasted_iota(jnp.int32, scores_t.shape, 0)
    s = jnp.where(allowed, scores_t, NEG_INF)
    picked = jnp.zeros(scores_t.shape, F32)
    for _ in range(MOBA_TOPK):
        mx = jnp.max(s, axis=0, keepdims=True)
        cand = jnp.where(jnp.logical_and(s == mx, s > NEG_INF), row, scores_t.shape[0])
        hit = row == jnp.min(cand, axis=0, keepdims=True)
        picked = jnp.where(hit, 1.0, picked)
        s = jnp.where(hit, NEG_INF, s)
    return picked


def _moba_prompt_kernel(q_ref, kb_ref, vt_ref, km_ref, o_ref, sa_ref, sb_ref, *, n_blocks, hp):
    own = pl.program_id(2)
    blk = MOBA_BLOCK
    scale = A_HEAD_DIM ** -0.5
    nbp = -(-n_blocks // BF16_ROWS) * BF16_ROWS
    lane = lax.broadcasted_iota(jnp.int32, (blk, LANES), 1)
    blk_i = lax.broadcasted_iota(jnp.int32, (nbp, blk), 0)
    heads = [slice(h * A_HEAD_DIM, (h + 1) * A_HEAD_DIM) for h in range(hp)]
    vrows = [slice(h * VT_ROWS, (h + 1) * VT_ROWS) for h in range(hp)]

    q_t = [q_ref[:, sl].T for sl in heads]
    kms = []
    for sl in heads:
        km = km_ref[:, sl]
        if nbp > n_blocks:
            km = jnp.concatenate([km, jnp.zeros((nbp - n_blocks, A_HEAD_DIM), F32)], axis=0)
        kms.append(km)
    scores = [jnp.dot(km, qt, precision=_HIGHEST, preferred_element_type=F32) for km, qt in zip(kms, q_t)]
    q_ext = []
    for qt, sc in zip(q_t, scores):
        picked = _top_blocks_t(sc, blk_i < own)
        unmasked = jnp.where(blk_i == own, 1.0, picked)
        pad = jnp.zeros((LANES - nbp, blk), BF16)
        q_ext.append(jnp.concatenate([(qt * scale).astype(BF16), (1.0 - unmasked).astype(BF16), pad], axis=0))

    def block_logits(j):
        st = pl.multiple_of(j * blk, blk)
        mask_cols = jnp.where(lane == j, -MASK_BIG, 0.0).astype(BF16)
        return tuple(
            jnp.dot(jnp.concatenate([kb_ref[pl.ds(st, blk), sl], mask_cols], axis=1), qe, preferred_element_type=F32)
            for sl, qe in zip(heads, q_ext))

    def stage_logits(ref, j):
        for h, s in enumerate(block_logits(j)):
            ref[h] = s

    def accumulate(j, logits, state):
        probs, stats = [], []
        for s, (m, _) in zip(logits, state):
            m_new = jnp.maximum(m, jnp.max(s, axis=0, keepdims=True))
            probs.append(jnp.exp(s - m_new).astype(BF16))
            stats.append((m_new, jnp.exp(m - m_new)))
        return tuple(
            (m_new, alpha * acc + jnp.dot(vt_ref[j, vr, :], pb, preferred_element_type=F32))
            for vr, pb, (m_new, alpha), (_, acc) in zip(vrows, probs, stats, state))

    def accumulate_staged(j, ref, state):
        return accumulate(j, tuple(ref[h] for h in range(hp)), state)

    state = tuple((jnp.full((1, blk), -MASK_BIG, F32), jnp.zeros((VT_ROWS, blk), F32)) for _ in heads)

    @pl.when(own > 0)
    def _():
        stage_logits(sa_ref, 0)

    def pair(i, state):
        j = 2 * i
        stage_logits(sb_ref, j + 1)
        state = accumulate_staged(j, sa_ref, state)
        stage_logits(sa_ref, j + 2)
        return accumulate_staged(j + 1, sb_ref, state)

    state = lax.fori_loop(0, lax.shift_right_logical(own, 1), pair, state)
    state = lax.fori_loop(0, jnp.bitwise_and(own, 1), lambda _, st: accumulate_staged(own - 1, sa_ref, st), state)

    key_i = lax.broadcasted_iota(jnp.int32, (blk, blk), 0)
    qry_i = lax.broadcasted_iota(jnp.int32, (blk, blk), 1)
    logits = tuple(jnp.where(key_i <= qry_i, s, NEG_INF) for s in block_logits(own))
    state = accumulate(own, logits, state)
    for sl, (_, acc) in zip(heads, state):
        o_ref[:, sl] = (acc[0:A_HEAD_DIM, :] / acc[A_HEAD_DIM:A_HEAD_DIM + 1, :]).T


def _moba_prompt(q, kb, vt, kmean):
    b, s, e = q.shape
    nb = s // MOBA_BLOCK
    hp = MOBA_HEADS_PER_STEP
    assert s % MOBA_BLOCK == 0 and nb <= LANES and A_HEADS % hp == 0
    w = hp * A_HEAD_DIM
    tile = pl.BlockSpec((None, MOBA_BLOCK, w), lambda bi, h, qi: (bi, qi, h))
    return pl.pallas_call(
        functools.partial(_moba_prompt_kernel, n_blocks=nb, hp=hp),
        grid=(b, A_HEADS // hp, nb),
        in_specs=[tile, pl.BlockSpec((None, s, w), lambda bi, h, qi: (bi, 0, h)),
                  pl.BlockSpec((None, nb, hp * VT_ROWS, MOBA_BLOCK), lambda bi, h, qi: (bi, 0, h, 0)),
                  pl.BlockSpec((None, nb, w), lambda bi, h, qi: (bi, 0, h))],
        out_specs=tile,
        out_shape=jax.ShapeDtypeStruct((b, s, e), F32),
        scratch_shapes=[pltpu.VMEM((hp, MOBA_BLOCK, MOBA_BLOCK), F32)] * 2,
        compiler_params=_params("parallel", "parallel", "arbitrary"),
        name="moba_prompt",
    )(q, kb, vt, kmean)


def _moba_decode_kernel(pt_ref, q_ref, kn_ref, vn_ref, *refs, n_pages, page_rows):
    pg = PAGES_PER_STEP
    k_refs, v_refs = refs[:pg], refs[pg:2 * pg]
    o_ref = refs[2 * pg]
    qs_ref, logit_ref, ksum_ref, acc_ref, l_ref, pn_ref = refs[2 * pg + 1:]
    step = pl.program_id(1)
    n_k = n_pages // pg
    R = A_HEADS * DEC_PAD
    scale = A_HEAD_DIM ** -0.5
    pages_per_block = MOBA_BLOCK // page_rows
    n_blocks = n_pages // pages_per_block

    def rows_by_head(ref):
        return jnp.concatenate([ref[:, h * A_HEAD_DIM:(h + 1) * A_HEAD_DIM] for h in range(A_HEADS)], axis=0)

    @pl.when(step == 0)
    def _():
        qs_ref[...] = rows_by_head(q_ref)

    def by_head(ref):
        return jnp.concatenate([ref[pl.ds(h, page_rows, stride=A_HEADS), :].astype(BF16) for h in range(A_HEADS)],
                               axis=0)

    @pl.when(step < n_k)
    def _():
        qb = qs_ref[...].astype(BF16)
        for p in range(pg):
            g = step * pg + p
            lg = lax.dot_general(qb, by_head(k_refs[p]), _NT, preferred_element_type=F32)
            for h in range(A_HEADS):
                hr = slice(h * DEC_PAD, (h + 1) * DEC_PAD)
                logit_ref[g, hr, :] = lg[hr, h * page_rows:(h + 1) * page_rows] * scale
            part = jnp.sum(k_refs[p][...].reshape(page_rows, A_HEADS, A_HEAD_DIM), axis=0)
            bidx = g // pages_per_block
            if p % pages_per_block == 0:
                ksum_ref[bidx] = part
            else:
                ksum_ref[bidx] = ksum_ref[bidx] + part

    @pl.when(step == n_k - 1)
    def _():
        q = qs_ref[...]
        kmean = jnp.concatenate([ksum_ref[bi] for bi in range(n_blocks)], axis=0) * (1.0 / MOBA_BLOCK)
        nbh = n_blocks * A_HEADS
        scores_t = lax.dot_general(kmean, q, _NT, precision=_HIGHEST, preferred_element_type=F32)
        brow = lax.broadcasted_iota(jnp.int32, (nbh, R), 0)
        qcol = lax.broadcasted_iota(jnp.int32, (nbh, R), 1)
        picked_t = _top_blocks_t(scores_t, brow % A_HEADS == qcol // DEC_PAD)
        erow = lax.broadcasted_iota(jnp.int32, (nbh, n_blocks * page_rows), 0)
        ecol = lax.broadcasted_iota(jnp.int32, (nbh, n_blocks * page_rows), 1)
        expand = jnp.where(erow // A_HEADS == ecol // page_rows, 1.0, 0.0).astype(BF16)
        keep = lax.dot_general(picked_t.astype(BF16), expand, _TN, preferred_element_type=F32)
        keep_blk = [keep[:, bi * page_rows:(bi + 1) * page_rows] > 0.5 for bi in range(n_blocks)]

        kn = rows_by_head(kn_ref).astype(BF16)
        ln = lax.dot_general(q.astype(BF16), kn, _NT, preferred_element_type=F32) * scale
        rn = lax.broadcasted_iota(jnp.int32, (R, R), 0)
        cn = lax.broadcasted_iota(jnp.int32, (R, R), 1)
        ok_n = jnp.logical_and(rn // DEC_PAD == cn // DEC_PAD, cn % DEC_PAD <= rn % DEC_PAD)
        ln = jnp.where(ok_n, ln, NEG_INF)

        masked = [jnp.where(keep_blk[g // pages_per_block], logit_ref[g], NEG_INF) for g in range(n_pages)]
        mm = masked[0]
        for g in range(1, n_pages):
            mm = jnp.maximum(mm, masked[g])
        m = jnp.maximum(jnp.max(ln, axis=-1, keepdims=True), jnp.max(mm, axis=-1, keepdims=True))
        pn = jnp.exp(ln - m)
        lsum = jnp.zeros((R, page_rows), F32)
        for g in range(n_pages):
            pe = jnp.exp(masked[g] - m)
            lsum = lsum + pe
            logit_ref[g] = pe
        l = jnp.sum(pn, axis=-1, keepdims=True) + jnp.sum(lsum, axis=-1, keepdims=True)
        l_ref[...] = jnp.broadcast_to(l, l_ref.shape)
        pn_ref[...] = pn
        acc_ref[...] = jnp.zeros(acc_ref.shape, F32)

    @pl.when(step >= n_k)
    def _():
        acc = acc_ref[...]
        zero = jnp.zeros((DEC_PAD, page_rows), F32)
        for p in range(pg):
            g = (step - n_k) * pg + p
            pe = logit_ref[g]
            p_bd = jnp.concatenate(
                [jnp.concatenate([pe[h * DEC_PAD:(h + 1) * DEC_PAD, :] if hh == h else zero for hh in range(A_HEADS)],
                                 axis=1) for h in range(A_HEADS)], axis=0)
            acc = acc + jnp.dot(p_bd.astype(BF16), by_head(v_refs[p]), preferred_element_type=F32)
        acc_ref[...] = acc

    @pl.when(step == 2 * n_k - 1)
    def _():
        vn = rows_by_head(vn_ref).astype(BF16)
        acc = acc_ref[...] + jnp.dot(pn_ref[...].astype(BF16), vn, preferred_element_type=F32)
        res = acc / l_ref[:, 0:1]
        for h in range(A_HEADS):
            o_ref[:, h * A_HEAD_DIM:(h + 1) * A_HEAD_DIM] = res[h * DEC_PAD:(h + 1) * DEC_PAD, :]


def _moba_decode(q, kn, vn, cache_k, cache_v, page_table):
    b, t, e = q.shape
    n_phys, page_rows = cache_k.shape[0], cache_k.shape[1]
    n_pages = page_table.shape[1]
    pg = PAGES_PER_STEP
    n_k = n_pages // pg
    assert t == DEC_PAD and n_pages % pg == 0 and MOBA_BLOCK % page_rows == 0 and pg % (MOBA_BLOCK // page_rows) == 0
    cols = page_rows * A_HEADS
    ck = cache_k.reshape(n_phys, cols, A_HEAD_DIM)
    cv = cache_v.reshape(n_phys, cols, A_HEAD_DIM)
    pt = page_table.reshape(-1).astype(jnp.int32)
    R = A_HEADS * DEC_PAD

    tok = pl.BlockSpec((None, t, e), lambda bi, s, pt_: (bi, 0, 0))

    def k_spec(p):
        return pl.BlockSpec((None, cols, A_HEAD_DIM),
                            lambda bi, s, pt_: (pt_[bi * n_pages + jnp.minimum(s, n_k - 1) * pg + p], 0, 0))

    def v_spec(p):
        def index(bi, s, pt_):
            held = jnp.maximum(bi - 1, 0) * n_pages + (n_k - 1) * pg + p
            mine = bi * n_pages + jnp.maximum(s - n_k, 0) * pg + p
            return (pt_[jnp.where(s < n_k, held, mine)], 0, 0)
        return pl.BlockSpec((None, cols, A_HEAD_DIM), index)

    grid_spec = pltpu.PrefetchScalarGridSpec(
        num_scalar_prefetch=1,
        grid=(b, 2 * n_k),
        in_specs=[tok, tok, tok] + [k_spec(p) for p in range(pg)] + [v_spec(p) for p in range(pg)],
        out_specs=tok,
        scratch_shapes=[pltpu.VMEM((R, A_HEAD_DIM), F32),
                        pltpu.VMEM((n_pages, R, page_rows), F32),
                        pltpu.VMEM((n_pages * page_rows // MOBA_BLOCK, A_HEADS, A_HEAD_DIM), F32),
                        pltpu.VMEM((R, A_HEAD_DIM), F32),
                        pltpu.VMEM((R, LANES), F32),
                        pltpu.VMEM((R, R), F32)],
    )
    return pl.pallas_call(
        functools.partial(_moba_decode_kernel, n_pages=n_pages, page_rows=page_rows),
        grid_spec=grid_spec,
        out_shape=jax.ShapeDtypeStruct((b, t, e), F32),
        compiler_params=_params("parallel", "arbitrary"),
        name="moba_decode",
    )(pt, q, kn, vn, *([ck] * pg), *([cv] * pg))


def _trunk(x, valid, pos0, c0, n0, m0, past, w, tm):
    b, s, d = x.shape
    n = b * s
    tm = min(tm, n)
    assert n % tm == 0 and (tm % s == 0 or s % tm == 0)
    xf = x.reshape(n, d)

    q, k, v, o, gates = _norm_proj(xf, w["norm_mix"][0], [w["wq_m"], w["wk_m"], w["wv_m"], w["wo_m"], w["wgate_m"]], tm)
    r3 = lambda t: t.reshape(b, s, t.shape[-1])
    hs, c_new, n_new, m_new = _mlstm(r3(q), r3(k), r3(v), r3(gates), w["b_gate"], c0, n0, m0, valid)
    xf = _mlstm_out(hs.reshape(n, M_V_W), o, xf, w["norm_h_m"], w["w_out_m"], tm)
    xf = _ffn(xf, w["norm_ffn"][0], w["wg"][0], w["wu"][0], w["wd"][0], tm)

    pos = pos0 + jnp.arange(s, dtype=jnp.int32)
    cos, sin = _rope_tables(pos)
    if s < tm:
        cos, sin = jnp.tile(cos, (tm // s, 1)), jnp.tile(sin, (tm // s, 1))
    qa, ka, va, *blocks = _qkv(xf, w["norm_mix"][1], w["wq_a"], w["wk_a"], w["wv_a"], w["norm_q_a"],
                               w["norm_k_a"], cos, sin, tm, with_blocks=past is None)
    if past is None:
        kb, vt, kmean = blocks
        nb = s // MOBA_BLOCK
        att = _moba_prompt(r3(qa), r3(kb), vt.reshape(b, nb, -1, MOBA_BLOCK), kmean[:, 0, :].reshape(b, nb, -1))
    else:
        att = _moba_decode(r3(qa), r3(ka), r3(va), *past)
    xf = _proj_res(att.reshape(n, -1), xf, w["w_out_a"], tm)
    xf = _ffn(xf, w["norm_ffn"][1], w["wg"][1], w["wu"][1], w["wd"][1], tm)

    heads = lambda t: t.reshape(b, s, A_HEADS, A_HEAD_DIM)[:, :valid]
    return xf.reshape(b, s, d)[:, :valid], heads(ka), heads(va), c_new, n_new, m_new


def kernel(x_prompt, x_sample, state_C, state_n, state_m, cache_k, cache_v, page_table, norm_mix, norm_ffn,
           w_in_m, b_gate_m, norm_h_m, w_out_m, w_qkv_a, norm_q_a, norm_k_a, w_out_a, w_gu, w_down):
    assert norm_mix.shape[0] == 2 and w_in_m.shape[0] == 1 and w_qkv_a.shape[0] == 1
    bf = lambda t: t.astype(BF16)
    row = lambda t: t.reshape(1, -1).astype(F32)
    w_in = w_in_m[0]
    e_a = A_HEADS * A_HEAD_DIM
    gate_w = jnp.pad(w_in[:, 2 * M_QK_W + 2 * M_V_W:], ((0, 0), (0, LANES - 2 * M_HEADS)))
    w = {
        "norm_mix": [row(norm_mix[0]), row(norm_mix[1])],
        "norm_ffn": [row(norm_ffn[0]), row(norm_ffn[1])],
        "wq_m": bf(w_in[:, :M_QK_W]), "wk_m": bf(w_in[:, M_QK_W:2 * M_QK_W]),
        "wv_m": bf(w_in[:, 2 * M_QK_W:2 * M_QK_W + M_V_W]),
        "wo_m": bf(w_in[:, 2 * M_QK_W + M_V_W:2 * M_QK_W + 2 * M_V_W]),
        "wgate_m": bf(gate_w),
        "b_gate": jnp.pad(b_gate_m[0].astype(F32), (0, LANES - 2 * M_HEADS)).reshape(1, LANES),
        "norm_h_m": row(norm_h_m[0]), "w_out_m": bf(w_out_m[0]),
        "wq_a": bf(w_qkv_a[0][:, :e_a]), "wk_a": bf(w_qkv_a[0][:, e_a:2 * e_a]), "wv_a": bf(w_qkv_a[0][:, 2 * e_a:]),
        "norm_q_a": row(norm_q_a[0]), "norm_k_a": row(norm_k_a[0]), "w_out_a": bf(w_out_a[0]),
        "wg": [bf(w_gu[i][:, :FFN_HIDDEN]) for i in range(2)],
        "wu": [bf(w_gu[i][:, FFN_HIDDEN:]) for i in range(2)],
        "wd": [bf(w_down[i]) for i in range(2)],
    }

    def state(c, nn, mm):
        bsz = c.shape[0]
        m_row = jnp.pad(mm.astype(F32), ((0, 0), (0, LANES - M_HEADS))).reshape(bsz, 1, LANES)
        return c.astype(F32), nn.astype(F32).reshape(bsz, M_HEADS, 1, M_QK_DIM), m_row

    def unstate(c, nn, mm, dt):
        bsz = c.shape[0]
        return (c[None].astype(dt), nn.reshape(1, bsz, M_HEADS, M_QK_DIM).astype(dt),
                mm.reshape(bsz, LANES)[None, :, :M_HEADS].astype(dt))

    sd = state_C.dtype
    bp, sp, _ = x_prompt.shape
    zc = jnp.zeros((bp,) + state_C.shape[2:], F32)
    zn = jnp.zeros((bp,) + state_n.shape[2:], F32)
    zm = jnp.zeros((bp,) + state_m.shape[2:], F32)
    yp, pk, pv, pc, pn, pm = _trunk(x_prompt, sp, 0, *state(zc, zn, zm), None, w, 512)

    bs, ss, _ = x_sample.shape
    xs = jnp.pad(x_sample, ((0, 0), (0, DEC_PAD - ss), (0, 0)))
    past_len = page_table.shape[1] * cache_k.shape[2]
    ys, sk, sv, sc, sn, sm = _trunk(xs, ss, past_len, *state(state_C[0], state_n[0], state_m[0]),
                                    (cache_k[0], cache_v[0], page_table), w, 512)

    return (yp, ys, pk[None], pv[None], *unstate(pc, pn, pm, sd), sk[None], sv[None], *unstate(sc, sn, sm, sd))
```

```python
import functools
import math

import jax
import jax.numpy as jnp
from jax import lax
from jax.experimental import pallas as pl
from jax.experimental.pallas import tpu as pltpu

F32 = jnp.float32
BF16 = jnp.bfloat16

D_MODEL = 1024
M_HEADS = 4
M_QK_DIM = 128
M_V_DIM = 256
M_QK_W = M_HEADS * M_QK_DIM
M_V_W = M_HEADS * M_V_DIM
A_HEADS = 8
A_HEAD_DIM = 128
MOBA_BLOCK = 256
MOBA_TOPK = 3
ROPE_THETA = 10000.0
FFN_HIDDEN = 2816
EPS = 1e-6

LANES = 128
SUBLANES = 8
BF16_ROWS = 16
VT_ROWS = A_HEAD_DIM + BF16_ROWS
VMEM_LIMIT_BYTES = 56 * 1024 * 1024

MLSTM_CHUNK = 128
DEC_PAD = SUBLANES
NEG_INF = float("-inf")

_HIGHEST = lax.Precision.HIGHEST
_NT = (((1,), (1,)), ((), ()))
_TN = (((0,), (0,)), ((), ()))


def _params(*sem):
    return pltpu.CompilerParams(dimension_semantics=sem, vmem_limit_bytes=VMEM_LIMIT_BYTES)


def _const_spec(shape):
    nd = len(shape)
    return pl.BlockSpec(shape, lambda *_: (0,) * nd, pipeline_mode=pl.Buffered(1))


def _rms(x, g):
    return x * lax.rsqrt(jnp.mean(x * x, axis=-1, keepdims=True) + EPS) * g


def _norm_proj_kernel(x_ref, g_ref, *refs, n_out):
    w_refs, o_refs = refs[:n_out], refs[n_out:]
    hb = _rms(x_ref[...], g_ref[...]).astype(BF16)
    for w_ref, o_ref in zip(w_refs, o_refs):
        o_ref[...] = jnp.dot(hb, w_ref[...], preferred_element_type=F32)


def _norm_proj(x, g, ws, tm):
    n, d = x.shape
    return pl.pallas_call(
        functools.partial(_norm_proj_kernel, n_out=len(ws)),
        grid=(n // tm,),
        in_specs=[pl.BlockSpec((tm, d), lambda i: (i, 0)), _const_spec((1, d))]
        + [_const_spec(w.shape) for w in ws],
        out_specs=[pl.BlockSpec((tm, w.shape[1]), lambda i: (i, 0)) for w in ws],
        out_shape=[jax.ShapeDtypeStruct((n, w.shape[1]), F32) for w in ws],
        compiler_params=_params("parallel"),
        name="norm_proj",
    )(x, g, *ws)


def _mlstm_kernel(q_ref, k_ref, v_ref, gt_ref, bias_ref, c0_ref, n0_ref, m0_ref,
                  hs_ref, c_ref, n_ref, m_ref, *pad_refs, rows, valid, chunk):
    ci = pl.program_id(1)

    @pl.when(ci == 0)
    def _():
        c_ref[...] = c0_ref[...]
        n_ref[...] = n0_ref[...]
        m_ref[...] = m0_ref[...]

    if rows == chunk:
        q_all, k_all, v_all, gt_all = q_ref[...], k_ref[...], v_ref[...], gt_ref[...]
    else:
        qp, kp, vp, gp = pad_refs

        @pl.when(jnp.logical_and(pl.program_id(0) == 0, ci == 0))
        def _():
            for p in pad_refs:
                p[...] = jnp.zeros(p.shape, p.dtype)

        qp[0:rows, :], kp[0:rows, :], vp[0:rows, :], gp[0:rows, :] = (
            q_ref[...], k_ref[...], v_ref[...], gt_ref[...])
        q_all, k_all, v_all, gt_all = qp[...], kp[...], vp[...], gp[...]

    L = chunk
    row_l = lax.broadcasted_iota(jnp.int32, (L, LANES), 0)
    lane_l = lax.broadcasted_iota(jnp.int32, (L, LANES), 1)
    is_ig = lane_l < M_HEADS
    real = row_l < valid
    gc = gt_all + bias_ref[...]
    lfc = jnp.where(real, jax.nn.log_sigmoid(gc), 0.0)
    igc = jnp.where(real, gc, NEG_INF)
    gsel = jnp.where(is_ig, igc, lfc)

    eye8 = (lax.broadcasted_iota(jnp.int32, (SUBLANES, LANES), 0)
            == lax.broadcasted_iota(jnp.int32, (SUBLANES, LANES), 1)).astype(F32)
    gsel_fin = jnp.where(real, gsel, 0.0)
    rows_g = lax.dot_general(eye8, gsel_fin, _NT, precision=_HIGHEST, preferred_element_type=F32)
    col_real = lax.broadcasted_iota(jnp.int32, (SUBLANES, L), 1) < valid
    rows_is_ig = lax.broadcasted_iota(jnp.int32, (SUBLANES, L), 0) < M_HEADS
    rows_g = jnp.where(jnp.logical_and(rows_is_ig, jnp.logical_not(col_real)), NEG_INF, rows_g)

    tt = lax.broadcasted_iota(jnp.int32, (L, L), 0)
    ss = lax.broadcasted_iota(jnp.int32, (L, L), 1)
    causal = ss <= tt
    tril = causal.astype(F32)
    triu = (tt <= ss).astype(F32)
    b_cols = jnp.dot(tril, lfc, precision=_HIGHEST, preferred_element_type=F32)
    lf_rows = jnp.where(rows_is_ig, 0.0, rows_g)
    b_rows = jnp.dot(lf_rows, triu, precision=_HIGHEST, preferred_element_type=F32)

    m_prev_row = m_ref[0]
    lane_1 = lax.broadcasted_iota(jnp.int32, (1, LANES), 1)
    m_new_row = m_prev_row
    scale = M_QK_DIM ** -0.5
    hd = range(M_HEADS)
    qs = [q_all[:, h * M_QK_DIM:(h + 1) * M_QK_DIM] * scale for h in hd]
    ks = [k_all[:, h * M_QK_DIM:(h + 1) * M_QK_DIM] for h in hd]
    qbs = [q.astype(BF16) for q in qs]
    kbs = [k.astype(BF16) for k in ks]
    vbs = [v_all[:, h * M_V_DIM:(h + 1) * M_V_DIM].astype(BF16) for h in hd]
    c_olds = [c_ref[0, h] for h in hd]
    n_olds = [n_ref[0, h] for h in hd]
    b_cs = [b_cols[:, M_HEADS + h:M_HEADS + h + 1] for h in hd]
    m_prevs = [m_prev_row[:, h:h + 1] for h in hd]

    qk = [lax.dot_general(qb, kb, _NT, preferred_element_type=F32) for qb, kb in zip(qbs, kbs)]
    qc = [jnp.dot(qb, c.astype(BF16), preferred_element_type=F32) for qb, c in zip(qbs, c_olds)]

    a_s, m_ts, w_inters = [], [], []
    for h in hd:
        d = jnp.where(causal, b_cs[h] - b_rows[M_HEADS + h:M_HEADS + h + 1, :] + rows_g[h:h + 1, :], NEG_INF)
        m_inter = b_cs[h] + m_prevs[h]
        m_t = jnp.maximum(m_inter, jnp.max(d, axis=-1, keepdims=True))
        a_s.append(qk[h] * jnp.exp(d - m_t))
        m_ts.append(m_t)
        w_inters.append(jnp.exp(m_inter - m_t))
    av = [jnp.dot(a.astype(BF16), vb, preferred_element_type=F32) for a, vb in zip(a_s, vbs)]

    kws, w_states = [], []
    for h in hd:
        num = w_inters[h] * qc[h] + av[h]
        den = (w_inters[h] * jnp.sum(qs[h] * n_olds[h], axis=-1, keepdims=True)
               + jnp.sum(a_s[h], axis=-1, keepdims=True))
        hval = num / jnp.maximum(jnp.abs(den), jnp.exp(-m_ts[h]))
        hs_ref[:, h * M_V_DIM:(h + 1) * M_V_DIM] = hval[0:rows, :]
        m_new = m_ts[h][L - 1:L, :]
        b_last = b_cs[h][L - 1:L, :]
        w_states.append(jnp.exp(b_last + m_prevs[h] - m_new))
        kws.append(ks[h] * jnp.exp(b_last - b_cs[h] + gsel[:, h:h + 1] - m_new))
        m_new_row = jnp.where(lane_1 == h, m_new, m_new_row)
    kv = [lax.dot_general(kw.astype(BF16), vb, _TN, preferred_element_type=F32) for kw, vb in zip(kws, vbs)]
    for h in hd:
        c_ref[0, h] = w_states[h] * c_olds[h] + kv[h]
        n_ref[0, h] = w_states[h] * n_olds[h] + jnp.sum(kws[h], axis=0, keepdims=True)
    m_ref[0] = m_new_row


def _mlstm(q, k, v, gates, bias, c0, n0, m0, valid):
    b, s, _ = q.shape
    if s >= MLSTM_CHUNK:
        rows, nc = MLSTM_CHUNK, s // MLSTM_CHUNK
        assert valid == s and s % MLSTM_CHUNK == 0
        scratch = []
    else:
        rows, nc = s, 1
        scratch = [pltpu.VMEM((MLSTM_CHUNK, w), F32) for w in (M_QK_W, M_QK_W, M_V_W, LANES)]
    tok = lambda w: pl.BlockSpec((None, rows, w), lambda bi, ci: (bi, ci, 0))
    st_c = pl.BlockSpec((1, M_HEADS, M_QK_DIM, M_V_DIM), lambda bi, ci: (bi, 0, 0, 0))
    st_n = pl.BlockSpec((1, M_HEADS, 1, M_QK_DIM), lambda bi, ci: (bi, 0, 0, 0))
    st_m = pl.BlockSpec((1, 1, LANES), lambda bi, ci: (bi, 0, 0))
    return pl.pallas_call(
        functools.partial(_mlstm_kernel, rows=rows, valid=valid, chunk=MLSTM_CHUNK),
        grid=(b, nc),
        in_specs=[tok(M_QK_W), tok(M_QK_W), tok(M_V_W), tok(LANES), _const_spec((1, LANES)), st_c, st_n, st_m],
        out_specs=[tok(M_V_W), st_c, st_n, st_m],
        out_shape=[jax.ShapeDtypeStruct((b, s, M_V_W), F32), jax.ShapeDtypeStruct(c0.shape, F32),
                   jax.ShapeDtypeStruct(n0.shape, F32), jax.ShapeDtypeStruct(m0.shape, F32)],
        scratch_shapes=scratch,
        compiler_params=_params("arbitrary" if scratch else "parallel", "arbitrary"),
        name="mlstm_chunks",
    )(q, k, v, gates, bias, c0, n0, m0)


def _mlstm_out_kernel(hs_ref, o_ref, x_ref, g_ref, w_ref, out_ref):
    parts = []
    for h in range(M_HEADS):
        sl = slice(h * M_V_DIM, (h + 1) * M_V_DIM)
        parts.append(_rms(hs_ref[:, sl], g_ref[:, sl]))
    y = jnp.concatenate(parts, axis=-1) * jax.nn.sigmoid(o_ref[...])
    out_ref[...] = x_ref[...] + jnp.dot(y.astype(BF16), w_ref[...], preferred_element_type=F32)


def _mlstm_out(hs, o, x, g, w, tm):
    n, d = x.shape
    row = lambda wd: pl.BlockSpec((tm, wd), lambda i: (i, 0))
    return pl.pallas_call(
        _mlstm_out_kernel,
        grid=(n // tm,),
        in_specs=[row(M_V_W), row(M_V_W), row(d), _const_spec((1, M_V_W)), _const_spec(w.shape)],
        out_specs=row(d),
        out_shape=jax.ShapeDtypeStruct((n, d), F32),
        compiler_params=_params("parallel"),
        name="mlstm_out",
    )(hs, o, x, g, w)


def _proj_res_kernel(a_ref, x_ref, w_ref, out_ref):
    out_ref[...] = x_ref[...] + jnp.dot(a_ref[...].astype(BF16), w_ref[...], preferred_element_type=F32)


def _proj_res(a, x, w, tm):
    n, d = x.shape
    return pl.pallas_call(
        _proj_res_kernel,
        grid=(n // tm,),
        in_specs=[pl.BlockSpec((tm, a.shape[1]), lambda i: (i, 0)), pl.BlockSpec((tm, d), lambda i: (i, 0)),
                  _const_spec(w.shape)],
        out_specs=pl.BlockSpec((tm, d), lambda i: (i, 0)),
        out_shape=jax.ShapeDtypeStruct((n, d), F32),
        compiler_params=_params("parallel"),
        name="proj_res",
    )(a, x, w)


FFN_CHUNKS = 2


def _ffn_kernel(x_ref, g_ref, wg_ref, wu_ref, wd_ref, out_ref):
    x = x_ref[...]
    hb = _rms(x, g_ref[...]).astype(BF16)
    acc = x
    cw = FFN_HIDDEN // FFN_CHUNKS
    for c in range(FFN_CHUNKS):
        sl = slice(c * cw, (c + 1) * cw)
        gg = jnp.dot(hb, wg_ref[:, sl], preferred_element_type=F32)
        uu = jnp.dot(hb, wu_ref[:, sl], preferred_element_type=F32)
        act = (jax.nn.silu(gg) * uu).astype(BF16)
        acc = acc + jnp.dot(act, wd_ref[sl, :], preferred_element_type=F32)
    out_ref[...] = acc


def _ffn(x, g, wg, wu, wd, tm):
    n, d = x.shape
    return pl.pallas_call(
        _ffn_kernel,
        grid=(n // tm,),
        in_specs=[pl.BlockSpec((tm, d), lambda i: (i, 0)), _const_spec((1, d)),
                  _const_spec(wg.shape), _const_spec(wu.shape), _const_spec(wd.shape)],
        out_specs=pl.BlockSpec((tm, d), lambda i: (i, 0)),
        out_shape=jax.ShapeDtypeStruct((n, d), F32),
        compiler_params=_params("parallel"),
        name="ffn",
    )(x, g, wg, wu, wd)


def _qkv_kernel(x_ref, g_ref, wq_ref, wk_ref, wv_ref, gq_ref, gk_ref, cos_ref, sin_ref,
                q_ref, k_ref, v_ref, *blk_refs):
    hb = _rms(x_ref[...], g_ref[...]).astype(BF16)
    cos, sin = cos_ref[...], sin_ref[...]

    kb_ref, vt_ref, km_ref = blk_refs if blk_refs else (None, None, None)
    tm = x_ref.shape[0]
    rt = min(tm, MOBA_BLOCK)
    cw = 2 * A_HEAD_DIM

    def norm_rope(t, gh_ref, rows):
        y = _rms(t, gh_ref[...])
        return y * cos[rows, :] + pltpu.roll(y, A_HEAD_DIM // 2, 1) * sin[rows, :]

    for j in range(tm // rt):
        rows = slice(j * rt, (j + 1) * rt)
        hr = hb[rows, :]
        for c in range(A_HEADS // 2):
            t = jnp.dot(hr, wq_ref[:, c * cw:(c + 1) * cw], preferred_element_type=F32)
            for i in range(2):
                sl = slice((2 * c + i) * A_HEAD_DIM, (2 * c + i + 1) * A_HEAD_DIM)
                q_ref[rows, sl] = norm_rope(t[:, i * A_HEAD_DIM:(i + 1) * A_HEAD_DIM], gq_ref, rows)
        for c in range(A_HEADS // 2):
            t = jnp.dot(hr, wk_ref[:, c * cw:(c + 1) * cw], preferred_element_type=F32)
            for i in range(2):
                sl = slice((2 * c + i) * A_HEAD_DIM, (2 * c + i + 1) * A_HEAD_DIM)
                k = norm_rope(t[:, i * A_HEAD_DIM:(i + 1) * A_HEAD_DIM], gk_ref, rows)
                k_ref[rows, sl] = k
                if blk_refs:
                    kb_ref[rows, sl] = k.astype(BF16)
                    mean = jnp.mean(k, axis=0, keepdims=True)
                    km_ref[j, :, sl] = jnp.broadcast_to(mean, (SUBLANES, A_HEAD_DIM))
        for c in range(A_HEADS // 2):
            t = jnp.dot(hr, wv_ref[:, c * cw:(c + 1) * cw], preferred_element_type=F32)
            v_ref[rows, c * cw:(c + 1) * cw] = t
            if blk_refs:
                for i in range(2):
                    r0 = (2 * c + i) * VT_ROWS
                    vt_ref[j, r0:r0 + A_HEAD_DIM, :] = t[:, i * A_HEAD_DIM:(i + 1) * A_HEAD_DIM].T.astype(BF16)
                    vt_ref[j, r0 + A_HEAD_DIM:r0 + VT_ROWS, :] = jnp.ones((BF16_ROWS, MOBA_BLOCK), BF16)


def _qkv(x, g, wq, wk, wv, gq, gk, cos, sin, tm, with_blocks):
    n, d = x.shape
    e = wq.shape[1]
    n_tab = cos.shape[0] // tm
    row = lambda wd: pl.BlockSpec((tm, wd), lambda i: (i, 0))
    tab = pl.BlockSpec((tm, A_HEAD_DIM), lambda i: (i % n_tab, 0))
    out_specs = [row(e), row(e), row(e)]
    out_shape = [jax.ShapeDtypeStruct((n, e), F32)] * 3
    if with_blocks:
        assert tm % MOBA_BLOCK == 0
        gpt = tm // MOBA_BLOCK
        vrows = A_HEADS * VT_ROWS
        out_specs += [row(e), pl.BlockSpec((gpt, vrows, MOBA_BLOCK), lambda i: (i, 0, 0)),
                      pl.BlockSpec((gpt, SUBLANES, e), lambda i: (i, 0, 0))]
        out_shape += [jax.ShapeDtypeStruct((n, e), BF16),
                      jax.ShapeDtypeStruct((n // MOBA_BLOCK, vrows, MOBA_BLOCK), BF16),
                      jax.ShapeDtypeStruct((n // MOBA_BLOCK, SUBLANES, e), F32)]
    return pl.pallas_call(
        _qkv_kernel,
        grid=(n // tm,),
        in_specs=[row(d), _const_spec((1, d)), _const_spec(wq.shape), _const_spec(wk.shape),
                  _const_spec(wv.shape), _const_spec((1, A_HEAD_DIM)), _const_spec((1, A_HEAD_DIM)), tab, tab],
        out_specs=out_specs,
        out_shape=out_shape,
        compiler_params=_params("parallel"),
        name="qkv_rope",
    )(x, g, wq, wk, wv, gq, gk, cos, sin)


def _rope_tables(pos):
    half = A_HEAD_DIM // 2
    inv_freq = ROPE_THETA ** (-jnp.arange(half, dtype=F32) / half)
    ang = pos.astype(F32)[:, None] * inv_freq[None, :]
    c, s = jnp.cos(ang), jnp.sin(ang)
    return jnp.concatenate([c, c], axis=-1), jnp.concatenate([-s, s], axis=-1)


MOBA_HEADS_PER_STEP = 4
MASK_BIG = 2.0 ** 100


def _top_blocks_t(scores_t, allowed):
    row = lax.broadcasted_iota(jnp.int32, scores_t.shape, 0)
    s = jnp.where(allowed, scores_t, NEG_INF)
    picked = jnp.zeros(scores_t.shape, F32)
    for _ in range(MOBA_TOPK):
        mx = jnp.max(s, axis=0, keepdims=True)
        cand = jnp.where(jnp.logical_and(s == mx, s > NEG_INF), row, scores_t.shape[0])
        hit = row == jnp.min(cand, axis=0, keepdims=True)
        picked = jnp.where(hit, 1.0, picked)
        s = jnp.where(hit, NEG_INF, s)
    return picked


def _moba_prompt_kernel(q_ref, kb_ref, vt_ref, km_ref, o_ref, sa_ref, sb_ref, *, n_blocks, hp):
    own = pl.program_id(2)
    blk = MOBA_BLOCK
    scale = A_HEAD_DIM ** -0.5
    nbp = -(-n_blocks // BF16_ROWS) * BF16_ROWS
    lane = lax.broadcasted_iota(jnp.int32, (blk, LANES), 1)
    blk_i = lax.broadcasted_iota(jnp.int32, (nbp, blk), 0)
    heads = [slice(h * A_HEAD_DIM, (h + 1) * A_HEAD_DIM) for h in range(hp)]
    vrows = [slice(h * VT_ROWS, (h + 1) * VT_ROWS) for h in range(hp)]

    q_t = [q_ref[:, sl].T for sl in heads]
    kms = []
    for sl in heads:
        km = km_ref[:, sl]
        if nbp > n_blocks:
            km = jnp.concatenate([km, jnp.zeros((nbp - n_blocks, A_HEAD_DIM), F32)], axis=0)
        kms.append(km)
    scores = [jnp.dot(km, qt, precision=_HIGHEST, preferred_element_type=F32) for km, qt in zip(kms, q_t)]
    q_ext = []
    for qt, sc in zip(q_t, scores):
        picked = _top_blocks_t(sc, blk_i < own)
        unmasked = jnp.where(blk_i == own, 1.0, picked)
        pad = jnp.zeros((LANES - nbp, blk), BF16)
        q_ext.append(jnp.concatenate([(qt * scale).astype(BF16), (1.0 - unmasked).astype(BF16), pad], axis=0))

    def block_logits(j):
        st = pl.multiple_of(j * blk, blk)
        mask_cols = jnp.where(lane == j, -MASK_BIG, 0.0).astype(BF16)
        return tuple(
            jnp.dot(jnp.concatenate([kb_ref[pl.ds(st, blk), sl], mask_cols], axis=1), qe, preferred_element_type=F32)
            for sl, qe in zip(heads, q_ext))

    def stage_logits(ref, j):
        for h, s in enumerate(block_logits(j)):
            ref[h] = s

    def accumulate(j, logits, state):
        probs, stats = [], []
        for s, (m, _) in zip(logits, state):
            m_new = jnp.maximum(m, jnp.max(s, axis=0, keepdims=True))
            probs.append(jnp.exp(s - m_new).astype(BF16))
            stats.append((m_new, jnp.exp(m - m_new)))
        return tuple(
            (m_new, alpha * acc + jnp.dot(vt_ref[j, vr, :], pb, preferred_element_type=F32))
            for vr, pb, (m_new, alpha), (_, acc) in zip(vrows, probs, stats, state))

    def accumulate_staged(j, ref, state):
        return accumulate(j, tuple(ref[h] for h in range(hp)), state)

    state = tuple((jnp.full((1, blk), -MASK_BIG, F32), jnp.zeros((VT_ROWS, blk), F32)) for _ in heads)

    @pl.when(own > 0)
    def _():
        stage_logits(sa_ref, 0)

    def pair(i, state):
        j = 2 * i
        stage_logits(sb_ref, j + 1)
        state = accumulate_staged(j, sa_ref, state)
        stage_logits(sa_ref, j + 2)
        return accumulate_staged(j + 1, sb_ref, state)

    state = lax.fori_loop(0, lax.shift_right_logical(own, 1), pair, state)
    state = lax.fori_loop(0, jnp.bitwise_and(own, 1), lambda _, st: accumulate_staged(own - 1, sa_ref, st), state)

    key_i = lax.broadcasted_iota(jnp.int32, (blk, blk), 0)
    qry_i = lax.broadcasted_iota(jnp.int32, (blk, blk), 1)
    logits = tuple(jnp.where(key_i <= qry_i, s, NEG_INF) for s in block_logits(own))
    state = accumulate(own, logits, state)
    for sl, (_, acc) in zip(heads, state):
        o_ref[:, sl] = (acc[0:A_HEAD_DIM, :] / acc[A_HEAD_DIM:A_HEAD_DIM + 1, :]).T


def _moba_prompt(q, kb, vt, kmean):
    b, s, e = q.shape
    nb = s // MOBA_BLOCK
    hp = MOBA_HEADS_PER_STEP
    assert s % MOBA_BLOCK == 0 and nb <= LANES and A_HEADS % hp == 0
    w = hp * A_HEAD_DIM
    tile = pl.BlockSpec((None, MOBA_BLOCK, w), lambda bi, h, qi: (bi, qi, h))
    return pl.pallas_call(
        functools.partial(_moba_prompt_kernel, n_blocks=nb, hp=hp),
        grid=(b, A_HEADS // hp, nb),
        in_specs=[tile, pl.BlockSpec((None, s, w), lambda bi, h, qi: (bi, 0, h)),
                  pl.BlockSpec((None, nb, hp * VT_ROWS, MOBA_BLOCK), lambda bi, h, qi: (bi, 0, h, 0)),
                  pl.BlockSpec((None, nb, w), lambda bi, h, qi: (bi, 0, h))],
        out_specs=tile,
        out_shape=jax.ShapeDtypeStruct((b, s, e), F32),
        scratch_shapes=[pltpu.VMEM((hp, MOBA_BLOCK, MOBA_BLOCK), F32)] * 2,
        compiler_params=_params("parallel", "parallel", "arbitrary"),
        name="moba_prompt",
    )(q, kb, vt, kmean)


def _moba_decode_kernel(pt_ref, q_ref, kn_ref, vn_ref, ck_ref, cv_ref, o_ref, kbuf, vbuf, sem, logit_ref,
                        *, n_pages, page_rows):
    bi = pl.program_id(0)
    n_seq = pl.num_programs(0)
    slot = lax.rem(bi, 2)
    R = A_HEADS * DEC_PAD
    scale = A_HEAD_DIM ** -0.5
    pages_per_block = MOBA_BLOCK // page_rows
    n_blocks = n_pages // pages_per_block

    def page_copy(cache_ref, buf_ref, which, seq, sl, g):
        page = pt_ref[seq * n_pages + g]
        return pltpu.make_async_copy(cache_ref.at[page], buf_ref.at[sl, g], sem.at[sl, which])

    def start_pages(seq, sl):
        for g in range(n_pages):
            page_copy(ck_ref, kbuf, 0, seq, sl, g).start()
        for g in range(n_pages):
            page_copy(cv_ref, vbuf, 1, seq, sl, g).start()

    @pl.when(bi == 0)
    def _():
        start_pages(bi, slot)

    @pl.when(bi + 1 < n_seq)
    def _():
        start_pages(bi + 1, 1 - slot)

    def rows_by_head(ref):
        return jnp.concatenate([ref[:, h * A_HEAD_DIM:(h + 1) * A_HEAD_DIM] for h in range(A_HEADS)], axis=0)

    def by_head(ref):
        return jnp.concatenate([ref[pl.ds(h, page_rows, stride=A_HEADS), :].astype(BF16) for h in range(A_HEADS)],
                               axis=0)

    q = rows_by_head(q_ref)
    qb = q.astype(BF16)

    for g in range(n_pages):
        page_copy(ck_ref, kbuf, 0, bi, slot, g).wait()
    ksum = []
    for g in range(n_pages):
        kpage = kbuf.at[slot, g]
        lg = lax.dot_general(qb, by_head(kpage), _NT, preferred_element_type=F32)
        for h in range(A_HEADS):
            hr = slice(h * DEC_PAD, (h + 1) * DEC_PAD)
            logit_ref[g, hr, :] = lg[hr, h * page_rows:(h + 1) * page_rows] * scale
        part = jnp.sum(kpage[...].reshape(page_rows, A_HEADS, A_HEAD_DIM), axis=0)
        if g % pages_per_block == 0:
            ksum.append(part)
        else:
            ksum[-1] = ksum[-1] + part

    kmean = jnp.concatenate(ksum, axis=0) * (1.0 / MOBA_BLOCK)
    nbh = n_blocks * A_HEADS
    scores_t = lax.dot_general(kmean, q, _NT, precision=_HIGHEST, preferred_element_type=F32)
    brow = lax.broadcasted_iota(jnp.int32, (nbh, R), 0)
    qcol = lax.broadcasted_iota(jnp.int32, (nbh, R), 1)
    picked_t = _top_blocks_t(scores_t, brow % A_HEADS == qcol // DEC_PAD)
    erow = lax.broadcasted_iota(jnp.int32, (nbh, n_blocks * page_rows), 0)
    ecol = lax.broadcasted_iota(jnp.int32, (nbh, n_blocks * page_rows), 1)
    expand = jnp.where(erow // A_HEADS == ecol // page_rows, 1.0, 0.0).astype(BF16)
    keep = lax.dot_general(picked_t.astype(BF16), expand, _TN, preferred_element_type=F32)
    keep_blk = [keep[:, j * page_rows:(j + 1) * page_rows] > 0.5 for j in range(n_blocks)]

    kn = rows_by_head(kn_ref).astype(BF16)
    ln = lax.dot_general(qb, kn, _NT, preferred_element_type=F32) * scale
    rn = lax.broadcasted_iota(jnp.int32, (R, R), 0)
    cn = lax.broadcasted_iota(jnp.int32, (R, R), 1)
    ok_n = jnp.logical_and(rn // DEC_PAD == cn // DEC_PAD, cn % DEC_PAD <= rn % DEC_PAD)
    ln = jnp.where(ok_n, ln, NEG_INF)

    masked = [jnp.where(keep_blk[g // pages_per_block], logit_ref[g], NEG_INF) for g in range(n_pages)]
    mm = masked[0]
    for g in range(1, n_pages):
        mm = jnp.maximum(mm, masked[g])
    m = jnp.maximum(jnp.max(ln, axis=-1, keepdims=True), jnp.max(mm, axis=-1, keepdims=True))
    pn = jnp.exp(ln - m)
    probs = [jnp.exp(s - m) for s in masked]
    lsum = probs[0]
    for g in range(1, n_pages):
        lsum = lsum + probs[g]
    l = jnp.sum(pn, axis=-1, keepdims=True) + jnp.sum(lsum, axis=-1, keepdims=True)

    for g in range(n_pages):
        page_copy(cv_ref, vbuf, 1, bi, slot, g).wait()
    acc = jnp.dot(pn.astype(BF16), rows_by_head(vn_ref).astype(BF16), preferred_element_type=F32)
    zero = jnp.zeros((DEC_PAD, page_rows), F32)
    for g in range(n_pages):
        pe = probs[g]
        p_bd = jnp.concatenate(
            [jnp.concatenate([pe[h * DEC_PAD:(h + 1) * DEC_PAD, :] if hh == h else zero for hh in range(A_HEADS)],
                             axis=1) for h in range(A_HEADS)], axis=0)
        acc = acc + jnp.dot(p_bd.astype(BF16), by_head(vbuf.at[slot, g]), preferred_element_type=F32)
    res = acc / l
    for h in range(A_HEADS):
        o_ref[:, h * A_HEAD_DIM:(h + 1) * A_HEAD_DIM] = res[h * DEC_PAD:(h + 1) * DEC_PAD, :]


def _moba_decode(q, kn, vn, cache_k, cache_v, page_table):
    b, t, e = q.shape
    n_phys, page_rows = cache_k.shape[0], cache_k.shape[1]
    n_pages = page_table.shape[1]
    assert t == DEC_PAD and MOBA_BLOCK % page_rows == 0 and n_pages % (MOBA_BLOCK // page_rows) == 0
    cols = page_rows * A_HEADS
    ck = cache_k.reshape(n_phys, cols, A_HEAD_DIM)
    cv = cache_v.reshape(n_phys, cols, A_HEAD_DIM)
    pt = page_table.reshape(-1).astype(jnp.int32)

    tok = pl.BlockSpec((None, t, e), lambda bi, pt_: (bi, 0, 0))
    hbm = pl.BlockSpec(memory_space=pl.ANY)
    grid_spec = pltpu.PrefetchScalarGridSpec(
        num_scalar_prefetch=1,
        grid=(b,),
        in_specs=[tok, tok, tok, hbm, hbm],
        out_specs=tok,
        scratch_shapes=[pltpu.VMEM((2, n_pages, cols, A_HEAD_DIM), F32),
                        pltpu.VMEM((2, n_pages, cols, A_HEAD_DIM), F32),
                        pltpu.SemaphoreType.DMA((2, 2)),
                        pltpu.VMEM((n_pages, A_HEADS * DEC_PAD, page_rows), F32)],
    )
    return pl.pallas_call(
        functools.partial(_moba_decode_kernel, n_pages=n_pages, page_rows=page_rows),
        grid_spec=grid_spec,
        out_shape=jax.ShapeDtypeStruct((b, t, e), F32),
        compiler_params=_params("arbitrary"),
        name="moba_decode",
    )(pt, q, kn, vn, ck, cv)


def _trunk(x, valid, pos0, c0, n0, m0, past, w, tm):
    b, s, d = x.shape
    n = b * s
    tm = min(tm, n)
    assert n % tm == 0 and (tm % s == 0 or s % tm == 0)
    xf = x.reshape(n, d)

    q, k, v, o, gates = _norm_proj(xf, w["norm_mix"][0], [w["wq_m"], w["wk_m"], w["wv_m"], w["wo_m"], w["wgate_m"]], tm)
    r3 = lambda t: t.reshape(b, s, t.shape[-1])
    hs, c_new, n_new, m_new = _mlstm(r3(q), r3(k), r3(v), r3(gates), w["b_gate"], c0, n0, m0, valid)
    xf = _mlstm_out(hs.reshape(n, M_V_W), o, xf, w["norm_h_m"], w["w_out_m"], tm)
    xf = _ffn(xf, w["norm_ffn"][0], w["wg"][0], w["wu"][0], w["wd"][0], tm)

    pos = pos0 + jnp.arange(s, dtype=jnp.int32)
    cos, sin = _rope_tables(pos)
    if s < tm:
        cos, sin = jnp.tile(cos, (tm // s, 1)), jnp.tile(sin, (tm // s, 1))
    qa, ka, va, *blocks = _qkv(xf, w["norm_mix"][1], w["wq_a"], w["wk_a"], w["wv_a"], w["norm_q_a"],
                               w["norm_k_a"], cos, sin, tm, with_blocks=past is None)
    if past is None:
        kb, vt, kmean = blocks
        nb = s // MOBA_BLOCK
        att = _moba_prompt(r3(qa), r3(kb), vt.reshape(b, nb, -1, MOBA_BLOCK), kmean[:, 0, :].reshape(b, nb, -1))
    else:
        att = _moba_decode(r3(qa), r3(ka), r3(va), *past)
    xf = _proj_res(att.reshape(n, -1), xf, w["w_out_a"], tm)
    xf = _ffn(xf, w["norm_ffn"][1], w["wg"][1], w["wu"][1], w["wd"][1], tm)

    heads = lambda t: t.reshape(b, s, A_HEADS, A_HEAD_DIM)[:, :valid]
    return xf.reshape(b, s, d)[:, :valid], heads(ka), heads(va), c_new, n_new, m_new


def kernel(x_prompt, x_sample, state_C, state_n, state_m, cache_k, cache_v, page_table, norm_mix, norm_ffn,
           w_in_m, b_gate_m, norm_h_m, w_out_m, w_qkv_a, norm_q_a, norm_k_a, w_out_a, w_gu, w_down):
    assert norm_mix.shape[0] == 2 and w_in_m.shape[0] == 1 and w_qkv_a.shape[0] == 1
    bf = lambda t: t.astype(BF16)
    row = lambda t: t.reshape(1, -1).astype(F32)
    w_in = w_in_m[0]
    e_a = A_HEADS * A_HEAD_DIM
    gate_w = jnp.pad(w_in[:, 2 * M_QK_W + 2 * M_V_W:], ((0, 0), (0, LANES - 2 * M_HEADS)))
    w = {
        "norm_mix": [row(norm_mix[0]), row(norm_mix[1])],
        "norm_ffn": [row(norm_ffn[0]), row(norm_ffn[1])],
        "wq_m": bf(w_in[:, :M_QK_W]), "wk_m": bf(w_in[:, M_QK_W:2 * M_QK_W]),
        "wv_m": bf(w_in[:, 2 * M_QK_W:2 * M_QK_W + M_V_W]),
        "wo_m": bf(w_in[:, 2 * M_QK_W + M_V_W:2 * M_QK_W + 2 * M_V_W]),
        "wgate_m": bf(gate_w),
        "b_gate": jnp.pad(b_gate_m[0].astype(F32), (0, LANES - 2 * M_HEADS)).reshape(1, LANES),
        "norm_h_m": row(norm_h_m[0]), "w_out_m": bf(w_out_m[0]),
        "wq_a": bf(w_qkv_a[0][:, :e_a]), "wk_a": bf(w_qkv_a[0][:, e_a:2 * e_a]), "wv_a": bf(w_qkv_a[0][:, 2 * e_a:]),
        "norm_q_a": row(norm_q_a[0]), "norm_k_a": row(norm_k_a[0]), "w_out_a": bf(w_out_a[0]),
        "wg": [bf(w_gu[i][:, :FFN_HIDDEN]) for i in range(2)],
        "wu": [bf(w_gu[i][:, FFN_HIDDEN:]) for i in range(2)],
        "wd": [bf(w_down[i]) for i in range(2)],
    }

    def state(c, nn, mm):
        bsz = c.shape[0]
        m_row = jnp.pad(mm.astype(F32), ((0, 0), (0, LANES - M_HEADS))).reshape(bsz, 1, LANES)
        return c.astype(F32), nn.astype(F32).reshape(bsz, M_HEADS, 1, M_QK_DIM), m_row

    def unstate(c, nn, mm, dt):
        bsz = c.shape[0]
        return (c[None].astype(dt), nn.reshape(1, bsz, M_HEADS, M_QK_DIM).astype(dt),
                mm.reshape(bsz, LANES)[None, :, :M_HEADS].astype(dt))

    sd = state_C.dtype
    bp, sp, _ = x_prompt.shape
    zc = jnp.zeros((bp,) + state_C.shape[2:], F32)
    zn = jnp.zeros((bp,) + state_n.shape[2:], F32)
    zm = jnp.zeros((bp,) + state_m.shape[2:], F32)
    yp, pk, pv, pc, pn, pm = _trunk(x_prompt, sp, 0, *state(zc, zn, zm), None, w, 512)

    bs, ss, _ = x_sample.shape
    xs = jnp.pad(x_sample, ((0, 0), (0, DEC_PAD - ss), (0, 0)))
    past_len = page_table.shape[1] * cache_k.shape[2]
    ys, sk, sv, sc, sn, sm = _trunk(xs, ss, past_len, *state(state_C[0], state_n[0], state_m[0]),
                                    (cache_k[0], cache_v[0], page_table), w, 512)

    return (yp, ys, pk[None], pv[None], *unstate(pc, pn, pm, sd), sk[None], sv[None], *unstate(sc, sn, sm, sd))
```

```python
import functools
import math

import jax
import jax.numpy as jnp
from jax import lax
from jax.experimental import pallas as pl
from jax.experimental.pallas import tpu as pltpu

F32 = jnp.float32
BF16 = jnp.bfloat16

D_MODEL = 1024
M_HEADS = 4
M_QK_DIM = 128
M_V_DIM = 256
M_QK_W = M_HEADS * M_QK_DIM
M_V_W = M_HEADS * M_V_DIM
A_HEADS = 8
A_HEAD_DIM = 128
MOBA_BLOCK = 256
MOBA_TOPK = 3
ROPE_THETA = 10000.0
FFN_HIDDEN = 2816
EPS = 1e-6

LANES = 128
SUBLANES = 8
BF16_ROWS = 16
VT_ROWS = A_HEAD_DIM + BF16_ROWS
VMEM_LIMIT_BYTES = 56 * 1024 * 1024

MLSTM_CHUNK = 128
DEC_PAD = SUBLANES
NEG_INF = float("-inf")

_HIGHEST = lax.Precision.HIGHEST
_NT = (((1,), (1,)), ((), ()))
_TN = (((0,), (0,)), ((), ()))


def _params(*sem):
    return pltpu.CompilerParams(dimension_semantics=sem, vmem_limit_bytes=VMEM_LIMIT_BYTES)


def _const_spec(shape):
    nd = len(shape)
    return pl.BlockSpec(shape, lambda *_: (0,) * nd, pipeline_mode=pl.Buffered(1))


def _rms(x, g):
    return x * lax.rsqrt(jnp.mean(x * x, axis=-1, keepdims=True) + EPS) * g


def _norm_proj_kernel(x_ref, g_ref, *refs, n_out):
    w_refs, o_refs = refs[:n_out], refs[n_out:]
    hb = _rms(x_ref[...], g_ref[...]).astype(BF16)
    for w_ref, o_ref in zip(w_refs, o_refs):
        o_ref[...] = jnp.dot(hb, w_ref[...], preferred_element_type=F32)


def _norm_proj(x, g, ws, tm):
    n, d = x.shape
    return pl.pallas_call(
        functools.partial(_norm_proj_kernel, n_out=len(ws)),
        grid=(n // tm,),
        in_specs=[pl.BlockSpec((tm, d), lambda i: (i, 0)), _const_spec((1, d))]
        + [_const_spec(w.shape) for w in ws],
        out_specs=[pl.BlockSpec((tm, w.shape[1]), lambda i: (i, 0)) for w in ws],
        out_shape=[jax.ShapeDtypeStruct((n, w.shape[1]), F32) for w in ws],
        compiler_params=_params("parallel"),
        name="norm_proj",
    )(x, g, *ws)


def _mlstm_kernel(q_ref, k_ref, v_ref, gt_ref, bias_ref, c0_ref, n0_ref, m0_ref,
                  hs_ref, c_ref, n_ref, m_ref, *pad_refs, rows, valid, chunk):
    ci = pl.program_id(1)

    @pl.when(ci == 0)
    def _():
        c_ref[...] = c0_ref[...]
        n_ref[...] = n0_ref[...]
        m_ref[...] = m0_ref[...]

    if rows == chunk:
        q_all, k_all, v_all, gt_all = q_ref[...], k_ref[...], v_ref[...], gt_ref[...]
    else:
        qp, kp, vp, gp = pad_refs

        @pl.when(jnp.logical_and(pl.program_id(0) == 0, ci == 0))
        def _():
            for p in pad_refs:
                p[...] = jnp.zeros(p.shape, p.dtype)

        qp[0:rows, :], kp[0:rows, :], vp[0:rows, :], gp[0:rows, :] = (
            q_ref[...], k_ref[...], v_ref[...], gt_ref[...])
        q_all, k_all, v_all, gt_all = qp[...], kp[...], vp[...], gp[...]

    L = chunk
    row_l = lax.broadcasted_iota(jnp.int32, (L, LANES), 0)
    lane_l = lax.broadcasted_iota(jnp.int32, (L, LANES), 1)
    is_ig = lane_l < M_HEADS
    real = row_l < valid
    gc = gt_all + bias_ref[...]
    lfc = jnp.where(real, jax.nn.log_sigmoid(gc), 0.0)
    igc = jnp.where(real, gc, NEG_INF)
    gsel = jnp.where(is_ig, igc, lfc)

    eye8 = (lax.broadcasted_iota(jnp.int32, (SUBLANES, LANES), 0)
            == lax.broadcasted_iota(jnp.int32, (SUBLANES, LANES), 1)).astype(F32)
    gsel_fin = jnp.where(real, gsel, 0.0)
    rows_g = lax.dot_general(eye8, gsel_fin, _NT, precision=_HIGHEST, preferred_element_type=F32)
    col_real = lax.broadcasted_iota(jnp.int32, (SUBLANES, L), 1) < valid
    rows_is_ig = lax.broadcasted_iota(jnp.int32, (SUBLANES, L), 0) < M_HEADS
    rows_g = jnp.where(jnp.logical_and(rows_is_ig, jnp.logical_not(col_real)), NEG_INF, rows_g)

    tt = lax.broadcasted_iota(jnp.int32, (L, L), 0)
    ss = lax.broadcasted_iota(jnp.int32, (L, L), 1)
    causal = ss <= tt
    tril = causal.astype(F32)
    triu = (tt <= ss).astype(F32)
    b_cols = jnp.dot(tril, lfc, precision=_HIGHEST, preferred_element_type=F32)
    lf_rows = jnp.where(rows_is_ig, 0.0, rows_g)
    b_rows = jnp.dot(lf_rows, triu, precision=_HIGHEST, preferred_element_type=F32)

    m_prev_row = m_ref[0]
    lane_1 = lax.broadcasted_iota(jnp.int32, (1, LANES), 1)
    m_new_row = m_prev_row
    scale = M_QK_DIM ** -0.5
    hd = range(M_HEADS)
    qs = [q_all[0:rows, h * M_QK_DIM:(h + 1) * M_QK_DIM] * scale for h in hd]
    ks = [k_all[:, h * M_QK_DIM:(h + 1) * M_QK_DIM] for h in hd]
    qbs = [q.astype(BF16) for q in qs]
    kbs = [k.astype(BF16) for k in ks]
    vbs = [v_all[:, h * M_V_DIM:(h + 1) * M_V_DIM].astype(BF16) for h in hd]
    c_olds = [c_ref[0, h] for h in hd]
    n_olds = [n_ref[0, h] for h in hd]
    b_cs = [b_cols[:, M_HEADS + h:M_HEADS + h + 1] for h in hd]
    m_prevs = [m_prev_row[:, h:h + 1] for h in hd]

    qk = [lax.dot_general(qb, kb, _NT, preferred_element_type=F32) for qb, kb in zip(qbs, kbs)]
    qc = [jnp.dot(qb, c.astype(BF16), preferred_element_type=F32) for qb, c in zip(qbs, c_olds)]

    a_s, m_ts, w_inters = [], [], []
    for h in hd:
        b_q = b_cs[h][0:rows, :]
        d = jnp.where(causal[0:rows, :], b_q - b_rows[M_HEADS + h:M_HEADS + h + 1, :] + rows_g[h:h + 1, :], NEG_INF)
        m_inter = b_q + m_prevs[h]
        m_t = jnp.maximum(m_inter, jnp.max(d, axis=-1, keepdims=True))
        a_s.append(qk[h] * jnp.exp(d - m_t))
        m_ts.append(m_t)
        w_inters.append(jnp.exp(m_inter - m_t))
    av = [jnp.dot(a.astype(BF16), vb, preferred_element_type=F32) for a, vb in zip(a_s, vbs)]

    kws, w_states = [], []
    for h in hd:
        num = w_inters[h] * qc[h] + av[h]
        den = (w_inters[h] * jnp.sum(qs[h] * n_olds[h], axis=-1, keepdims=True)
               + jnp.sum(a_s[h], axis=-1, keepdims=True))
        hval = num / jnp.maximum(jnp.abs(den), jnp.exp(-m_ts[h]))
        hs_ref[:, h * M_V_DIM:(h + 1) * M_V_DIM] = hval
        m_new = m_ts[h][rows - 1:rows, :]
        b_last = b_cs[h][rows - 1:rows, :]
        w_states.append(jnp.exp(b_last + m_prevs[h] - m_new))
        kws.append(ks[h] * jnp.exp(b_last - b_cs[h] + gsel[:, h:h + 1] - m_new))
        m_new_row = jnp.where(lane_1 == h, m_new, m_new_row)
    kv = [lax.dot_general(kw.astype(BF16), vb, _TN, preferred_element_type=F32) for kw, vb in zip(kws, vbs)]
    for h in hd:
        c_ref[0, h] = w_states[h] * c_olds[h] + kv[h]
        n_ref[0, h] = w_states[h] * n_olds[h] + jnp.sum(kws[h], axis=0, keepdims=True)
    m_ref[0] = m_new_row


def _mlstm(q, k, v, gates, bias, c0, n0, m0, valid):
    b, s, _ = q.shape
    if s >= MLSTM_CHUNK:
        rows, nc = MLSTM_CHUNK, s // MLSTM_CHUNK
        assert valid == s and s % MLSTM_CHUNK == 0
        scratch = []
    else:
        rows, nc = s, 1
        scratch = [pltpu.VMEM((MLSTM_CHUNK, w), F32) for w in (M_QK_W, M_QK_W, M_V_W, LANES)]
    tok = lambda w: pl.BlockSpec((None, rows, w), lambda bi, ci: (bi, ci, 0))
    st_c = pl.BlockSpec((1, M_HEADS, M_QK_DIM, M_V_DIM), lambda bi, ci: (bi, 0, 0, 0))
    st_n = pl.BlockSpec((1, M_HEADS, 1, M_QK_DIM), lambda bi, ci: (bi, 0, 0, 0))
    st_m = pl.BlockSpec((1, 1, LANES), lambda bi, ci: (bi, 0, 0))
    return pl.pallas_call(
        functools.partial(_mlstm_kernel, rows=rows, valid=valid, chunk=MLSTM_CHUNK),
        grid=(b, nc),
        in_specs=[tok(M_QK_W), tok(M_QK_W), tok(M_V_W), tok(LANES), _const_spec((1, LANES)), st_c, st_n, st_m],
        out_specs=[tok(M_V_W), st_c, st_n, st_m],
        out_shape=[jax.ShapeDtypeStruct((b, s, M_V_W), F32), jax.ShapeDtypeStruct(c0.shape, F32),
                   jax.ShapeDtypeStruct(n0.shape, F32), jax.ShapeDtypeStruct(m0.shape, F32)],
        scratch_shapes=scratch,
        compiler_params=_params("arbitrary" if scratch else "parallel", "arbitrary"),
        name="mlstm_chunks",
    )(q, k, v, gates, bias, c0, n0, m0)


FFN_CHUNKS = 2


def _ffn_residual(x, g_ref, wg_ref, wu_ref, wd_ref):
    hb = _rms(x, g_ref[...]).astype(BF16)
    acc = x
    cw = FFN_HIDDEN // FFN_CHUNKS
    for c in range(FFN_CHUNKS):
        sl = slice(c * cw, (c + 1) * cw)
        gg = jnp.dot(hb, wg_ref[:, sl], preferred_element_type=F32)
        uu = jnp.dot(hb, wu_ref[:, sl], preferred_element_type=F32)
        act = (jax.nn.silu(gg) * uu).astype(BF16)
        acc = acc + jnp.dot(act, wd_ref[sl, :], preferred_element_type=F32)
    return acc


def _mlstm_out_ffn_kernel(hs_ref, o_ref, x_ref, gh_ref, w_ref, g_ref, wg_ref, wu_ref, wd_ref, out_ref):
    parts = []
    for h in range(M_HEADS):
        sl = slice(h * M_V_DIM, (h + 1) * M_V_DIM)
        parts.append(_rms(hs_ref[:, sl], gh_ref[:, sl]))
    y = jnp.concatenate(parts, axis=-1) * jax.nn.sigmoid(o_ref[...])
    x1 = x_ref[...] + jnp.dot(y.astype(BF16), w_ref[...], preferred_element_type=F32)
    out_ref[...] = _ffn_residual(x1, g_ref, wg_ref, wu_ref, wd_ref)


def _proj_ffn_kernel(a_ref, x_ref, w_ref, g_ref, wg_ref, wu_ref, wd_ref, out_ref):
    x1 = x_ref[...] + jnp.dot(a_ref[...].astype(BF16), w_ref[...], preferred_element_type=F32)
    out_ref[...] = _ffn_residual(x1, g_ref, wg_ref, wu_ref, wd_ref)


def _mixer_out_ffn(acts, x, consts, tm, mlstm):
    n, d = x.shape
    row = lambda a: pl.BlockSpec((tm, a.shape[1]), lambda i: (i, 0))
    return pl.pallas_call(
        _mlstm_out_ffn_kernel if mlstm else _proj_ffn_kernel,
        grid=(n // tm,),
        in_specs=[row(a) for a in acts] + [row(x)] + [_const_spec(c.shape) for c in consts],
        out_specs=row(x),
        out_shape=jax.ShapeDtypeStruct((n, d), F32),
        compiler_params=_params("parallel"),
        name="mlstm_out_ffn" if mlstm else "attn_out_ffn",
    )(*acts, x, *consts)


def _qkv_kernel(x_ref, g_ref, wq_ref, wk_ref, wv_ref, gq_ref, gk_ref, cos_ref, sin_ref,
                q_ref, k_ref, v_ref, *blk_refs):
    hb = _rms(x_ref[...], g_ref[...]).astype(BF16)
    cos, sin = cos_ref[...], sin_ref[...]

    kb_ref, vt_ref, km_ref = blk_refs if blk_refs else (None, None, None)
    tm = x_ref.shape[0]
    rt = min(tm, MOBA_BLOCK)
    cw = 2 * A_HEAD_DIM

    def norm_rope(t, gh_ref, rows):
        y = _rms(t, gh_ref[...])
        return y * cos[rows, :] + pltpu.roll(y, A_HEAD_DIM // 2, 1) * sin[rows, :]

    for j in range(tm // rt):
        rows = slice(j * rt, (j + 1) * rt)
        hr = hb[rows, :]
        for c in range(A_HEADS // 2):
            t = jnp.dot(hr, wq_ref[:, c * cw:(c + 1) * cw], preferred_element_type=F32)
            for i in range(2):
                sl = slice((2 * c + i) * A_HEAD_DIM, (2 * c + i + 1) * A_HEAD_DIM)
                q_ref[rows, sl] = norm_rope(t[:, i * A_HEAD_DIM:(i + 1) * A_HEAD_DIM], gq_ref, rows)
        for c in range(A_HEADS // 2):
            t = jnp.dot(hr, wk_ref[:, c * cw:(c + 1) * cw], preferred_element_type=F32)
            for i in range(2):
                sl = slice((2 * c + i) * A_HEAD_DIM, (2 * c + i + 1) * A_HEAD_DIM)
                k = norm_rope(t[:, i * A_HEAD_DIM:(i + 1) * A_HEAD_DIM], gk_ref, rows)
                k_ref[rows, sl] = k
                if blk_refs:
                    kb_ref[rows, sl] = k.astype(BF16)
                    mean = jnp.mean(k, axis=0, keepdims=True)
                    km_ref[j, :, sl] = jnp.broadcast_to(mean, (SUBLANES, A_HEAD_DIM))
        for c in range(A_HEADS // 2):
            t = jnp.dot(hr, wv_ref[:, c * cw:(c + 1) * cw], preferred_element_type=F32)
            v_ref[rows, c * cw:(c + 1) * cw] = t
            if blk_refs:
                for i in range(2):
                    r0 = (2 * c + i) * VT_ROWS
                    vt_ref[j, r0:r0 + A_HEAD_DIM, :] = t[:, i * A_HEAD_DIM:(i + 1) * A_HEAD_DIM].T.astype(BF16)
                    vt_ref[j, r0 + A_HEAD_DIM:r0 + VT_ROWS, :] = jnp.ones((BF16_ROWS, MOBA_BLOCK), BF16)


def _qkv(x, g, wq, wk, wv, gq, gk, cos, sin, tm, with_blocks):
    n, d = x.shape
    e = wq.shape[1]
    n_tab = cos.shape[0] // tm
    row = lambda wd: pl.BlockSpec((tm, wd), lambda i: (i, 0))
    tab = pl.BlockSpec((tm, A_HEAD_DIM), lambda i: (i % n_tab, 0))
    out_specs = [row(e), row(e), row(e)]
    out_shape = [jax.ShapeDtypeStruct((n, e), F32)] * 3
    if with_blocks:
        assert tm % MOBA_BLOCK == 0
        gpt = tm // MOBA_BLOCK
        vrows = A_HEADS * VT_ROWS
        out_specs += [row(e), pl.BlockSpec((gpt, vrows, MOBA_BLOCK), lambda i: (i, 0, 0)),
                      pl.BlockSpec((gpt, SUBLANES, e), lambda i: (i, 0, 0))]
        out_shape += [jax.ShapeDtypeStruct((n, e), BF16),
                      jax.ShapeDtypeStruct((n // MOBA_BLOCK, vrows, MOBA_BLOCK), BF16),
                      jax.ShapeDtypeStruct((n // MOBA_BLOCK, SUBLANES, e), F32)]
    return pl.pallas_call(
        _qkv_kernel,
        grid=(n // tm,),
        in_specs=[row(d), _const_spec((1, d)), _const_spec(wq.shape), _const_spec(wk.shape),
                  _const_spec(wv.shape), _const_spec((1, A_HEAD_DIM)), _const_spec((1, A_HEAD_DIM)), tab, tab],
        out_specs=out_specs,
        out_shape=out_shape,
        compiler_params=_params("parallel"),
        name="qkv_rope",
    )(x, g, wq, wk, wv, gq, gk, cos, sin)


def _rope_tables(pos):
    half = A_HEAD_DIM // 2
    inv_freq = ROPE_THETA ** (-jnp.arange(half, dtype=F32) / half)
    ang = pos.astype(F32)[:, None] * inv_freq[None, :]
    c, s = jnp.cos(ang), jnp.sin(ang)
    return jnp.concatenate([c, c], axis=-1), jnp.concatenate([-s, s], axis=-1)


MOBA_HEADS_PER_STEP = 4
MASK_BIG = 2.0 ** 100


def _top_blocks_t(scores_t, allowed):
    row = lax.broadcasted_iota(jnp.int32, scores_t.shape, 0)
    s = jnp.where(allowed, scores_t, NEG_INF)
    picked = jnp.zeros(scores_t.shape, F32)
    for _ in range(MOBA_TOPK):
        mx = jnp.max(s, axis=0, keepdims=True)
        cand = jnp.where(jnp.logical_and(s == mx, s > NEG_INF), row, scores_t.shape[0])
        hit = row == jnp.min(cand, axis=0, keepdims=True)
        picked = jnp.where(hit, 1.0, picked)
        s = jnp.where(hit, NEG_INF, s)
    return picked


def _moba_prompt_kernel(q_ref, kb_ref, vt_ref, km_ref, o_ref, sa_ref, sb_ref, *, n_blocks, hp):
    own = pl.program_id(2)
    blk = MOBA_BLOCK
    scale = A_HEAD_DIM ** -0.5
    nbp = -(-n_blocks // BF16_ROWS) * BF16_ROWS
    lane = lax.broadcasted_iota(jnp.int32, (blk, LANES), 1)
    blk_i = lax.broadcasted_iota(jnp.int32, (nbp, blk), 0)
    heads = [slice(h * A_HEAD_DIM, (h + 1) * A_HEAD_DIM) for h in range(hp)]
    vrows = [slice(h * VT_ROWS, (h + 1) * VT_ROWS) for h in range(hp)]

    q_t = [q_ref[:, sl].T for sl in heads]
    q_tb = [(qt * scale).astype(BF16) for qt in q_t]
    own_start = pl.multiple_of(own * blk, blk)
    own_logits = [jnp.dot(kb_ref[pl.ds(own_start, blk), sl], qb, preferred_element_type=F32)
                  for sl, qb in zip(heads, q_tb)]
    kms = []
    for sl in heads:
        km = km_ref[:, sl]
        if nbp > n_blocks:
            km = jnp.concatenate([km, jnp.zeros((nbp - n_blocks, A_HEAD_DIM), F32)], axis=0)
        kms.append(km)
    scores = [jnp.dot(km, qt, precision=_HIGHEST, preferred_element_type=F32) for km, qt in zip(kms, q_t)]
    q_ext = []
    for qb, sc in zip(q_tb, scores):
        picked = _top_blocks_t(sc, blk_i < own)
        pad = jnp.zeros((LANES - nbp, blk), BF16)
        q_ext.append(jnp.concatenate([qb, (1.0 - picked).astype(BF16), pad], axis=0))

    def stage_logits(ref, j):
        st = pl.multiple_of(j * blk, blk)
        mask_cols = jnp.where(lane == j, -MASK_BIG, 0.0).astype(BF16)
        for h, (sl, qe) in enumerate(zip(heads, q_ext)):
            k_ext = jnp.concatenate([kb_ref[pl.ds(st, blk), sl], mask_cols], axis=1)
            ref[h] = jnp.dot(k_ext, qe, preferred_element_type=F32)

    stage_logits(sb_ref, 0)

    def accumulate(j, logits, state):
        probs, stats = [], []
        for s, (m, _) in zip(logits, state):
            m_new = jnp.maximum(m, jnp.max(s, axis=0, keepdims=True))
            probs.append(jnp.exp(s - m_new).astype(BF16))
            stats.append((m_new, jnp.exp(m - m_new)))
        return tuple(
            (m_new, alpha * acc + jnp.dot(vt_ref[j, vr, :], pb, preferred_element_type=F32))
            for vr, pb, (m_new, alpha), (_, acc) in zip(vrows, probs, stats, state))

    def accumulate_staged(j, ref, state):
        return accumulate(j, tuple(ref[h] for h in range(hp)), state)

    key_i = lax.broadcasted_iota(jnp.int32, (blk, blk), 0)
    qry_i = lax.broadcasted_iota(jnp.int32, (blk, blk), 1)
    state = []
    for s, vr in zip(own_logits, vrows):
        s = jnp.where(key_i <= qry_i, s, NEG_INF)
        m = jnp.max(s, axis=0, keepdims=True)
        state.append((m, jnp.dot(vt_ref[own, vr, :], jnp.exp(s - m).astype(BF16), preferred_element_type=F32)))
    state = tuple(state)

    def pair(i, state):
        j = 2 * i
        stage_logits(sa_ref, j + 1)
        state = accumulate_staged(j, sb_ref, state)
        stage_logits(sb_ref, j + 2)
        return accumulate_staged(j + 1, sa_ref, state)

    state = lax.fori_loop(0, lax.shift_right_logical(own, 1), pair, state)
    state = lax.fori_loop(0, jnp.bitwise_and(own, 1), lambda _, st: accumulate_staged(own - 1, sb_ref, st), state)
    for sl, (_, acc) in zip(heads, state):
        o_ref[:, sl] = (acc[0:A_HEAD_DIM, :] / acc[A_HEAD_DIM:A_HEAD_DIM + 1, :]).T


def _moba_prompt(q, kb, vt, kmean):
    b, s, e = q.shape
    nb = s // MOBA_BLOCK
    hp = MOBA_HEADS_PER_STEP
    assert s % MOBA_BLOCK == 0 and nb <= LANES and A_HEADS % hp == 0
    w = hp * A_HEAD_DIM
    tile = pl.BlockSpec((None, MOBA_BLOCK, w), lambda bi, h, qi: (bi, qi, h))
    return pl.pallas_call(
        functools.partial(_moba_prompt_kernel, n_blocks=nb, hp=hp),
        grid=(b, A_HEADS // hp, nb),
        in_specs=[tile, pl.BlockSpec((None, s, w), lambda bi, h, qi: (bi, 0, h)),
                  pl.BlockSpec((None, nb, hp * VT_ROWS, MOBA_BLOCK), lambda bi, h, qi: (bi, 0, h, 0)),
                  pl.BlockSpec((None, nb, w), lambda bi, h, qi: (bi, 0, h))],
        out_specs=tile,
        out_shape=jax.ShapeDtypeStruct((b, s, e), F32),
        scratch_shapes=[pltpu.VMEM((hp, MOBA_BLOCK, MOBA_BLOCK), F32)] * 2,
        compiler_params=_params("parallel", "parallel", "arbitrary"),
        name="moba_prompt",
    )(q, kb, vt, kmean)


def _moba_decode_kernel(pt_ref, q_ref, kn_ref, vn_ref, ck_ref, cv_ref, o_ref, kbuf, vbuf, sem, logit_ref,
                        *, n_pages, page_rows):
    bi = pl.program_id(0)
    n_seq = pl.num_programs(0)
    slot = lax.rem(bi, 2)
    R = A_HEADS * DEC_PAD
    scale = A_HEAD_DIM ** -0.5
    pages_per_block = MOBA_BLOCK // page_rows
    n_blocks = n_pages // pages_per_block

    def page_copy(cache_ref, buf_ref, which, seq, sl, g):
        page = pt_ref[seq * n_pages + g]
        return pltpu.make_async_copy(cache_ref.at[page], buf_ref.at[sl, g], sem.at[sl, which])

    def start_pages(seq, sl):
        for g in range(n_pages):
            page_copy(ck_ref, kbuf, 0, seq, sl, g).start()
        for g in range(n_pages):
            page_copy(cv_ref, vbuf, 1, seq, sl, g).start()

    @pl.when(bi == 0)
    def _():
        start_pages(bi, slot)

    @pl.when(bi + 1 < n_seq)
    def _():
        start_pages(bi + 1, 1 - slot)

    def rows_by_head(ref):
        return jnp.concatenate([ref[:, h * A_HEAD_DIM:(h + 1) * A_HEAD_DIM] for h in range(A_HEADS)], axis=0)

    def by_head(ref):
        return jnp.concatenate([ref[pl.ds(h, page_rows, stride=A_HEADS), :].astype(BF16) for h in range(A_HEADS)],
                               axis=0)

    q = rows_by_head(q_ref)
    qb = q.astype(BF16)

    for g in range(n_pages):
        page_copy(ck_ref, kbuf, 0, bi, slot, g).wait()
    ksum = []
    for g in range(n_pages):
        kpage = kbuf.at[slot, g]
        lg = lax.dot_general(qb, by_head(kpage), _NT, preferred_element_type=F32)
        for h in range(A_HEADS):
            hr = slice(h * DEC_PAD, (h + 1) * DEC_PAD)
            logit_ref[g, hr, :] = lg[hr, h * page_rows:(h + 1) * page_rows] * scale
        part = jnp.sum(kpage[...].reshape(page_rows, A_HEADS, A_HEAD_DIM), axis=0)
        if g % pages_per_block == 0:
            ksum.append(part)
        else:
            ksum[-1] = ksum[-1] + part

    kmean = jnp.concatenate(ksum, axis=0) * (1.0 / MOBA_BLOCK)
    nbh = n_blocks * A_HEADS
    scores_t = lax.dot_general(kmean, q, _NT, precision=_HIGHEST, preferred_element_type=F32)
    brow = lax.broadcasted_iota(jnp.int32, (nbh, R), 0)
    qcol = lax.broadcasted_iota(jnp.int32, (nbh, R), 1)
    picked_t = _top_blocks_t(scores_t, brow % A_HEADS == qcol // DEC_PAD)
    erow = lax.broadcasted_iota(jnp.int32, (nbh, n_blocks * page_rows), 0)
    ecol = lax.broadcasted_iota(jnp.int32, (nbh, n_blocks * page_rows), 1)
    expand = jnp.where(erow // A_HEADS == ecol // page_rows, 1.0, 0.0).astype(BF16)
    keep = lax.dot_general(picked_t.astype(BF16), expand, _TN, preferred_element_type=F32)
    keep_blk = [keep[:, j * page_rows:(j + 1) * page_rows] > 0.5 for j in range(n_blocks)]

    kn = rows_by_head(kn_ref).astype(BF16)
    ln = lax.dot_general(qb, kn, _NT, preferred_element_type=F32) * scale
    rn = lax.broadcasted_iota(jnp.int32, (R, R), 0)
    cn = lax.broadcasted_iota(jnp.int32, (R, R), 1)
    ok_n = jnp.logical_and(rn // DEC_PAD == cn // DEC_PAD, cn % DEC_PAD <= rn % DEC_PAD)
    ln = jnp.where(ok_n, ln, NEG_INF)

    masked = [jnp.where(keep_blk[g // pages_per_block], logit_ref[g], NEG_INF) for g in range(n_pages)]
    mm = masked[0]
    for g in range(1, n_pages):
        mm = jnp.maximum(mm, masked[g])
    m = jnp.maximum(jnp.max(ln, axis=-1, keepdims=True), jnp.max(mm, axis=-1, keepdims=True))
    pn = jnp.exp(ln - m)
    probs = [jnp.exp(s - m) for s in masked]
    lsum = probs[0]
    for g in range(1, n_pages):
        lsum = lsum + probs[g]
    l = jnp.sum(pn, axis=-1, keepdims=True) + jnp.sum(lsum, axis=-1, keepdims=True)

    for g in range(n_pages):
        page_copy(cv_ref, vbuf, 1, bi, slot, g).wait()
    acc = jnp.dot(pn.astype(BF16), rows_by_head(vn_ref).astype(BF16), preferred_element_type=F32)
    zero = jnp.zeros((DEC_PAD, page_rows), F32)
    for g in range(n_pages):
        pe = probs[g]
        p_bd = jnp.concatenate(
            [jnp.concatenate([pe[h * DEC_PAD:(h + 1) * DEC_PAD, :] if hh == h else zero for hh in range(A_HEADS)],
                             axis=1) for h in range(A_HEADS)], axis=0)
        acc = acc + jnp.dot(p_bd.astype(BF16), by_head(vbuf.at[slot, g]), preferred_element_type=F32)
    res = acc / l
    for h in range(A_HEADS):
        o_ref[:, h * A_HEAD_DIM:(h + 1) * A_HEAD_DIM] = res[h * DEC_PAD:(h + 1) * DEC_PAD, :]


def _moba_decode(q, kn, vn, cache_k, cache_v, page_table):
    b, t, e = q.shape
    n_phys, page_rows = cache_k.shape[0], cache_k.shape[1]
    n_pages = page_table.shape[1]
    assert t == DEC_PAD and MOBA_BLOCK % page_rows == 0 and n_pages % (MOBA_BLOCK // page_rows) == 0
    cols = page_rows * A_HEADS
    ck = cache_k.reshape(n_phys, cols, A_HEAD_DIM)
    cv = cache_v.reshape(n_phys, cols, A_HEAD_DIM)
    pt = page_table.reshape(-1).astype(jnp.int32)

    tok = pl.BlockSpec((None, t, e), lambda bi, pt_: (bi, 0, 0))
    hbm = pl.BlockSpec(memory_space=pl.ANY)
    grid_spec = pltpu.PrefetchScalarGridSpec(
        num_scalar_prefetch=1,
        grid=(b,),
        in_specs=[tok, tok, tok, hbm, hbm],
        out_specs=tok,
        scratch_shapes=[pltpu.VMEM((2, n_pages, cols, A_HEAD_DIM), F32),
                        pltpu.VMEM((2, n_pages, cols, A_HEAD_DIM), F32),
                        pltpu.SemaphoreType.DMA((2, 2)),
                        pltpu.VMEM((n_pages, A_HEADS * DEC_PAD, page_rows), F32)],
    )
    return pl.pallas_call(
        functools.partial(_moba_decode_kernel, n_pages=n_pages, page_rows=page_rows),
        grid_spec=grid_spec,
        out_shape=jax.ShapeDtypeStruct((b, t, e), F32),
        compiler_params=_params("arbitrary"),
        name="moba_decode",
    )(pt, q, kn, vn, ck, cv)


def _trunk(x, valid, pos0, c0, n0, m0, past, w, tm):
    b, s, d = x.shape
    n = b * s
    tm = min(tm, n)
    assert n % tm == 0 and (tm % s == 0 or s % tm == 0)
    xf = x.reshape(n, d)

    q, k, v, o, gates = _norm_proj(xf, w["norm_mix"][0], [w["wq_m"], w["wk_m"], w["wv_m"], w["wo_m"], w["wgate_m"]], tm)
    r3 = lambda t: t.reshape(b, s, t.shape[-1])
    hs, c_new, n_new, m_new = _mlstm(r3(q), r3(k), r3(v), r3(gates), w["b_gate"], c0, n0, m0, valid)
    xf = _mixer_out_ffn((hs.reshape(n, M_V_W), o), xf, (w["norm_h_m"], w["w_out_m"], w["norm_ffn"][0], w["wg"][0],
                                                         w["wu"][0], w["wd"][0]), tm, mlstm=True)

    pos = pos0 + jnp.arange(s, dtype=jnp.int32)
    cos, sin = _rope_tables(pos)
    if s < tm:
        cos, sin = jnp.tile(cos, (tm // s, 1)), jnp.tile(sin, (tm // s, 1))
    qa, ka, va, *blocks = _qkv(xf, w["norm_mix"][1], w["wq_a"], w["wk_a"], w["wv_a"], w["norm_q_a"],
                               w["norm_k_a"], cos, sin, tm, with_blocks=past is None)
    if past is None:
        kb, vt, kmean = blocks
        nb = s // MOBA_BLOCK
        att = _moba_prompt(r3(qa), r3(kb), vt.reshape(b, nb, -1, MOBA_BLOCK), kmean[:, 0, :].reshape(b, nb, -1))
    else:
        att = _moba_decode(r3(qa), r3(ka), r3(va), *past)
    xf = _mixer_out_ffn((att.reshape(n, -1),), xf, (w["w_out_a"], w["norm_ffn"][1], w["wg"][1], w["wu"][1],
                                                    w["wd"][1]), tm, mlstm=False)

    heads = lambda t: t.reshape(b, s, A_HEADS, A_HEAD_DIM)[:, :valid]
    return xf.reshape(b, s, d)[:, :valid], heads(ka), heads(va), c_new, n_new, m_new


def kernel(x_prompt, x_sample, state_C, state_n, state_m, cache_k, cache_v, page_table, norm_mix, norm_ffn,
           w_in_m, b_gate_m, norm_h_m, w_out_m, w_qkv_a, norm_q_a, norm_k_a, w_out_a, w_gu, w_down):
    assert norm_mix.shape[0] == 2 and w_in_m.shape[0] == 1 and w_qkv_a.shape[0] == 1
    bf = lambda t: t.astype(BF16)
    row = lambda t: t.reshape(1, -1).astype(F32)
    w_in = w_in_m[0]
    e_a = A_HEADS * A_HEAD_DIM
    gate_w = jnp.pad(w_in[:, 2 * M_QK_W + 2 * M_V_W:], ((0, 0), (0, LANES - 2 * M_HEADS)))
    w = {
        "norm_mix": [row(norm_mix[0]), row(norm_mix[1])],
        "norm_ffn": [row(norm_ffn[0]), row(norm_ffn[1])],
        "wq_m": bf(w_in[:, :M_QK_W]), "wk_m": bf(w_in[:, M_QK_W:2 * M_QK_W]),
        "wv_m": bf(w_in[:, 2 * M_QK_W:2 * M_QK_W + M_V_W]),
        "wo_m": bf(w_in[:, 2 * M_QK_W + M_V_W:2 * M_QK_W + 2 * M_V_W]),
        "wgate_m": bf(gate_w),
        "b_gate": jnp.pad(b_gate_m[0].astype(F32), (0, LANES - 2 * M_HEADS)).reshape(1, LANES),
        "norm_h_m": row(norm_h_m[0]), "w_out_m": bf(w_out_m[0]),
        "wq_a": bf(w_qkv_a[0][:, :e_a]), "wk_a": bf(w_qkv_a[0][:, e_a:2 * e_a]), "wv_a": bf(w_qkv_a[0][:, 2 * e_a:]),
        "norm_q_a": row(norm_q_a[0]), "norm_k_a": row(norm_k_a[0]), "w_out_a": bf(w_out_a[0]),
        "wg": [bf(w_gu[i][:, :FFN_HIDDEN]) for i in range(2)],
        "wu": [bf(w_gu[i][:, FFN_HIDDEN:]) for i in range(2)],
        "wd": [bf(w_down[i]) for i in range(2)],
    }

    def state(c, nn, mm):
        bsz = c.shape[0]
        m_row = jnp.pad(mm.astype(F32), ((0, 0), (0, LANES - M_HEADS))).reshape(bsz, 1, LANES)
        return c.astype(F32), nn.astype(F32).reshape(bsz, M_HEADS, 1, M_QK_DIM), m_row

    def unstate(c, nn, mm, dt):
        bsz = c.shape[0]
        return (c[None].astype(dt), nn.reshape(1, bsz, M_HEADS, M_QK_DIM).astype(dt),
                mm.reshape(bsz, LANES)[None, :, :M_HEADS].astype(dt))

    sd = state_C.dtype
    bp, sp, _ = x_prompt.shape
    zc = jnp.zeros((bp,) + state_C.shape[2:], F32)
    zn = jnp.zeros((bp,) + state_n.shape[2:], F32)
    zm = jnp.zeros((bp,) + state_m.shape[2:], F32)
    yp, pk, pv, pc, pn, pm = _trunk(x_prompt, sp, 0, *state(zc, zn, zm), None, w, 512)

    bs, ss, _ = x_sample.shape
    xs = jnp.pad(x_sample, ((0, 0), (0, DEC_PAD - ss), (0, 0)))
    past_len = page_table.shape[1] * cache_k.shape[2]
    ys, sk, sv, sc, sn, sm = _trunk(xs, ss, past_len, *state(state_C[0], state_n[0], state_m[0]),
                                    (cache_k[0], cache_v[0], page_table), w, 512)

    return (yp, ys, pk[None], pv[None], *unstate(pc, pn, pm, sd), sk[None], sv[None], *unstate(sc, sn, sm, sd))
```

```python
import functools
import math

import jax
import jax.numpy as jnp
from jax import lax
from jax.experimental import pallas as pl
from jax.experimental.pallas import tpu as pltpu

F32 = jnp.float32
BF16 = jnp.bfloat16

D_MODEL = 1024
M_HEADS = 4
M_QK_DIM = 128
M_V_DIM = 256
M_QK_W = M_HEADS * M_QK_DIM
M_V_W = M_HEADS * M_V_DIM
A_HEADS = 8
A_HEAD_DIM = 128
MOBA_BLOCK = 256
MOBA_TOPK = 3
ROPE_THETA = 10000.0
FFN_HIDDEN = 2816
EPS = 1e-6

LANES = 128
SUBLANES = 8
BF16_ROWS = 16
VT_ROWS = A_HEAD_DIM + BF16_ROWS
VMEM_LIMIT_BYTES = 56 * 1024 * 1024

MLSTM_CHUNK = 128
DEC_PAD = SUBLANES
NEG_INF = float("-inf")

_HIGHEST = lax.Precision.HIGHEST
_NT = (((1,), (1,)), ((), ()))
_TN = (((0,), (0,)), ((), ()))


def _params(*sem):
    return pltpu.CompilerParams(dimension_semantics=sem, vmem_limit_bytes=VMEM_LIMIT_BYTES)


def _const_spec(shape):
    nd = len(shape)
    return pl.BlockSpec(shape, lambda *_: (0,) * nd, pipeline_mode=pl.Buffered(1))


def _rms(x, g):
    return x * lax.rsqrt(jnp.mean(x * x, axis=-1, keepdims=True) + EPS) * g


def _norm_proj_kernel(x_ref, g_ref, *refs, n_out):
    w_refs, o_refs = refs[:n_out], refs[n_out:]
    hb = _rms(x_ref[...], g_ref[...]).astype(BF16)
    for w_ref, o_ref in zip(w_refs, o_refs):
        o_ref[...] = jnp.dot(hb, w_ref[...], preferred_element_type=F32)


def _norm_proj(x, g, ws, tm):
    n, d = x.shape
    return pl.pallas_call(
        functools.partial(_norm_proj_kernel, n_out=len(ws)),
        grid=(n // tm,),
        in_specs=[pl.BlockSpec((tm, d), lambda i: (i, 0)), _const_spec((1, d))]
        + [_const_spec(w.shape) for w in ws],
        out_specs=[pl.BlockSpec((tm, w.shape[1]), lambda i: (i, 0)) for w in ws],
        out_shape=[jax.ShapeDtypeStruct((n, w.shape[1]), F32) for w in ws],
        compiler_params=_params("parallel"),
        name="norm_proj",
    )(x, g, *ws)


def _mlstm_kernel(q_ref, k_ref, v_ref, gt_ref, bias_ref, c0_ref, n0_ref, m0_ref,
                  hs_ref, c_ref, n_ref, m_ref, *pad_refs, rows, valid, chunk, single_chunk):
    nseq = q_ref.shape[0]
    ci = pl.program_id(1)
    if single_chunk:
        c_src, n_src, m_src = c0_ref, n0_ref, m0_ref
    else:
        c_src, n_src, m_src = c_ref, n_ref, m_ref

        @pl.when(ci == 0)
        def _():
            c_ref[...] = c0_ref[...]
            n_ref[...] = n0_ref[...]
            m_ref[...] = m0_ref[...]

    if rows != chunk:
        @pl.when(jnp.logical_and(pl.program_id(0) == 0, ci == 0))
        def _():
            for p in pad_refs:
                p[...] = jnp.zeros(p.shape, p.dtype)

    L = chunk
    row_l = lax.broadcasted_iota(jnp.int32, (L, LANES), 0)
    lane_l = lax.broadcasted_iota(jnp.int32, (L, LANES), 1)
    is_ig = lane_l < M_HEADS
    real = row_l < valid
    eye8 = (lax.broadcasted_iota(jnp.int32, (SUBLANES, LANES), 0)
            == lax.broadcasted_iota(jnp.int32, (SUBLANES, LANES), 1)).astype(F32)
    col_real = lax.broadcasted_iota(jnp.int32, (SUBLANES, L), 1) < valid
    causal = lax.broadcasted_iota(jnp.int32, (L, L), 1) <= lax.broadcasted_iota(jnp.int32, (L, L), 0)
    lane_1 = lax.broadcasted_iota(jnp.int32, (1, LANES), 1)
    scale = M_QK_DIM ** -0.5

    tiles, gsels, b_colss, cols_l = [], [], [], []
    for sq in range(nseq):
        if rows == chunk:
            tiles.append((q_ref[sq], k_ref[sq], v_ref[sq]))
            gt_all = gt_ref[sq]
        else:
            qp, kp, vp, gp = pad_refs
            qp[sq, 0:rows, :], kp[sq, 0:rows, :], vp[sq, 0:rows, :], gp[sq, 0:rows, :] = (
                q_ref[sq], k_ref[sq], v_ref[sq], gt_ref[sq])
            tiles.append((qp[sq], kp[sq], vp[sq]))
            gt_all = gp[sq]
        gc = gt_all + bias_ref[...]
        lfc = jnp.where(real, jax.nn.log_sigmoid(gc), 0.0)
        igc = jnp.where(real, gc, NEG_INF)
        gsels.append(jnp.where(is_ig, igc, lfc))
        b_cols = lfc
        shift = 1
        while shift < L:
            b_cols = b_cols + jnp.where(row_l >= shift, pltpu.roll(b_cols, shift, 0), 0.0)
            shift *= 2
        b_colss.append(b_cols)
        cols_l.append(jnp.where(is_ig, jnp.where(real, gc, 0.0), b_cols))
    rows_ts = [lax.dot_general(eye8, c, _NT, precision=_HIGHEST, preferred_element_type=F32) for c in cols_l]

    ch = [(sq, h) for sq in range(nseq) for h in range(M_HEADS)]
    qs = [tiles[sq][0][0:rows, h * M_QK_DIM:(h + 1) * M_QK_DIM] * scale for sq, h in ch]
    ks = [tiles[sq][1][:, h * M_QK_DIM:(h + 1) * M_QK_DIM] for sq, h in ch]
    qbs = [q.astype(BF16) for q in qs]
    kbs = [k.astype(BF16) for k in ks]
    vbs = [tiles[sq][2][:, h * M_V_DIM:(h + 1) * M_V_DIM].astype(BF16) for sq, h in ch]
    c_olds = [c_src[sq, h] for sq, h in ch]
    n_olds = [n_src[sq, h] for sq, h in ch]
    b_cs = [b_colss[sq][:, M_HEADS + h:M_HEADS + h + 1] for sq, h in ch]
    m_prevs = [m_src[sq][:, h:h + 1] for sq, h in ch]

    qk = [lax.dot_general(qb, kb, _NT, preferred_element_type=F32) for qb, kb in zip(qbs, kbs)]
    qc = [jnp.dot(qb, c.astype(BF16), preferred_element_type=F32) for qb, c in zip(qbs, c_olds)]

    twice = lambda t: jnp.concatenate([t, t], axis=1)
    a_s, m_ts, w_inters, b_reps = [], [], [], []
    for i, (sq, h) in enumerate(ch):
        b_rep = jnp.broadcast_to(b_cs[i], (L, LANES))
        b_q = b_rep[0:rows, :]
        ig_row = jnp.where(col_real, rows_ts[sq], NEG_INF)[h:h + 1, :]
        d = jnp.where(causal[0:rows, :], b_q - rows_ts[sq][M_HEADS + h:M_HEADS + h + 1, :] + ig_row, NEG_INF)
        m_inter = b_q + m_prevs[i]
        m_t = jnp.maximum(m_inter, jnp.max(d, axis=-1, keepdims=True))
        a_s.append(qk[i] * jnp.exp(d - m_t))
        m_ts.append(m_t)
        w_inters.append(jnp.exp(m_inter - m_t))
        b_reps.append(b_rep)
    av = [jnp.dot(a.astype(BF16), vb, preferred_element_type=F32) for a, vb in zip(a_s, vbs)]

    kws, w_states = [], []
    m_new_rows = [m_src[sq] for sq in range(nseq)]
    for i, (sq, h) in enumerate(ch):
        num = twice(w_inters[i]) * qc[i] + av[i]
        den = (w_inters[i] * jnp.sum(qs[i] * n_olds[i], axis=-1, keepdims=True)
               + jnp.sum(a_s[i], axis=-1, keepdims=True))
        hval = num / twice(jnp.maximum(jnp.abs(den), jnp.exp(-m_ts[i])))
        hs_ref[sq, :, h * M_V_DIM:(h + 1) * M_V_DIM] = hval
        m_new = m_ts[i][rows - 1:rows, :]
        b_last = b_reps[i][rows - 1:rows, :]
        w_states.append(jnp.exp(b_last + m_prevs[i] - m_new))
        ig_rep = jnp.broadcast_to(gsels[sq][:, h:h + 1], (L, LANES))
        kws.append(ks[i] * jnp.exp(b_last - b_reps[i] + ig_rep - m_new))
        m_new_rows[sq] = jnp.where(lane_1 == h, m_new, m_new_rows[sq])
    kv = [lax.dot_general(kw.astype(BF16), vb, _TN, preferred_element_type=F32) for kw, vb in zip(kws, vbs)]
    for i, (sq, h) in enumerate(ch):
        c_ref[sq, h] = twice(w_states[i]) * c_olds[i] + kv[i]
        n_ref[sq, h] = w_states[i] * n_olds[i] + jnp.sum(kws[i], axis=0, keepdims=True)
    for sq in range(nseq):
        m_ref[sq] = m_new_rows[sq]


MLSTM_SEQS_PER_STEP = {True: 4, False: 8}


def _mlstm(q, k, v, gates, bias, c0, n0, m0, valid):
    b, s, _ = q.shape
    assert MLSTM_CHUNK == LANES
    nseq = math.gcd(b, MLSTM_SEQS_PER_STEP[s >= MLSTM_CHUNK])
    if s >= MLSTM_CHUNK:
        rows, nc = MLSTM_CHUNK, s // MLSTM_CHUNK
        assert valid == s and s % MLSTM_CHUNK == 0
        scratch = []
    else:
        rows, nc = s, 1
        scratch = [pltpu.VMEM((nseq, MLSTM_CHUNK, w), F32) for w in (M_QK_W, M_QK_W, M_V_W, LANES)]
    tok = lambda w: pl.BlockSpec((nseq, rows, w), lambda bi, ci: (bi, ci, 0))
    st_c = pl.BlockSpec((nseq, M_HEADS, M_QK_DIM, M_V_DIM), lambda bi, ci: (bi, 0, 0, 0))
    st_n = pl.BlockSpec((nseq, M_HEADS, 1, M_QK_DIM), lambda bi, ci: (bi, 0, 0, 0))
    st_m = pl.BlockSpec((nseq, 1, LANES), lambda bi, ci: (bi, 0, 0))
    return pl.pallas_call(
        functools.partial(_mlstm_kernel, rows=rows, valid=valid, chunk=MLSTM_CHUNK, single_chunk=nc == 1),
        grid=(b // nseq, nc),
        in_specs=[tok(M_QK_W), tok(M_QK_W), tok(M_V_W), tok(LANES), _const_spec((1, LANES)), st_c, st_n, st_m],
        out_specs=[tok(M_V_W), st_c, st_n, st_m],
        out_shape=[jax.ShapeDtypeStruct((b, s, M_V_W), F32), jax.ShapeDtypeStruct(c0.shape, F32),
                   jax.ShapeDtypeStruct(n0.shape, F32), jax.ShapeDtypeStruct(m0.shape, F32)],
        scratch_shapes=scratch,
        compiler_params=_params("arbitrary" if scratch else "parallel", "arbitrary"),
        name="mlstm_chunks",
    )(q, k, v, gates, bias, c0, n0, m0)


FFN_CHUNKS = 2


def _ffn_residual(x, g_ref, wg_ref, wu_ref, wd_ref):
    hb = _rms(x, g_ref[...]).astype(BF16)
    acc = x
    cw = FFN_HIDDEN // FFN_CHUNKS
    for c in range(FFN_CHUNKS):
        sl = slice(c * cw, (c + 1) * cw)
        gg = jnp.dot(hb, wg_ref[:, sl], preferred_element_type=F32)
        uu = jnp.dot(hb, wu_ref[:, sl], preferred_element_type=F32)
        act = (jax.nn.silu(gg) * uu).astype(BF16)
        acc = acc + jnp.dot(act, wd_ref[sl, :], preferred_element_type=F32)
    return acc


def _mlstm_out_ffn_kernel(hs_ref, o_ref, x_ref, gh_ref, w_ref, g_ref, wg_ref, wu_ref, wd_ref, out_ref):
    parts = []
    for h in range(M_HEADS):
        sl = slice(h * M_V_DIM, (h + 1) * M_V_DIM)
        parts.append(_rms(hs_ref[:, sl], gh_ref[:, sl]))
    y = jnp.concatenate(parts, axis=-1) * jax.nn.sigmoid(o_ref[...])
    x1 = x_ref[...] + jnp.dot(y.astype(BF16), w_ref[...], preferred_element_type=F32)
    out_ref[...] = _ffn_residual(x1, g_ref, wg_ref, wu_ref, wd_ref)


def _proj_ffn_kernel(a_ref, x_ref, w_ref, g_ref, wg_ref, wu_ref, wd_ref, out_ref):
    x1 = x_ref[...] + jnp.dot(a_ref[...].astype(BF16), w_ref[...], preferred_element_type=F32)
    out_ref[...] = _ffn_residual(x1, g_ref, wg_ref, wu_ref, wd_ref)


def _mixer_out_ffn(acts, x, consts, tm, mlstm):
    n, d = x.shape
    row = lambda a: pl.BlockSpec((tm, a.shape[1]), lambda i: (i, 0))
    return pl.pallas_call(
        _mlstm_out_ffn_kernel if mlstm else _proj_ffn_kernel,
        grid=(n // tm,),
        in_specs=[row(a) for a in acts] + [row(x)] + [_const_spec(c.shape) for c in consts],
        out_specs=row(x),
        out_shape=jax.ShapeDtypeStruct((n, d), F32),
        compiler_params=_params("parallel"),
        name="mlstm_out_ffn" if mlstm else "attn_out_ffn",
    )(*acts, x, *consts)


def _qkv_kernel(x_ref, g_ref, wq_ref, wk_ref, wv_ref, gq_ref, gk_ref, cos_ref, sin_ref,
                q_ref, k_ref, v_ref, *blk_refs):
    hb = _rms(x_ref[...], g_ref[...]).astype(BF16)
    cos, sin = cos_ref[...], sin_ref[...]

    kb_ref, vt_ref, km_ref = blk_refs if blk_refs else (None, None, None)
    tm = x_ref.shape[0]
    rt = min(tm, MOBA_BLOCK)
    cw = 2 * A_HEAD_DIM

    def norm_rope(t, gh_ref, rows):
        y = _rms(t, gh_ref[...])
        return y * cos[rows, :] + pltpu.roll(y, A_HEAD_DIM // 2, 1) * sin[rows, :]

    for j in range(tm // rt):
        rows = slice(j * rt, (j + 1) * rt)
        hr = hb[rows, :]
        for c in range(A_HEADS // 2):
            t = jnp.dot(hr, wq_ref[:, c * cw:(c + 1) * cw], preferred_element_type=F32)
            for i in range(2):
                sl = slice((2 * c + i) * A_HEAD_DIM, (2 * c + i + 1) * A_HEAD_DIM)
                q_ref[rows, sl] = norm_rope(t[:, i * A_HEAD_DIM:(i + 1) * A_HEAD_DIM], gq_ref, rows)
        for c in range(A_HEADS // 2):
            t = jnp.dot(hr, wk_ref[:, c * cw:(c + 1) * cw], preferred_element_type=F32)
            for i in range(2):
                sl = slice((2 * c + i) * A_HEAD_DIM, (2 * c + i + 1) * A_HEAD_DIM)
                k = norm_rope(t[:, i * A_HEAD_DIM:(i + 1) * A_HEAD_DIM], gk_ref, rows)
                k_ref[rows, sl] = k
                if blk_refs:
                    kb_ref[rows, sl] = k.astype(BF16)
                    mean = jnp.mean(k, axis=0, keepdims=True)
                    km_ref[j, :, sl] = jnp.broadcast_to(mean, (SUBLANES, A_HEAD_DIM))
        for c in range(A_HEADS // 2):
            t = jnp.dot(hr, wv_ref[:, c * cw:(c + 1) * cw], preferred_element_type=F32)
            v_ref[rows, c * cw:(c + 1) * cw] = t
            if blk_refs:
                for i in range(2):
                    r0 = (2 * c + i) * VT_ROWS
                    vt_ref[j, r0:r0 + A_HEAD_DIM, :] = t[:, i * A_HEAD_DIM:(i + 1) * A_HEAD_DIM].T.astype(BF16)
                    vt_ref[j, r0 + A_HEAD_DIM:r0 + VT_ROWS, :] = jnp.ones((BF16_ROWS, MOBA_BLOCK), BF16)


def _qkv(x, g, wq, wk, wv, gq, gk, cos, sin, tm, with_blocks):
    n, d = x.shape
    e = wq.shape[1]
    n_tab = cos.shape[0] // tm
    row = lambda wd: pl.BlockSpec((tm, wd), lambda i: (i, 0))
    tab = pl.BlockSpec((tm, A_HEAD_DIM), lambda i: (i % n_tab, 0))
    out_specs = [row(e), row(e), row(e)]
    out_shape = [jax.ShapeDtypeStruct((n, e), F32)] * 3
    if with_blocks:
        assert tm % MOBA_BLOCK == 0
        gpt = tm // MOBA_BLOCK
        vrows = A_HEADS * VT_ROWS
        out_specs += [row(e), pl.BlockSpec((gpt, vrows, MOBA_BLOCK), lambda i: (i, 0, 0)),
                      pl.BlockSpec((gpt, SUBLANES, e), lambda i: (i, 0, 0))]
        out_shape += [jax.ShapeDtypeStruct((n, e), BF16),
                      jax.ShapeDtypeStruct((n // MOBA_BLOCK, vrows, MOBA_BLOCK), BF16),
                      jax.ShapeDtypeStruct((n // MOBA_BLOCK, SUBLANES, e), F32)]
    return pl.pallas_call(
        _qkv_kernel,
        grid=(n // tm,),
        in_specs=[row(d), _const_spec((1, d)), _const_spec(wq.shape), _const_spec(wk.shape),
                  _const_spec(wv.shape), _const_spec((1, A_HEAD_DIM)), _const_spec((1, A_HEAD_DIM)), tab, tab],
        out_specs=out_specs,
        out_shape=out_shape,
        compiler_params=_params("parallel"),
        name="qkv_rope",
    )(x, g, wq, wk, wv, gq, gk, cos, sin)


def _rope_tables(pos):
    half = A_HEAD_DIM // 2
    inv_freq = ROPE_THETA ** (-jnp.arange(half, dtype=F32) / half)
    ang = pos.astype(F32)[:, None] * inv_freq[None, :]
    c, s = jnp.cos(ang), jnp.sin(ang)
    return jnp.concatenate([c, c], axis=-1), jnp.concatenate([-s, s], axis=-1)


MOBA_HEADS_PER_STEP = 4
MASK_BIG = 2.0 ** 100


def _top_blocks_t(scores_t, allowed):
    row = lax.broadcasted_iota(jnp.int32, scores_t.shape, 0)
    s = jnp.where(allowed, scores_t, NEG_INF)
    picked = jnp.zeros(scores_t.shape, F32)
    for _ in range(MOBA_TOPK):
        mx = jnp.max(s, axis=0, keepdims=True)
        cand = jnp.where(jnp.logical_and(s == mx, s > NEG_INF), row, scores_t.shape[0])
        hit = row == jnp.min(cand, axis=0, keepdims=True)
        picked = jnp.where(hit, 1.0, picked)
        s = jnp.where(hit, NEG_INF, s)
    return picked


def _moba_prompt_kernel(q_ref, kb_ref, vt_ref, km_ref, o_ref, sa_ref, sb_ref, *, n_blocks, hp):
    own = pl.program_id(2)
    blk = MOBA_BLOCK
    scale = A_HEAD_DIM ** -0.5
    nbp = -(-n_blocks // BF16_ROWS) * BF16_ROWS
    lane = lax.broadcasted_iota(jnp.int32, (blk, LANES), 1)
    blk_i = lax.broadcasted_iota(jnp.int32, (nbp, blk), 0)
    heads = [slice(h * A_HEAD_DIM, (h + 1) * A_HEAD_DIM) for h in range(hp)]
    vrows = [slice(h * VT_ROWS, (h + 1) * VT_ROWS) for h in range(hp)]

    q_t = [q_ref[:, sl].T for sl in heads]
    q_tb = [(qt * scale).astype(BF16) for qt in q_t]
    own_start = pl.multiple_of(own * blk, blk)
    own_logits = [jnp.dot(kb_ref[pl.ds(own_start, blk), sl], qb, preferred_element_type=F32)
                  for sl, qb in zip(heads, q_tb)]
    kms = []
    for sl in heads:
        km = km_ref[:, sl]
        if nbp > n_blocks:
            km = jnp.concatenate([km, jnp.zeros((nbp - n_blocks, A_HEAD_DIM), F32)], axis=0)
        kms.append(km)
    scores = [jnp.dot(km, qt, precision=_HIGHEST, preferred_element_type=F32) for km, qt in zip(kms, q_t)]
    q_ext = []
    for qb, sc in zip(q_tb, scores):
        picked = _top_blocks_t(sc, blk_i < own)
        pad = jnp.zeros((LANES - nbp, blk), BF16)
        q_ext.append(jnp.concatenate([qb, (1.0 - picked).astype(BF16), pad], axis=0))

    def stage_logits(ref, j):
        st = pl.multiple_of(j * blk, blk)
        mask_cols = jnp.where(lane == j, -MASK_BIG, 0.0).astype(BF16)
        for h, (sl, qe) in enumerate(zip(heads, q_ext)):
            k_ext = jnp.concatenate([kb_ref[pl.ds(st, blk), sl], mask_cols], axis=1)
            ref[h] = jnp.dot(k_ext, qe, preferred_element_type=F32)

    stage_logits(sb_ref, 0)

    def accumulate(j, logits, state):
        probs, stats = [], []
        for s, (m, _) in zip(logits, state):
            m_new = jnp.maximum(m, jnp.max(s, axis=0, keepdims=True))
            probs.append(jnp.exp(s - m_new).astype(BF16))
            stats.append((m_new, jnp.exp(m - m_new)))
        return tuple(
            (m_new, alpha * acc + jnp.dot(vt_ref[j, vr, :], pb, preferred_element_type=F32))
            for vr, pb, (m_new, alpha), (_, acc) in zip(vrows, probs, stats, state))

    def accumulate_staged(j, ref, state):
        return accumulate(j, tuple(ref[h] for h in range(hp)), state)

    key_i = lax.broadcasted_iota(jnp.int32, (blk, blk), 0)
    qry_i = lax.broadcasted_iota(jnp.int32, (blk, blk), 1)
    state = []
    for s, vr in zip(own_logits, vrows):
        s = jnp.where(key_i <= qry_i, s, NEG_INF)
        m = jnp.max(s, axis=0, keepdims=True)
        state.append((m, jnp.dot(vt_ref[own, vr, :], jnp.exp(s - m).astype(BF16), preferred_element_type=F32)))
    state = tuple(state)

    def pair(i, state):
        j = 2 * i
        stage_logits(sa_ref, j + 1)
        state = accumulate_staged(j, sb_ref, state)
        stage_logits(sb_ref, j + 2)
        return accumulate_staged(j + 1, sa_ref, state)

    state = lax.fori_loop(0, lax.shift_right_logical(own, 1), pair, state)
    state = lax.fori_loop(0, jnp.bitwise_and(own, 1), lambda _, st: accumulate_staged(own - 1, sb_ref, st), state)
    for sl, (_, acc) in zip(heads, state):
        o_ref[:, sl] = (acc[0:A_HEAD_DIM, :] / acc[A_HEAD_DIM:A_HEAD_DIM + 1, :]).T


def _moba_prompt(q, kb, vt, kmean):
    b, s, e = q.shape
    nb = s // MOBA_BLOCK
    hp = MOBA_HEADS_PER_STEP
    assert s % MOBA_BLOCK == 0 and nb <= LANES and A_HEADS % hp == 0
    w = hp * A_HEAD_DIM
    tile = pl.BlockSpec((None, MOBA_BLOCK, w), lambda bi, h, qi: (bi, qi, h))
    return pl.pallas_call(
        functools.partial(_moba_prompt_kernel, n_blocks=nb, hp=hp),
        grid=(b, A_HEADS // hp, nb),
        in_specs=[tile, pl.BlockSpec((None, s, w), lambda bi, h, qi: (bi, 0, h)),
                  pl.BlockSpec((None, nb, hp * VT_ROWS, MOBA_BLOCK), lambda bi, h, qi: (bi, 0, h, 0)),
                  pl.BlockSpec((None, nb, w), lambda bi, h, qi: (bi, 0, h))],
        out_specs=tile,
        out_shape=jax.ShapeDtypeStruct((b, s, e), F32),
        scratch_shapes=[pltpu.VMEM((hp, MOBA_BLOCK, MOBA_BLOCK), F32)] * 2,
        compiler_params=_params("parallel", "parallel", "arbitrary"),
        name="moba_prompt",
    )(q, kb, vt, kmean)


def _moba_decode_kernel(pt_ref, q_ref, kn_ref, vn_ref, ck_ref, cv_ref, o_ref, kbuf, vbuf, sem, logit_ref,
                        *, n_pages, page_rows):
    bi = pl.program_id(0)
    n_seq = pl.num_programs(0)
    slot = lax.rem(bi, 2)
    R = A_HEADS * DEC_PAD
    scale = A_HEAD_DIM ** -0.5
    pages_per_block = MOBA_BLOCK // page_rows
    n_blocks = n_pages // pages_per_block

    def page_copy(cache_ref, buf_ref, which, seq, sl, g):
        page = pt_ref[seq * n_pages + g]
        return pltpu.make_async_copy(cache_ref.at[page], buf_ref.at[sl, g], sem.at[sl, which])

    def start_pages(seq, sl):
        for g in range(n_pages):
            page_copy(ck_ref, kbuf, 0, seq, sl, g).start()
        for g in range(n_pages):
            page_copy(cv_ref, vbuf, 1, seq, sl, g).start()

    @pl.when(bi == 0)
    def _():
        start_pages(bi, slot)

    @pl.when(bi + 1 < n_seq)
    def _():
        start_pages(bi + 1, 1 - slot)

    def rows_by_head(ref):
        return jnp.concatenate([ref[:, h * A_HEAD_DIM:(h + 1) * A_HEAD_DIM] for h in range(A_HEADS)], axis=0)

    def by_head(ref):
        return jnp.concatenate([ref[pl.ds(h, page_rows, stride=A_HEADS), :].astype(BF16) for h in range(A_HEADS)],
                               axis=0)

    q = rows_by_head(q_ref)
    qb = q.astype(BF16)

    for g in range(n_pages):
        page_copy(ck_ref, kbuf, 0, bi, slot, g).wait()
    ksum = []
    for g in range(n_pages):
        kpage = kbuf.at[slot, g]
        lg = lax.dot_general(qb, by_head(kpage), _NT, preferred_element_type=F32)
        for h in range(A_HEADS):
            hr = slice(h * DEC_PAD, (h + 1) * DEC_PAD)
            logit_ref[g, hr, :] = lg[hr, h * page_rows:(h + 1) * page_rows] * scale
        part = jnp.sum(kpage[...].reshape(page_rows, A_HEADS, A_HEAD_DIM), axis=0)
        if g % pages_per_block == 0:
            ksum.append(part)
        else:
            ksum[-1] = ksum[-1] + part

    kmean = jnp.concatenate(ksum, axis=0) * (1.0 / MOBA_BLOCK)
    nbh = n_blocks * A_HEADS
    scores_t = lax.dot_general(kmean, q, _NT, precision=_HIGHEST, preferred_element_type=F32)
    brow = lax.broadcasted_iota(jnp.int32, (nbh, R), 0)
    qcol = lax.broadcasted_iota(jnp.int32, (nbh, R), 1)
    picked_t = _top_blocks_t(scores_t, brow % A_HEADS == qcol // DEC_PAD)
    erow = lax.broadcasted_iota(jnp.int32, (nbh, n_blocks * page_rows), 0)
    ecol = lax.broadcasted_iota(jnp.int32, (nbh, n_blocks * page_rows), 1)
    expand = jnp.where(erow // A_HEADS == ecol // page_rows, 1.0, 0.0).astype(BF16)
    keep = lax.dot_general(picked_t.astype(BF16), expand, _TN, preferred_element_type=F32)
    keep_blk = [keep[:, j * page_rows:(j + 1) * page_rows] > 0.5 for j in range(n_blocks)]

    kn = rows_by_head(kn_ref).astype(BF16)
    ln = lax.dot_general(qb, kn, _NT, preferred_element_type=F32) * scale
    rn = lax.broadcasted_iota(jnp.int32, (R, R), 0)
    cn = lax.broadcasted_iota(jnp.int32, (R, R), 1)
    ok_n = jnp.logical_and(rn // DEC_PAD == cn // DEC_PAD, cn % DEC_PAD <= rn % DEC_PAD)
    ln = jnp.where(ok_n, ln, NEG_INF)

    masked = [jnp.where(keep_blk[g // pages_per_block], logit_ref[g], NEG_INF) for g in range(n_pages)]
    mm = masked[0]
    for g in range(1, n_pages):
        mm = jnp.maximum(mm, masked[g])
    m = jnp.maximum(jnp.max(ln, axis=-1, keepdims=True), jnp.max(mm, axis=-1, keepdims=True))
    pn = jnp.exp(ln - m)
    probs = [jnp.exp(s - m) for s in masked]
    lsum = probs[0]
    for g in range(1, n_pages):
        lsum = lsum + probs[g]
    l = jnp.sum(pn, axis=-1, keepdims=True) + jnp.sum(lsum, axis=-1, keepdims=True)

    for g in range(n_pages):
        page_copy(cv_ref, vbuf, 1, bi, slot, g).wait()
    acc = jnp.dot(pn.astype(BF16), rows_by_head(vn_ref).astype(BF16), preferred_element_type=F32)
    zero = jnp.zeros((DEC_PAD, page_rows), F32)
    for g in range(n_pages):
        pe = probs[g]
        p_bd = jnp.concatenate(
            [jnp.concatenate([pe[h * DEC_PAD:(h + 1) * DEC_PAD, :] if hh == h else zero for hh in range(A_HEADS)],
                             axis=1) for h in range(A_HEADS)], axis=0)
        acc = acc + jnp.dot(p_bd.astype(BF16), by_head(vbuf.at[slot, g]), preferred_element_type=F32)
    res = acc / l
    for h in range(A_HEADS):
        o_ref[:, h * A_HEAD_DIM:(h + 1) * A_HEAD_DIM] = res[h * DEC_PAD:(h + 1) * DEC_PAD, :]


def _moba_decode(q, kn, vn, cache_k, cache_v, page_table):
    b, t, e = q.shape
    n_phys, page_rows = cache_k.shape[0], cache_k.shape[1]
    n_pages = page_table.shape[1]
    assert t == DEC_PAD and MOBA_BLOCK % page_rows == 0 and n_pages % (MOBA_BLOCK // page_rows) == 0
    cols = page_rows * A_HEADS
    ck = cache_k.reshape(n_phys, cols, A_HEAD_DIM)
    cv = cache_v.reshape(n_phys, cols, A_HEAD_DIM)
    pt = page_table.reshape(-1).astype(jnp.int32)

    tok = pl.BlockSpec((None, t, e), lambda bi, pt_: (bi, 0, 0))
    hbm = pl.BlockSpec(memory_space=pl.ANY)
    grid_spec = pltpu.PrefetchScalarGridSpec(
        num_scalar_prefetch=1,
        grid=(b,),
        in_specs=[tok, tok, tok, hbm, hbm],
        out_specs=tok,
        scratch_shapes=[pltpu.VMEM((2, n_pages, cols, A_HEAD_DIM), F32),
                        pltpu.VMEM((2, n_pages, cols, A_HEAD_DIM), F32),
                        pltpu.SemaphoreType.DMA((2, 2)),
                        pltpu.VMEM((n_pages, A_HEADS * DEC_PAD, page_rows), F32)],
    )
    return pl.pallas_call(
        functools.partial(_moba_decode_kernel, n_pages=n_pages, page_rows=page_rows),
        grid_spec=grid_spec,
        out_shape=jax.ShapeDtypeStruct((b, t, e), F32),
        compiler_params=_params("arbitrary"),
        name="moba_decode",
    )(pt, q, kn, vn, ck, cv)


def _trunk(x, valid, pos0, c0, n0, m0, past, w, tm):
    b, s, d = x.shape
    n = b * s
    tm = min(tm, n)
    assert n % tm == 0 and (tm % s == 0 or s % tm == 0)
    xf = x.reshape(n, d)

    q, k, v, o, gates = _norm_proj(xf, w["norm_mix"][0], [w["wq_m"], w["wk_m"], w["wv_m"], w["wo_m"], w["wgate_m"]], tm)
    r3 = lambda t: t.reshape(b, s, t.shape[-1])
    hs, c_new, n_new, m_new = _mlstm(r3(q), r3(k), r3(v), r3(gates), w["b_gate"], c0, n0, m0, valid)
    xf = _mixer_out_ffn((hs.reshape(n, M_V_W), o), xf, (w["norm_h_m"], w["w_out_m"], w["norm_ffn"][0], w["wg"][0],
                                                         w["wu"][0], w["wd"][0]), tm, mlstm=True)

    pos = pos0 + jnp.arange(s, dtype=jnp.int32)
    cos, sin = _rope_tables(pos)
    if s < tm:
        cos, sin = jnp.tile(cos, (tm // s, 1)), jnp.tile(sin, (tm // s, 1))
    qa, ka, va, *blocks = _qkv(xf, w["norm_mix"][1], w["wq_a"], w["wk_a"], w["wv_a"], w["norm_q_a"],
                               w["norm_k_a"], cos, sin, tm, with_blocks=past is None)
    if past is None:
        kb, vt, kmean = blocks
        nb = s // MOBA_BLOCK
        att = _moba_prompt(r3(qa), r3(kb), vt.reshape(b, nb, -1, MOBA_BLOCK), kmean[:, 0, :].reshape(b, nb, -1))
    else:
        att = _moba_decode(r3(qa), r3(ka), r3(va), *past)
    xf = _mixer_out_ffn((att.reshape(n, -1),), xf, (w["w_out_a"], w["norm_ffn"][1], w["wg"][1], w["wu"][1],
                                                    w["wd"][1]), tm, mlstm=False)

    heads = lambda t: t.reshape(b, s, A_HEADS, A_HEAD_DIM)[:, :valid]
    return xf.reshape(b, s, d)[:, :valid], heads(ka), heads(va), c_new, n_new, m_new


def kernel(x_prompt, x_sample, state_C, state_n, state_m, cache_k, cache_v, page_table, norm_mix, norm_ffn,
           w_in_m, b_gate_m, norm_h_m, w_out_m, w_qkv_a, norm_q_a, norm_k_a, w_out_a, w_gu, w_down):
    assert norm_mix.shape[0] == 2 and w_in_m.shape[0] == 1 and w_qkv_a.shape[0] == 1
    bf = lambda t: t.astype(BF16)
    row = lambda t: t.reshape(1, -1).astype(F32)
    w_in = w_in_m[0]
    e_a = A_HEADS * A_HEAD_DIM
    gate_w = jnp.pad(w_in[:, 2 * M_QK_W + 2 * M_V_W:], ((0, 0), (0, LANES - 2 * M_HEADS)))
    w = {
        "norm_mix": [row(norm_mix[0]), row(norm_mix[1])],
        "norm_ffn": [row(norm_ffn[0]), row(norm_ffn[1])],
        "wq_m": bf(w_in[:, :M_QK_W]), "wk_m": bf(w_in[:, M_QK_W:2 * M_QK_W]),
        "wv_m": bf(w_in[:, 2 * M_QK_W:2 * M_QK_W + M_V_W]),
        "wo_m": bf(w_in[:, 2 * M_QK_W + M_V_W:2 * M_QK_W + 2 * M_V_W]),
        "wgate_m": bf(gate_w),
        "b_gate": jnp.pad(b_gate_m[0].astype(F32), (0, LANES - 2 * M_HEADS)).reshape(1, LANES),
        "norm_h_m": row(norm_h_m[0]), "w_out_m": bf(w_out_m[0]),
        "wq_a": bf(w_qkv_a[0][:, :e_a]), "wk_a": bf(w_qkv_a[0][:, e_a:2 * e_a]), "wv_a": bf(w_qkv_a[0][:, 2 * e_a:]),
        "norm_q_a": row(norm_q_a[0]), "norm_k_a": row(norm_k_a[0]), "w_out_a": bf(w_out_a[0]),
        "wg": [bf(w_gu[i][:, :FFN_HIDDEN]) for i in range(2)],
        "wu": [bf(w_gu[i][:, FFN_HIDDEN:]) for i in range(2)],
        "wd": [bf(w_down[i]) for i in range(2)],
    }

    def state(c, nn, mm):
        bsz = c.shape[0]
        m_row = jnp.pad(mm.astype(F32), ((0, 0), (0, LANES - M_HEADS))).reshape(bsz, 1, LANES)
        return c.astype(F32), nn.astype(F32).reshape(bsz, M_HEADS, 1, M_QK_DIM), m_row

    def unstate(c, nn, mm, dt):
        bsz = c.shape[0]
        return (c[None].astype(dt), nn.reshape(1, bsz, M_HEADS, M_QK_DIM).astype(dt),
                mm.reshape(bsz, LANES)[None, :, :M_HEADS].astype(dt))

    sd = state_C.dtype
    bp, sp, _ = x_prompt.shape
    zc = jnp.zeros((bp,) + state_C.shape[2:], F32)
    zn = jnp.zeros((bp,) + state_n.shape[2:], F32)
    zm = jnp.zeros((bp,) + state_m.shape[2:], F32)
    yp, pk, pv, pc, pn, pm = _trunk(x_prompt, sp, 0, *state(zc, zn, zm), None, w, 512)

    bs, ss, _ = x_sample.shape
    xs = jnp.pad(x_sample, ((0, 0), (0, DEC_PAD - ss), (0, 0)))
    past_len = page_table.shape[1] * cache_k.shape[2]
    ys, sk, sv, sc, sn, sm = _trunk(xs, ss, past_len, *state(state_C[0], state_n[0], state_m[0]),
                                    (cache_k[0], cache_v[0], page_table), w, 512)

    return (yp, ys, pk[None], pv[None], *unstate(pc, pn, pm, sd), sk[None], sv[None], *unstate(sc, sn, sm, sd))
```

```python
import functools
import math

import jax
import jax.numpy as jnp
from jax import lax
from jax.experimental import pallas as pl
from jax.experimental.pallas import tpu as pltpu

F32 = jnp.float32
BF16 = jnp.bfloat16

D_MODEL = 1024
M_HEADS = 4
M_QK_DIM = 128
M_V_DIM = 256
M_QK_W = M_HEADS * M_QK_DIM
M_V_W = M_HEADS * M_V_DIM
A_HEADS = 8
A_HEAD_DIM = 128
MOBA_BLOCK = 256
MOBA_TOPK = 3
ROPE_THETA = 10000.0
FFN_HIDDEN = 2816
EPS = 1e-6

LANES = 128
SUBLANES = 8
BF16_ROWS = 16
VT_ROWS = A_HEAD_DIM + BF16_ROWS
VMEM_LIMIT_BYTES = 56 * 1024 * 1024

MLSTM_CHUNK = 128
DEC_PAD = SUBLANES
NEG_INF = float("-inf")

_HIGHEST = lax.Precision.HIGHEST
_NT = (((1,), (1,)), ((), ()))
_TN = (((0,), (0,)), ((), ()))


def _params(*sem):
    return pltpu.CompilerParams(dimension_semantics=sem, vmem_limit_bytes=VMEM_LIMIT_BYTES)


def _const_spec(shape):
    nd = len(shape)
    return pl.BlockSpec(shape, lambda *_: (0,) * nd, pipeline_mode=pl.Buffered(1))


def _rms(x, g):
    return x * lax.rsqrt(jnp.mean(x * x, axis=-1, keepdims=True) + EPS) * g


def _norm_proj_kernel(x_ref, g_ref, *refs, n_out):
    w_refs, o_refs = refs[:n_out], refs[n_out:]
    hb = _rms(x_ref[...], g_ref[...]).astype(BF16)
    for w_ref, o_ref in zip(w_refs, o_refs):
        o_ref[...] = jnp.dot(hb, w_ref[...], preferred_element_type=F32)


def _norm_proj(x, g, ws, tm):
    n, d = x.shape
    return pl.pallas_call(
        functools.partial(_norm_proj_kernel, n_out=len(ws)),
        grid=(n // tm,),
        in_specs=[pl.BlockSpec((tm, d), lambda i: (i, 0)), _const_spec((1, d))]
        + [_const_spec(w.shape) for w in ws],
        out_specs=[pl.BlockSpec((tm, w.shape[1]), lambda i: (i, 0)) for w in ws],
        out_shape=[jax.ShapeDtypeStruct((n, w.shape[1]), F32) for w in ws],
        compiler_params=_params("parallel"),
        name="norm_proj",
    )(x, g, *ws)


def _mlstm_kernel(q_ref, k_ref, v_ref, gt_ref, bias_ref, c0_ref, n0_ref, m0_ref,
                  hs_ref, c_ref, n_ref, m_ref, *pad_refs, rows, valid, chunk, single_chunk):
    nseq = q_ref.shape[0]
    ci = pl.program_id(1)
    if single_chunk:
        c_src, n_src, m_src = c0_ref, n0_ref, m0_ref
    else:
        c_src, n_src, m_src = c_ref, n_ref, m_ref

        @pl.when(ci == 0)
        def _():
            c_ref[...] = c0_ref[...]
            n_ref[...] = n0_ref[...]
            m_ref[...] = m0_ref[...]

    if rows != chunk:
        @pl.when(jnp.logical_and(pl.program_id(0) == 0, ci == 0))
        def _():
            for p in pad_refs:
                p[...] = jnp.zeros(p.shape, p.dtype)

    L = chunk
    row_l = lax.broadcasted_iota(jnp.int32, (L, LANES), 0)
    lane_l = lax.broadcasted_iota(jnp.int32, (L, LANES), 1)
    is_ig = lane_l < M_HEADS
    real = row_l < valid
    eye8 = (lax.broadcasted_iota(jnp.int32, (SUBLANES, LANES), 0)
            == lax.broadcasted_iota(jnp.int32, (SUBLANES, LANES), 1)).astype(F32)
    col_real = lax.broadcasted_iota(jnp.int32, (SUBLANES, L), 1) < valid
    causal = lax.broadcasted_iota(jnp.int32, (L, L), 1) <= lax.broadcasted_iota(jnp.int32, (L, L), 0)
    lane_1 = lax.broadcasted_iota(jnp.int32, (1, LANES), 1)
    scale = M_QK_DIM ** -0.5

    tiles, gsels, b_colss, cols_l = [], [], [], []
    for sq in range(nseq):
        if rows == chunk:
            tiles.append((q_ref[sq], k_ref[sq], v_ref[sq]))
            gt_all = gt_ref[sq]
        else:
            qp, kp, vp, gp = pad_refs
            qp[sq, 0:rows, :], kp[sq, 0:rows, :], vp[sq, 0:rows, :], gp[sq, 0:rows, :] = (
                q_ref[sq], k_ref[sq], v_ref[sq], gt_ref[sq])
            tiles.append((qp[sq], kp[sq], vp[sq]))
            gt_all = gp[sq]
        gc = gt_all + bias_ref[...]
        lfc = jnp.where(real, jax.nn.log_sigmoid(gc), 0.0)
        igc = jnp.where(real, gc, NEG_INF)
        gsels.append(jnp.where(is_ig, igc, lfc))
        b_cols = lfc
        shift = 1
        while shift < L:
            b_cols = b_cols + jnp.where(row_l >= shift, pltpu.roll(b_cols, shift, 0), 0.0)
            shift *= 2
        b_colss.append(b_cols)
        cols_l.append(jnp.where(is_ig, jnp.where(real, gc, 0.0), b_cols))
    rows_ts = [lax.dot_general(eye8, c, _NT, precision=_HIGHEST, preferred_element_type=F32) for c in cols_l]

    ch = [(sq, h) for sq in range(nseq) for h in range(M_HEADS)]
    qs = [tiles[sq][0][0:rows, h * M_QK_DIM:(h + 1) * M_QK_DIM] * scale for sq, h in ch]
    ks = [tiles[sq][1][:, h * M_QK_DIM:(h + 1) * M_QK_DIM] for sq, h in ch]
    qbs = [q.astype(BF16) for q in qs]
    kbs = [k.astype(BF16) for k in ks]
    vbs = [tiles[sq][2][:, h * M_V_DIM:(h + 1) * M_V_DIM].astype(BF16) for sq, h in ch]
    c_olds = [c_src[sq, h] for sq, h in ch]
    n_olds = [n_src[sq, h] for sq, h in ch]
    b_cs = [b_colss[sq][:, M_HEADS + h:M_HEADS + h + 1] for sq, h in ch]
    m_prevs = [m_src[sq][:, h:h + 1] for sq, h in ch]

    qk = [lax.dot_general(qb, kb, _NT, preferred_element_type=F32) for qb, kb in zip(qbs, kbs)]
    qc = [jnp.dot(qb, c.astype(BF16), preferred_element_type=F32) for qb, c in zip(qbs, c_olds)]

    twice = lambda t: jnp.concatenate([t, t], axis=1)
    a_s, m_ts, w_inters, b_reps = [], [], [], []
    for i, (sq, h) in enumerate(ch):
        b_rep = jnp.broadcast_to(b_cs[i], (L, LANES))
        b_q = b_rep[0:rows, :]
        ig_row = jnp.where(col_real, rows_ts[sq], NEG_INF)[h:h + 1, :]
        d = jnp.where(causal[0:rows, :], b_q - rows_ts[sq][M_HEADS + h:M_HEADS + h + 1, :] + ig_row, NEG_INF)
        m_inter = b_q + m_prevs[i]
        m_t = jnp.maximum(m_inter, jnp.max(d, axis=-1, keepdims=True))
        a_s.append(qk[i] * jnp.exp(d - m_t))
        m_ts.append(m_t)
        w_inters.append(jnp.exp(m_inter - m_t))
        b_reps.append(b_rep)
    av = [jnp.dot(a.astype(BF16), vb, preferred_element_type=F32) for a, vb in zip(a_s, vbs)]

    kws, w_states = [], []
    m_new_rows = [m_src[sq] for sq in range(nseq)]
    for i, (sq, h) in enumerate(ch):
        num = twice(w_inters[i]) * qc[i] + av[i]
        den = (w_inters[i] * jnp.sum(qs[i] * n_olds[i], axis=-1, keepdims=True)
               + jnp.sum(a_s[i], axis=-1, keepdims=True))
        hval = num / twice(jnp.maximum(jnp.abs(den), jnp.exp(-m_ts[i])))
        hs_ref[sq, :, h * M_V_DIM:(h + 1) * M_V_DIM] = hval
        m_new = m_ts[i][rows - 1:rows, :]
        b_last = b_reps[i][rows - 1:rows, :]
        w_states.append(jnp.exp(b_last + m_prevs[i] - m_new))
        ig_rep = jnp.broadcast_to(gsels[sq][:, h:h + 1], (L, LANES))
        kws.append(ks[i] * jnp.exp(b_last - b_reps[i] + ig_rep - m_new))
        m_new_rows[sq] = jnp.where(lane_1 == h, m_new, m_new_rows[sq])
    kv = [lax.dot_general(kw.astype(BF16), vb, _TN, preferred_element_type=F32) for kw, vb in zip(kws, vbs)]
    for i, (sq, h) in enumerate(ch):
        c_ref[sq, h] = twice(w_states[i]) * c_olds[i] + kv[i]
        n_ref[sq, h] = w_states[i] * n_olds[i] + jnp.sum(kws[i], axis=0, keepdims=True)
    for sq in range(nseq):
        m_ref[sq] = m_new_rows[sq]


MLSTM_SEQS_PER_STEP = {True: 4, False: 8}


def _mlstm(q, k, v, gates, bias, c0, n0, m0, valid):
    b, s, _ = q.shape
    assert MLSTM_CHUNK == LANES
    nseq = math.gcd(b, MLSTM_SEQS_PER_STEP[s >= MLSTM_CHUNK])
    if s >= MLSTM_CHUNK:
        rows, nc = MLSTM_CHUNK, s // MLSTM_CHUNK
        assert valid == s and s % MLSTM_CHUNK == 0
        scratch = []
    else:
        rows, nc = s, 1
        scratch = [pltpu.VMEM((nseq, MLSTM_CHUNK, w), F32) for w in (M_QK_W, M_QK_W, M_V_W, LANES)]
    tok = lambda w: pl.BlockSpec((nseq, rows, w), lambda bi, ci: (bi, ci, 0))
    st_c = pl.BlockSpec((nseq, M_HEADS, M_QK_DIM, M_V_DIM), lambda bi, ci: (bi, 0, 0, 0))
    st_n = pl.BlockSpec((nseq, M_HEADS, 1, M_QK_DIM), lambda bi, ci: (bi, 0, 0, 0))
    st_m = pl.BlockSpec((nseq, 1, LANES), lambda bi, ci: (bi, 0, 0))
    return pl.pallas_call(
        functools.partial(_mlstm_kernel, rows=rows, valid=valid, chunk=MLSTM_CHUNK, single_chunk=nc == 1),
        grid=(b // nseq, nc),
        in_specs=[tok(M_QK_W), tok(M_QK_W), tok(M_V_W), tok(LANES), _const_spec((1, LANES)), st_c, st_n, st_m],
        out_specs=[tok(M_V_W), st_c, st_n, st_m],
        out_shape=[jax.ShapeDtypeStruct((b, s, M_V_W), F32), jax.ShapeDtypeStruct(c0.shape, F32),
                   jax.ShapeDtypeStruct(n0.shape, F32), jax.ShapeDtypeStruct(m0.shape, F32)],
        scratch_shapes=scratch,
        compiler_params=_params("arbitrary" if scratch else "parallel", "arbitrary"),
        name="mlstm_chunks",
    )(q, k, v, gates, bias, c0, n0, m0)


FFN_CHUNKS = 11


def _ffn_residual(x, g_ref, wg_ref, wu_ref, wd_ref):
    hb = _rms(x, g_ref[...]).astype(BF16)
    acc = x
    cw = FFN_HIDDEN // FFN_CHUNKS
    for c in range(FFN_CHUNKS):
        sl = slice(c * cw, (c + 1) * cw)
        gg = jnp.dot(hb, wg_ref[:, sl], preferred_element_type=F32)
        uu = jnp.dot(hb, wu_ref[:, sl], preferred_element_type=F32)
        act = (jax.nn.silu(gg) * uu).astype(BF16)
        acc = acc + jnp.dot(act, wd_ref[sl, :], preferred_element_type=F32)
    return acc


def _mlstm_out_ffn_kernel(hs_ref, o_ref, x_ref, gh_ref, w_ref, g_ref, wg_ref, wu_ref, wd_ref, out_ref):
    parts = []
    for h in range(M_HEADS):
        sl = slice(h * M_V_DIM, (h + 1) * M_V_DIM)
        parts.append(_rms(hs_ref[:, sl], gh_ref[:, sl]))
    y = jnp.concatenate(parts, axis=-1) * jax.nn.sigmoid(o_ref[...])
    x1 = x_ref[...] + jnp.dot(y.astype(BF16), w_ref[...], preferred_element_type=F32)
    out_ref[...] = _ffn_residual(x1, g_ref, wg_ref, wu_ref, wd_ref)


def _proj_ffn_kernel(a_ref, x_ref, w_ref, g_ref, wg_ref, wu_ref, wd_ref, out_ref):
    x1 = x_ref[...] + jnp.dot(a_ref[...].astype(BF16), w_ref[...], preferred_element_type=F32)
    out_ref[...] = _ffn_residual(x1, g_ref, wg_ref, wu_ref, wd_ref)


def _mixer_out_ffn(acts, x, consts, tm, mlstm):
    n, d = x.shape
    row = lambda a: pl.BlockSpec((tm, a.shape[1]), lambda i: (i, 0))
    return pl.pallas_call(
        _mlstm_out_ffn_kernel if mlstm else _proj_ffn_kernel,
        grid=(n // tm,),
        in_specs=[row(a) for a in acts] + [row(x)] + [_const_spec(c.shape) for c in consts],
        out_specs=row(x),
        out_shape=jax.ShapeDtypeStruct((n, d), F32),
        compiler_params=_params("parallel"),
        name="mlstm_out_ffn" if mlstm else "attn_out_ffn",
    )(*acts, x, *consts)


def _qkv_kernel(x_ref, g_ref, wq_ref, wk_ref, wv_ref, gq_ref, gk_ref, cos_ref, sin_ref,
                q_ref, k_ref, v_ref, *blk_refs):
    hb = _rms(x_ref[...], g_ref[...]).astype(BF16)
    cos, sin = cos_ref[...], sin_ref[...]

    kb_ref, vt_ref, km_ref = blk_refs if blk_refs else (None, None, None)
    tm = x_ref.shape[0]
    rt = min(tm, MOBA_BLOCK)
    cw = 2 * A_HEAD_DIM

    def norm_rope(t, gh_ref, rows):
        y = _rms(t, gh_ref[...])
        return y * cos[rows, :] + pltpu.roll(y, A_HEAD_DIM // 2, 1) * sin[rows, :]

    for j in range(tm // rt):
        rows = slice(j * rt, (j + 1) * rt)
        hr = hb[rows, :]
        for c in range(A_HEADS // 2):
            t = jnp.dot(hr, wq_ref[:, c * cw:(c + 1) * cw], preferred_element_type=F32)
            for i in range(2):
                sl = slice((2 * c + i) * A_HEAD_DIM, (2 * c + i + 1) * A_HEAD_DIM)
                q_ref[rows, sl] = norm_rope(t[:, i * A_HEAD_DIM:(i + 1) * A_HEAD_DIM], gq_ref, rows)
        for c in range(A_HEADS // 2):
            t = jnp.dot(hr, wk_ref[:, c * cw:(c + 1) * cw], preferred_element_type=F32)
            for i in range(2):
                sl = slice((2 * c + i) * A_HEAD_DIM, (2 * c + i + 1) * A_HEAD_DIM)
                k = norm_rope(t[:, i * A_HEAD_DIM:(i + 1) * A_HEAD_DIM], gk_ref, rows)
                k_ref[rows, sl] = k
                if blk_refs:
                    kb_ref[rows, sl] = k.astype(BF16)
                    mean = jnp.mean(k, axis=0, keepdims=True)
                    km_ref[j, :, sl] = jnp.broadcast_to(mean, (SUBLANES, A_HEAD_DIM))
        for c in range(A_HEADS // 2):
            t = jnp.dot(hr, wv_ref[:, c * cw:(c + 1) * cw], preferred_element_type=F32)
            v_ref[rows, c * cw:(c + 1) * cw] = t
            if blk_refs:
                for i in range(2):
                    r0 = (2 * c + i) * VT_ROWS
                    vt_ref[j, r0:r0 + A_HEAD_DIM, :] = t[:, i * A_HEAD_DIM:(i + 1) * A_HEAD_DIM].T.astype(BF16)
                    vt_ref[j, r0 + A_HEAD_DIM:r0 + VT_ROWS, :] = jnp.ones((BF16_ROWS, MOBA_BLOCK), BF16)


def _qkv(x, g, wq, wk, wv, gq, gk, cos, sin, tm, with_blocks):
    n, d = x.shape
    e = wq.shape[1]
    n_tab = cos.shape[0] // tm
    row = lambda wd: pl.BlockSpec((tm, wd), lambda i: (i, 0))
    tab = pl.BlockSpec((tm, A_HEAD_DIM), lambda i: (i % n_tab, 0))
    out_specs = [row(e), row(e), row(e)]
    out_shape = [jax.ShapeDtypeStruct((n, e), F32)] * 3
    if with_blocks:
        assert tm % MOBA_BLOCK == 0
        gpt = tm // MOBA_BLOCK
        vrows = A_HEADS * VT_ROWS
        out_specs += [row(e), pl.BlockSpec((gpt, vrows, MOBA_BLOCK), lambda i: (i, 0, 0)),
                      pl.BlockSpec((gpt, SUBLANES, e), lambda i: (i, 0, 0))]
        out_shape += [jax.ShapeDtypeStruct((n, e), BF16),
                      jax.ShapeDtypeStruct((n // MOBA_BLOCK, vrows, MOBA_BLOCK), BF16),
                      jax.ShapeDtypeStruct((n // MOBA_BLOCK, SUBLANES, e), F32)]
    return pl.pallas_call(
        _qkv_kernel,
        grid=(n // tm,),
        in_specs=[row(d), _const_spec((1, d)), _const_spec(wq.shape), _const_spec(wk.shape),
                  _const_spec(wv.shape), _const_spec((1, A_HEAD_DIM)), _const_spec((1, A_HEAD_DIM)), tab, tab],
        out_specs=out_specs,
        out_shape=out_shape,
        compiler_params=_params("parallel"),
        name="qkv_rope",
    )(x, g, wq, wk, wv, gq, gk, cos, sin)


def _rope_tables(pos):
    half = A_HEAD_DIM // 2
    inv_freq = ROPE_THETA ** (-jnp.arange(half, dtype=F32) / half)
    ang = pos.astype(F32)[:, None] * inv_freq[None, :]
    c, s = jnp.cos(ang), jnp.sin(ang)
    return jnp.concatenate([c, c], axis=-1), jnp.concatenate([-s, s], axis=-1)


MOBA_HEADS_PER_STEP = 4
MASK_BIG = 2.0 ** 100


def _top_blocks_t(scores_t, allowed):
    row = lax.broadcasted_iota(jnp.int32, scores_t.shape, 0)
    s = jnp.where(allowed, scores_t, NEG_INF)
    picked = jnp.zeros(scores_t.shape, F32)
    for _ in range(MOBA_TOPK):
        mx = jnp.max(s, axis=0, keepdims=True)
        cand = jnp.where(jnp.logical_and(s == mx, s > NEG_INF), row, scores_t.shape[0])
        hit = row == jnp.min(cand, axis=0, keepdims=True)
        picked = jnp.where(hit, 1.0, picked)
        s = jnp.where(hit, NEG_INF, s)
    return picked


def _moba_prompt_kernel(q_ref, kb_ref, vt_ref, km_ref, o_ref, sa_ref, sb_ref, *, n_blocks, hp):
    own = pl.program_id(2)
    blk = MOBA_BLOCK
    scale = A_HEAD_DIM ** -0.5
    nbp = -(-n_blocks // BF16_ROWS) * BF16_ROWS
    lane = lax.broadcasted_iota(jnp.int32, (blk, LANES), 1)
    blk_i = lax.broadcasted_iota(jnp.int32, (nbp, blk), 0)
    heads = [slice(h * A_HEAD_DIM, (h + 1) * A_HEAD_DIM) for h in range(hp)]
    vrows = [slice(h * VT_ROWS, (h + 1) * VT_ROWS) for h in range(hp)]

    q_t = [q_ref[:, sl].T for sl in heads]
    q_tb = [(qt * scale).astype(BF16) for qt in q_t]
    own_start = pl.multiple_of(own * blk, blk)
    own_logits = [jnp.dot(kb_ref[pl.ds(own_start, blk), sl], qb, preferred_element_type=F32)
                  for sl, qb in zip(heads, q_tb)]
    kms = []
    for sl in heads:
        km = km_ref[:, sl]
        if nbp > n_blocks:
            km = jnp.concatenate([km, jnp.zeros((nbp - n_blocks, A_HEAD_DIM), F32)], axis=0)
        kms.append(km)
    scores = [jnp.dot(km, qt, precision=_HIGHEST, preferred_element_type=F32) for km, qt in zip(kms, q_t)]
    q_ext = []
    for qb, sc in zip(q_tb, scores):
        picked = _top_blocks_t(sc, blk_i < own)
        pad = jnp.zeros((LANES - nbp, blk), BF16)
        q_ext.append(jnp.concatenate([qb, (1.0 - picked).astype(BF16), pad], axis=0))

    def stage_logits(ref, j):
        st = pl.multiple_of(j * blk, blk)
        mask_cols = jnp.where(lane == j, -MASK_BIG, 0.0).astype(BF16)
        for h, (sl, qe) in enumerate(zip(heads, q_ext)):
            k_ext = jnp.concatenate([kb_ref[pl.ds(st, blk), sl], mask_cols], axis=1)
            ref[h] = jnp.dot(k_ext, qe, preferred_element_type=F32)

    stage_logits(sb_ref, 0)

    def accumulate(j, logits, state):
        probs, stats = [], []
        for s, (m, _) in zip(logits, state):
            m_new = jnp.maximum(m, jnp.max(s, axis=0, keepdims=True))
            probs.append(jnp.exp(s - m_new).astype(BF16))
            stats.append((m_new, jnp.exp(m - m_new)))
        return tuple(
            (m_new, alpha * acc + jnp.dot(vt_ref[j, vr, :], pb, preferred_element_type=F32))
            for vr, pb, (m_new, alpha), (_, acc) in zip(vrows, probs, stats, state))

    def accumulate_staged(j, ref, state):
        return accumulate(j, tuple(ref[h] for h in range(hp)), state)

    key_i = lax.broadcasted_iota(jnp.int32, (blk, blk), 0)
    qry_i = lax.broadcasted_iota(jnp.int32, (blk, blk), 1)
    state = []
    for s, vr in zip(own_logits, vrows):
        s = jnp.where(key_i <= qry_i, s, NEG_INF)
        m = jnp.max(s, axis=0, keepdims=True)
        state.append((m, jnp.dot(vt_ref[own, vr, :], jnp.exp(s - m).astype(BF16), preferred_element_type=F32)))
    state = tuple(state)

    def pair(i, state):
        j = 2 * i
        stage_logits(sa_ref, j + 1)
        state = accumulate_staged(j, sb_ref, state)
        stage_logits(sb_ref, j + 2)
        return accumulate_staged(j + 1, sa_ref, state)

    state = lax.fori_loop(0, lax.shift_right_logical(own, 1), pair, state)
    state = lax.fori_loop(0, jnp.bitwise_and(own, 1), lambda _, st: accumulate_staged(own - 1, sb_ref, st), state)
    for sl, (_, acc) in zip(heads, state):
        o_ref[:, sl] = (acc[0:A_HEAD_DIM, :] / acc[A_HEAD_DIM:A_HEAD_DIM + 1, :]).T


def _moba_prompt(q, kb, vt, kmean):
    b, s, e = q.shape
    nb = s // MOBA_BLOCK
    hp = MOBA_HEADS_PER_STEP
    assert s % MOBA_BLOCK == 0 and nb <= LANES and A_HEADS % hp == 0
    w = hp * A_HEAD_DIM
    tile = pl.BlockSpec((None, MOBA_BLOCK, w), lambda bi, h, qi: (bi, qi, h))
    return pl.pallas_call(
        functools.partial(_moba_prompt_kernel, n_blocks=nb, hp=hp),
        grid=(b, A_HEADS // hp, nb),
        in_specs=[tile, pl.BlockSpec((None, s, w), lambda bi, h, qi: (bi, 0, h)),
                  pl.BlockSpec((None, nb, hp * VT_ROWS, MOBA_BLOCK), lambda bi, h, qi: (bi, 0, h, 0)),
                  pl.BlockSpec((None, nb, w), lambda bi, h, qi: (bi, 0, h))],
        out_specs=tile,
        out_shape=jax.ShapeDtypeStruct((b, s, e), F32),
        scratch_shapes=[pltpu.VMEM((hp, MOBA_BLOCK, MOBA_BLOCK), F32)] * 2,
        compiler_params=_params("parallel", "parallel", "arbitrary"),
        name="moba_prompt",
    )(q, kb, vt, kmean)


def _moba_decode_kernel(pt_ref, q_ref, kn_ref, vn_ref, ck_ref, cv_ref, o_ref, kbuf, vbuf, sem, logit_ref,
                        *, n_pages, page_rows):
    bi = pl.program_id(0)
    n_seq = pl.num_programs(0)
    slot = lax.rem(bi, 2)
    R = A_HEADS * DEC_PAD
    scale = A_HEAD_DIM ** -0.5
    pages_per_block = MOBA_BLOCK // page_rows
    n_blocks = n_pages // pages_per_block

    def page_copy(cache_ref, buf_ref, which, seq, sl, g):
        page = pt_ref[seq * n_pages + g]
        return pltpu.make_async_copy(cache_ref.at[page], buf_ref.at[sl, g], sem.at[sl, which])

    def start_pages(seq, sl):
        for g in range(n_pages):
            page_copy(ck_ref, kbuf, 0, seq, sl, g).start()
        for g in range(n_pages):
            page_copy(cv_ref, vbuf, 1, seq, sl, g).start()

    @pl.when(bi == 0)
    def _():
        start_pages(bi, slot)

    @pl.when(bi + 1 < n_seq)
    def _():
        start_pages(bi + 1, 1 - slot)

    def rows_by_head(ref):
        return jnp.concatenate([ref[:, h * A_HEAD_DIM:(h + 1) * A_HEAD_DIM] for h in range(A_HEADS)], axis=0)

    def by_head(ref):
        return jnp.concatenate([ref[pl.ds(h, page_rows, stride=A_HEADS), :].astype(BF16) for h in range(A_HEADS)],
                               axis=0)

    q = rows_by_head(q_ref)
    qb = q.astype(BF16)

    for g in range(n_pages):
        page_copy(ck_ref, kbuf, 0, bi, slot, g).wait()
    ksum = []
    for g in range(n_pages):
        kpage = kbuf.at[slot, g]
        lg = lax.dot_general(qb, by_head(kpage), _NT, preferred_element_type=F32)
        for h in range(A_HEADS):
            hr = slice(h * DEC_PAD, (h + 1) * DEC_PAD)
            logit_ref[g, hr, :] = lg[hr, h * page_rows:(h + 1) * page_rows] * scale
        part = jnp.sum(kpage[...].reshape(page_rows, A_HEADS, A_HEAD_DIM), axis=0)
        if g % pages_per_block == 0:
            ksum.append(part)
        else:
            ksum[-1] = ksum[-1] + part

    kmean = jnp.concatenate(ksum, axis=0) * (1.0 / MOBA_BLOCK)
    nbh = n_blocks * A_HEADS
    scores_t = lax.dot_general(kmean, q, _NT, precision=_HIGHEST, preferred_element_type=F32)
    brow = lax.broadcasted_iota(jnp.int32, (nbh, R), 0)
    qcol = lax.broadcasted_iota(jnp.int32, (nbh, R), 1)
    picked_t = _top_blocks_t(scores_t, brow % A_HEADS == qcol // DEC_PAD)
    erow = lax.broadcasted_iota(jnp.int32, (nbh, n_blocks * page_rows), 0)
    ecol = lax.broadcasted_iota(jnp.int32, (nbh, n_blocks * page_rows), 1)
    expand = jnp.where(erow // A_HEADS == ecol // page_rows, 1.0, 0.0).astype(BF16)
    keep = lax.dot_general(picked_t.astype(BF16), expand, _TN, preferred_element_type=F32)
    keep_blk = [keep[:, j * page_rows:(j + 1) * page_rows] > 0.5 for j in range(n_blocks)]

    kn = rows_by_head(kn_ref).astype(BF16)
    ln = lax.dot_general(qb, kn, _NT, preferred_element_type=F32) * scale
    rn = lax.broadcasted_iota(jnp.int32, (R, R), 0)
    cn = lax.broadcasted_iota(jnp.int32, (R, R), 1)
    ok_n = jnp.logical_and(rn // DEC_PAD == cn // DEC_PAD, cn % DEC_PAD <= rn % DEC_PAD)
    ln = jnp.where(ok_n, ln, NEG_INF)

    masked = [jnp.where(keep_blk[g // pages_per_block], logit_ref[g], NEG_INF) for g in range(n_pages)]
    mm = masked[0]
    for g in range(1, n_pages):
        mm = jnp.maximum(mm, masked[g])
    m = jnp.maximum(jnp.max(ln, axis=-1, keepdims=True), jnp.max(mm, axis=-1, keepdims=True))
    pn = jnp.exp(ln - m)
    probs = [jnp.exp(s - m) for s in masked]
    lsum = probs[0]
    for g in range(1, n_pages):
        lsum = lsum + probs[g]
    l = jnp.sum(pn, axis=-1, keepdims=True) + jnp.sum(lsum, axis=-1, keepdims=True)

    for g in range(n_pages):
        page_copy(cv_ref, vbuf, 1, bi, slot, g).wait()
    acc = jnp.dot(pn.astype(BF16), rows_by_head(vn_ref).astype(BF16), preferred_element_type=F32)
    zero = jnp.zeros((DEC_PAD, page_rows), F32)
    for g in range(n_pages):
        pe = probs[g]
        p_bd = jnp.concatenate(
            [jnp.concatenate([pe[h * DEC_PAD:(h + 1) * DEC_PAD, :] if hh == h else zero for hh in range(A_HEADS)],
                             axis=1) for h in range(A_HEADS)], axis=0)
        acc = acc + jnp.dot(p_bd.astype(BF16), by_head(vbuf.at[slot, g]), preferred_element_type=F32)
    res = acc / l
    for h in range(A_HEADS):
        o_ref[:, h * A_HEAD_DIM:(h + 1) * A_HEAD_DIM] = res[h * DEC_PAD:(h + 1) * DEC_PAD, :]


def _moba_decode(q, kn, vn, cache_k, cache_v, page_table):
    b, t, e = q.shape
    n_phys, page_rows = cache_k.shape[0], cache_k.shape[1]
    n_pages = page_table.shape[1]
    assert t == DEC_PAD and MOBA_BLOCK % page_rows == 0 and n_pages % (MOBA_BLOCK // page_rows) == 0
    cols = page_rows * A_HEADS
    ck = cache_k.reshape(n_phys, cols, A_HEAD_DIM)
    cv = cache_v.reshape(n_phys, cols, A_HEAD_DIM)
    pt = page_table.reshape(-1).astype(jnp.int32)

    tok = pl.BlockSpec((None, t, e), lambda bi, pt_: (bi, 0, 0))
    hbm = pl.BlockSpec(memory_space=pl.ANY)
    grid_spec = pltpu.PrefetchScalarGridSpec(
        num_scalar_prefetch=1,
        grid=(b,),
        in_specs=[tok, tok, tok, hbm, hbm],
        out_specs=tok,
        scratch_shapes=[pltpu.VMEM((2, n_pages, cols, A_HEAD_DIM), F32),
                        pltpu.VMEM((2, n_pages, cols, A_HEAD_DIM), F32),
                        pltpu.SemaphoreType.DMA((2, 2)),
                        pltpu.VMEM((n_pages, A_HEADS * DEC_PAD, page_rows), F32)],
    )
    return pl.pallas_call(
        functools.partial(_moba_decode_kernel, n_pages=n_pages, page_rows=page_rows),
        grid_spec=grid_spec,
        out_shape=jax.ShapeDtypeStruct((b, t, e), F32),
        compiler_params=_params("arbitrary"),
        name="moba_decode",
    )(pt, q, kn, vn, ck, cv)


ROW_TILE = 512


def _trunk(x, pos0, c0, n0, m0, past, w):
    b, s, d = x.shape
    n = b * s
    sp = -(-s // DEC_PAD) * DEC_PAD
    tm = min(ROW_TILE, n // 2)
    assert n % tm == 0 and (tm % s == 0 or s % tm == 0)
    xf = x.reshape(n, d)
    r3 = lambda t: jnp.pad(t.reshape(b, s, t.shape[-1]), ((0, 0), (0, sp - s), (0, 0)))
    flat = lambda t: t[:, :s].reshape(n, t.shape[-1])

    q, k, v, o, gates = _norm_proj(xf, w["norm_mix"][0], [w["wq_m"], w["wk_m"], w["wv_m"], w["wo_m"], w["wgate_m"]], tm)
    hs, c_new, n_new, m_new = _mlstm(r3(q), r3(k), r3(v), r3(gates), w["b_gate"], c0, n0, m0, s)
    xf = _mixer_out_ffn((flat(hs), o), xf, (w["norm_h_m"], w["w_out_m"], w["norm_ffn"][0], w["wg"][0],
                                            w["wu"][0], w["wd"][0]), tm, mlstm=True)

    pos = pos0 + jnp.arange(s, dtype=jnp.int32)
    cos, sin = _rope_tables(pos)
    if s < tm:
        cos, sin = jnp.tile(cos, (tm // s, 1)), jnp.tile(sin, (tm // s, 1))
    qa, ka, va, *blocks = _qkv(xf, w["norm_mix"][1], w["wq_a"], w["wk_a"], w["wv_a"], w["norm_q_a"],
                               w["norm_k_a"], cos, sin, tm, with_blocks=past is None)
    if past is None:
        kb, vt, kmean = blocks
        nb = s // MOBA_BLOCK
        att = _moba_prompt(r3(qa), r3(kb), vt.reshape(b, nb, -1, MOBA_BLOCK), kmean[:, 0, :].reshape(b, nb, -1))
    else:
        att = _moba_decode(r3(qa), r3(ka), r3(va), *past)
    xf = _mixer_out_ffn((flat(att),), xf, (w["w_out_a"], w["norm_ffn"][1], w["wg"][1], w["wu"][1], w["wd"][1]),
                        tm, mlstm=False)

    heads = lambda t: t.reshape(b, s, A_HEADS, A_HEAD_DIM)
    return xf.reshape(b, s, d), heads(ka), heads(va), c_new, n_new, m_new


def kernel(x_prompt, x_sample, state_C, state_n, state_m, cache_k, cache_v, page_table, norm_mix, norm_ffn,
           w_in_m, b_gate_m, norm_h_m, w_out_m, w_qkv_a, norm_q_a, norm_k_a, w_out_a, w_gu, w_down):
    assert norm_mix.shape[0] == 2 and w_in_m.shape[0] == 1 and w_qkv_a.shape[0] == 1
    bf = lambda t: t.astype(BF16)
    row = lambda t: t.reshape(1, -1).astype(F32)
    w_in = w_in_m[0]
    e_a = A_HEADS * A_HEAD_DIM
    gate_w = jnp.pad(w_in[:, 2 * M_QK_W + 2 * M_V_W:], ((0, 0), (0, LANES - 2 * M_HEADS)))
    w = {
        "norm_mix": [row(norm_mix[0]), row(norm_mix[1])],
        "norm_ffn": [row(norm_ffn[0]), row(norm_ffn[1])],
        "wq_m": bf(w_in[:, :M_QK_W]), "wk_m": bf(w_in[:, M_QK_W:2 * M_QK_W]),
        "wv_m": bf(w_in[:, 2 * M_QK_W:2 * M_QK_W + M_V_W]),
        "wo_m": bf(w_in[:, 2 * M_QK_W + M_V_W:2 * M_QK_W + 2 * M_V_W]),
        "wgate_m": bf(gate_w),
        "b_gate": jnp.pad(b_gate_m[0].astype(F32), (0, LANES - 2 * M_HEADS)).reshape(1, LANES),
        "norm_h_m": row(norm_h_m[0]), "w_out_m": bf(w_out_m[0]),
        "wq_a": bf(w_qkv_a[0][:, :e_a]), "wk_a": bf(w_qkv_a[0][:, e_a:2 * e_a]), "wv_a": bf(w_qkv_a[0][:, 2 * e_a:]),
        "norm_q_a": row(norm_q_a[0]), "norm_k_a": row(norm_k_a[0]), "w_out_a": bf(w_out_a[0]),
        "wg": [bf(w_gu[i][:, :FFN_HIDDEN]) for i in range(2)],
        "wu": [bf(w_gu[i][:, FFN_HIDDEN:]) for i in range(2)],
        "wd": [bf(w_down[i]) for i in range(2)],
    }

    def state(c, nn, mm):
        bsz = c.shape[0]
        m_row = jnp.pad(mm.astype(F32), ((0, 0), (0, LANES - M_HEADS))).reshape(bsz, 1, LANES)
        return c.astype(F32), nn.astype(F32).reshape(bsz, M_HEADS, 1, M_QK_DIM), m_row

    def unstate(c, nn, mm, dt):
        bsz = c.shape[0]
        return (c[None].astype(dt), nn.reshape(1, bsz, M_HEADS, M_QK_DIM).astype(dt),
                mm.reshape(bsz, LANES)[None, :, :M_HEADS].astype(dt))

    sd = state_C.dtype
    bp, sp, _ = x_prompt.shape
    zc = jnp.zeros((bp,) + state_C.shape[2:], F32)
    zn = jnp.zeros((bp,) + state_n.shape[2:], F32)
    zm = jnp.zeros((bp,) + state_m.shape[2:], F32)
    yp, pk, pv, pc, pn, pm = _trunk(x_prompt, 0, *state(zc, zn, zm), None, w)

    past_len = page_table.shape[1] * cache_k.shape[2]
    ys, sk, sv, sc, sn, sm = _trunk(x_sample, past_len, *state(state_C[0], state_n[0], state_m[0]),
                                    (cache_k[0], cache_v[0], page_table), w)

    return (yp, ys, pk[None], pv[None], *unstate(pc, pn, pm, sd), sk[None], sv[None], *unstate(sc, sn, sm, sd))
```

```python
import functools
import math

import jax
import jax.numpy as jnp
from jax import lax
from jax.experimental import pallas as pl
from jax.experimental.pallas import tpu as pltpu

F32 = jnp.float32
BF16 = jnp.bfloat16

D_MODEL = 1024
M_HEADS = 4
M_QK_DIM = 128
M_V_DIM = 256
M_QK_W = M_HEADS * M_QK_DIM
M_V_W = M_HEADS * M_V_DIM
A_HEADS = 8
A_HEAD_DIM = 128
MOBA_BLOCK = 256
MOBA_TOPK = 3
ROPE_THETA = 10000.0
FFN_HIDDEN = 2816
EPS = 1e-6

LANES = 128
SUBLANES = 8
BF16_ROWS = 16
VT_ROWS = A_HEAD_DIM + BF16_ROWS
VMEM_LIMIT_BYTES = 56 * 1024 * 1024

MLSTM_CHUNK = 128
DEC_PAD = SUBLANES
NEG_INF = float("-inf")

_HIGHEST = lax.Precision.HIGHEST
_NT = (((1,), (1,)), ((), ()))
_TN = (((0,), (0,)), ((), ()))


def _params(*sem):
    return pltpu.CompilerParams(dimension_semantics=sem, vmem_limit_bytes=VMEM_LIMIT_BYTES)


def _const_spec(shape):
    nd = len(shape)
    return pl.BlockSpec(shape, lambda *_: (0,) * nd, pipeline_mode=pl.Buffered(1))


def _rms(x, g):
    return x * lax.rsqrt(jnp.mean(x * x, axis=-1, keepdims=True) + EPS) * g


def _norm_proj_kernel(x_ref, g_ref, *refs, n_out):
    w_refs, o_refs = refs[:n_out], refs[n_out:]
    hb = _rms(x_ref[...], g_ref[...]).astype(BF16)
    for w_ref, o_ref in zip(w_refs, o_refs):
        o_ref[...] = jnp.dot(hb, w_ref[...], preferred_element_type=F32)


def _norm_proj(x, g, ws, tm):
    n, d = x.shape
    return pl.pallas_call(
        functools.partial(_norm_proj_kernel, n_out=len(ws)),
        grid=(n // tm,),
        in_specs=[pl.BlockSpec((tm, d), lambda i: (i, 0)), _const_spec((1, d))]
        + [_const_spec(w.shape) for w in ws],
        out_specs=[pl.BlockSpec((tm, w.shape[1]), lambda i: (i, 0)) for w in ws],
        out_shape=[jax.ShapeDtypeStruct((n, w.shape[1]), F32) for w in ws],
        compiler_params=_params("parallel"),
        name="norm_proj",
    )(x, g, *ws)


def _mlstm_kernel(q_ref, k_ref, v_ref, gt_ref, bias_ref, c0_ref, n0_ref, m0_ref,
                  hs_ref, c_ref, n_ref, m_ref, *pad_refs, rows, valid, chunk, single_chunk):
    nseq = q_ref.shape[0]
    ci = pl.program_id(1)
    if single_chunk:
        c_src, n_src, m_src = c0_ref, n0_ref, m0_ref
    else:
        c_src, n_src, m_src = c_ref, n_ref, m_ref

        @pl.when(ci == 0)
        def _():
            c_ref[...] = c0_ref[...]
            n_ref[...] = n0_ref[...]
            m_ref[...] = m0_ref[...]

    if rows != chunk:
        @pl.when(jnp.logical_and(pl.program_id(0) == 0, ci == 0))
        def _():
            for p in pad_refs:
                p[...] = jnp.zeros(p.shape, p.dtype)

    L = chunk
    row_l = lax.broadcasted_iota(jnp.int32, (L, LANES), 0)
    lane_l = lax.broadcasted_iota(jnp.int32, (L, LANES), 1)
    is_ig = lane_l < M_HEADS
    real = row_l < valid
    eye8 = (lax.broadcasted_iota(jnp.int32, (SUBLANES, LANES), 0)
            == lax.broadcasted_iota(jnp.int32, (SUBLANES, LANES), 1)).astype(F32)
    col_real = lax.broadcasted_iota(jnp.int32, (SUBLANES, L), 1) < valid
    causal = lax.broadcasted_iota(jnp.int32, (L, L), 1) <= lax.broadcasted_iota(jnp.int32, (L, L), 0)
    lane_1 = lax.broadcasted_iota(jnp.int32, (1, LANES), 1)
    scale = M_QK_DIM ** -0.5

    tiles, gsels, b_colss, cols_l = [], [], [], []
    for sq in range(nseq):
        if rows == chunk:
            tiles.append((q_ref[sq], k_ref[sq], v_ref[sq]))
            gt_all = gt_ref[sq]
        else:
            qp, kp, vp, gp = pad_refs
            qp[sq, 0:rows, :], kp[sq, 0:rows, :], vp[sq, 0:rows, :], gp[sq, 0:rows, :] = (
                q_ref[sq], k_ref[sq], v_ref[sq], gt_ref[sq])
            tiles.append((qp[sq], kp[sq], vp[sq]))
            gt_all = gp[sq]
        gc = gt_all + bias_ref[...]
        lfc = jnp.where(real, jax.nn.log_sigmoid(gc), 0.0)
        igc = jnp.where(real, gc, NEG_INF)
        gsels.append(jnp.where(is_ig, igc, lfc))
        b_cols = lfc
        shift = 1
        while shift < L:
            b_cols = b_cols + jnp.where(row_l >= shift, pltpu.roll(b_cols, shift, 0), 0.0)
            shift *= 2
        b_colss.append(b_cols)
        cols_l.append(jnp.where(is_ig, jnp.where(real, gc, 0.0), b_cols))
    rows_ts = [lax.dot_general(eye8, c, _NT, precision=_HIGHEST, preferred_element_type=F32) for c in cols_l]

    ch = [(sq, h) for sq in range(nseq) for h in range(M_HEADS)]
    qs = [tiles[sq][0][0:rows, h * M_QK_DIM:(h + 1) * M_QK_DIM] * scale for sq, h in ch]
    ks = [tiles[sq][1][:, h * M_QK_DIM:(h + 1) * M_QK_DIM] for sq, h in ch]
    qbs = [q.astype(BF16) for q in qs]
    kbs = [k.astype(BF16) for k in ks]
    vbs = [tiles[sq][2][:, h * M_V_DIM:(h + 1) * M_V_DIM].astype(BF16) for sq, h in ch]
    c_olds = [c_src[sq, h] for sq, h in ch]
    n_olds = [n_src[sq, h] for sq, h in ch]
    b_cs = [b_colss[sq][:, M_HEADS + h:M_HEADS + h + 1] for sq, h in ch]
    m_prevs = [m_src[sq][:, h:h + 1] for sq, h in ch]

    qk = [lax.dot_general(qb, kb, _NT, preferred_element_type=F32) for qb, kb in zip(qbs, kbs)]
    qc = [jnp.dot(qb, c.astype(BF16), preferred_element_type=F32) for qb, c in zip(qbs, c_olds)]

    twice = lambda t: jnp.concatenate([t, t], axis=1)
    a_s, m_ts, w_inters, b_reps = [], [], [], []
    for i, (sq, h) in enumerate(ch):
        b_rep = jnp.broadcast_to(b_cs[i], (L, LANES))
        b_q = b_rep[0:rows, :]
        ig_row = jnp.where(col_real, rows_ts[sq], NEG_INF)[h:h + 1, :]
        d = jnp.where(causal[0:rows, :], b_q - rows_ts[sq][M_HEADS + h:M_HEADS + h + 1, :] + ig_row, NEG_INF)
        m_inter = b_q + m_prevs[i]
        m_t = jnp.maximum(m_inter, jnp.max(d, axis=-1, keepdims=True))
        a_s.append(qk[i] * jnp.exp(d - m_t))
        m_ts.append(m_t)
        w_inters.append(jnp.exp(m_inter - m_t))
        b_reps.append(b_rep)
    av = [jnp.dot(a.astype(BF16), vb, preferred_element_type=F32) for a, vb in zip(a_s, vbs)]

    kws, w_states = [], []
    m_new_rows = [m_src[sq] for sq in range(nseq)]
    for i, (sq, h) in enumerate(ch):
        num = twice(w_inters[i]) * qc[i] + av[i]
        den = (w_inters[i] * jnp.sum(qs[i] * n_olds[i], axis=-1, keepdims=True)
               + jnp.sum(a_s[i], axis=-1, keepdims=True))
        hval = num / twice(jnp.maximum(jnp.abs(den), jnp.exp(-m_ts[i])))
        hs_ref[sq, :, h * M_V_DIM:(h + 1) * M_V_DIM] = hval
        m_new = m_ts[i][rows - 1:rows, :]
        b_last = b_reps[i][rows - 1:rows, :]
        w_states.append(jnp.exp(b_last + m_prevs[i] - m_new))
        ig_rep = jnp.broadcast_to(gsels[sq][:, h:h + 1], (L, LANES))
        kws.append(ks[i] * jnp.exp(b_last - b_reps[i] + ig_rep - m_new))
        m_new_rows[sq] = jnp.where(lane_1 == h, m_new, m_new_rows[sq])
    kv = [lax.dot_general(kw.astype(BF16), vb, _TN, preferred_element_type=F32) for kw, vb in zip(kws, vbs)]
    for i, (sq, h) in enumerate(ch):
        c_ref[sq, h] = twice(w_states[i]) * c_olds[i] + kv[i]
        n_ref[sq, h] = w_states[i] * n_olds[i] + jnp.sum(kws[i], axis=0, keepdims=True)
    for sq in range(nseq):
        m_ref[sq] = m_new_rows[sq]


MLSTM_SEQS_PER_STEP = {True: 4, False: 8}


def _mlstm(q, k, v, gates, bias, c0, n0, m0, valid):
    b, s, _ = q.shape
    assert MLSTM_CHUNK == LANES
    nseq = math.gcd(b, MLSTM_SEQS_PER_STEP[s >= MLSTM_CHUNK])
    if s >= MLSTM_CHUNK:
        rows, nc = MLSTM_CHUNK, s // MLSTM_CHUNK
        assert valid == s and s % MLSTM_CHUNK == 0
        scratch = []
    else:
        rows, nc = s, 1
        scratch = [pltpu.VMEM((nseq, MLSTM_CHUNK, w), F32) for w in (M_QK_W, M_QK_W, M_V_W, LANES)]
    tok = lambda w: pl.BlockSpec((nseq, rows, w), lambda bi, ci: (bi, ci, 0))
    st_c = pl.BlockSpec((nseq, M_HEADS, M_QK_DIM, M_V_DIM), lambda bi, ci: (bi, 0, 0, 0))
    st_n = pl.BlockSpec((nseq, M_HEADS, 1, M_QK_DIM), lambda bi, ci: (bi, 0, 0, 0))
    st_m = pl.BlockSpec((nseq, 1, LANES), lambda bi, ci: (bi, 0, 0))
    return pl.pallas_call(
        functools.partial(_mlstm_kernel, rows=rows, valid=valid, chunk=MLSTM_CHUNK, single_chunk=nc == 1),
        grid=(b // nseq, nc),
        in_specs=[tok(M_QK_W), tok(M_QK_W), tok(M_V_W), tok(LANES), _const_spec((1, LANES)), st_c, st_n, st_m],
        out_specs=[tok(M_V_W), st_c, st_n, st_m],
        out_shape=[jax.ShapeDtypeStruct((b, s, M_V_W), F32), jax.ShapeDtypeStruct(c0.shape, F32),
                   jax.ShapeDtypeStruct(n0.shape, F32), jax.ShapeDtypeStruct(m0.shape, F32)],
        scratch_shapes=scratch,
        compiler_params=_params("arbitrary" if scratch else "parallel", "arbitrary"),
        name="mlstm_chunks",
    )(q, k, v, gates, bias, c0, n0, m0)


FFN_CHUNKS = 11


def _ffn_residual(x, g_ref, wg_ref, wu_ref, wd_ref):
    hb = _rms(x, g_ref[...]).astype(BF16)
    acc = x
    cw = FFN_HIDDEN // FFN_CHUNKS
    for c in range(FFN_CHUNKS):
        sl = slice(c * cw, (c + 1) * cw)
        gg = jnp.dot(hb, wg_ref[:, sl], preferred_element_type=F32)
        uu = jnp.dot(hb, wu_ref[:, sl], preferred_element_type=F32)
        act = (jax.nn.silu(gg) * uu).astype(BF16)
        acc = acc + jnp.dot(act, wd_ref[sl, :], preferred_element_type=F32)
    return acc


def _mlstm_out_ffn_kernel(hs_ref, o_ref, x_ref, gh_ref, w_ref, g_ref, wg_ref, wu_ref, wd_ref, out_ref):
    parts = []
    for h in range(M_HEADS):
        sl = slice(h * M_V_DIM, (h + 1) * M_V_DIM)
        parts.append(_rms(hs_ref[:, sl], gh_ref[:, sl]))
    y = jnp.concatenate(parts, axis=-1) * jax.nn.sigmoid(o_ref[...])
    x1 = x_ref[...] + jnp.dot(y.astype(BF16), w_ref[...], preferred_element_type=F32)
    out_ref[...] = _ffn_residual(x1, g_ref, wg_ref, wu_ref, wd_ref)


def _proj_ffn_kernel(a_ref, x_ref, w_ref, g_ref, wg_ref, wu_ref, wd_ref, out_ref):
    x1 = x_ref[...] + jnp.dot(a_ref[...].astype(BF16), w_ref[...], preferred_element_type=F32)
    out_ref[...] = _ffn_residual(x1, g_ref, wg_ref, wu_ref, wd_ref)


def _mixer_out_ffn(acts, x, consts, tm, mlstm):
    n, d = x.shape
    row = lambda a: pl.BlockSpec((tm, a.shape[1]), lambda i: (i, 0))
    return pl.pallas_call(
        _mlstm_out_ffn_kernel if mlstm else _proj_ffn_kernel,
        grid=(n // tm,),
        in_specs=[row(a) for a in acts] + [row(x)] + [_const_spec(c.shape) for c in consts],
        out_specs=row(x),
        out_shape=jax.ShapeDtypeStruct((n, d), F32),
        compiler_params=_params("parallel"),
        name="mlstm_out_ffn" if mlstm else "attn_out_ffn",
    )(*acts, x, *consts)


def _qkv_kernel(x_ref, g_ref, wq_ref, wk_ref, wv_ref, gq_ref, gk_ref, cos_ref, sin_ref,
                q_ref, k_ref, v_ref, *blk_refs):
    hb = _rms(x_ref[...], g_ref[...]).astype(BF16)
    cos, sin = cos_ref[...], sin_ref[...]

    kb_ref, vt_ref, km_ref = blk_refs if blk_refs else (None, None, None)
    tm = x_ref.shape[0]
    rt = min(tm, MOBA_BLOCK)
    cw = 2 * A_HEAD_DIM

    def norm_rope(t, gh_ref, rows):
        y = _rms(t, gh_ref[...])
        return y * cos[rows, :] + pltpu.roll(y, A_HEAD_DIM // 2, 1) * sin[rows, :]

    for j in range(tm // rt):
        rows = slice(j * rt, (j + 1) * rt)
        hr = hb[rows, :]
        for c in range(A_HEADS // 2):
            t = jnp.dot(hr, wq_ref[:, c * cw:(c + 1) * cw], preferred_element_type=F32)
            for i in range(2):
                sl = slice((2 * c + i) * A_HEAD_DIM, (2 * c + i + 1) * A_HEAD_DIM)
                q_ref[rows, sl] = norm_rope(t[:, i * A_HEAD_DIM:(i + 1) * A_HEAD_DIM], gq_ref, rows)
        for c in range(A_HEADS // 2):
            t = jnp.dot(hr, wk_ref[:, c * cw:(c + 1) * cw], preferred_element_type=F32)
            for i in range(2):
                sl = slice((2 * c + i) * A_HEAD_DIM, (2 * c + i + 1) * A_HEAD_DIM)
                k = norm_rope(t[:, i * A_HEAD_DIM:(i + 1) * A_HEAD_DIM], gk_ref, rows)
                k_ref[rows, sl] = k
                if blk_refs:
                    kb_ref[rows, sl] = k.astype(BF16)
                    mean = jnp.mean(k, axis=0, keepdims=True)
                    km_ref[j, :, sl] = jnp.broadcast_to(mean, (SUBLANES, A_HEAD_DIM))
        for c in range(A_HEADS // 2):
            t = jnp.dot(hr, wv_ref[:, c * cw:(c + 1) * cw], preferred_element_type=F32)
            v_ref[rows, c * cw:(c + 1) * cw] = t
            if blk_refs:
                for i in range(2):
                    r0 = (2 * c + i) * VT_ROWS
                    vt_ref[j, r0:r0 + A_HEAD_DIM, :] = t[:, i * A_HEAD_DIM:(i + 1) * A_HEAD_DIM].T.astype(BF16)
                    vt_ref[j, r0 + A_HEAD_DIM:r0 + VT_ROWS, :] = jnp.ones((BF16_ROWS, MOBA_BLOCK), BF16)


def _qkv(x, g, wq, wk, wv, gq, gk, cos, sin, tm, with_blocks):
    n, d = x.shape
    e = wq.shape[1]
    n_tab = cos.shape[0] // tm
    row = lambda wd: pl.BlockSpec((tm, wd), lambda i: (i, 0))
    tab = pl.BlockSpec((tm, A_HEAD_DIM), lambda i: (i % n_tab, 0))
    out_specs = [row(e), row(e), row(e)]
    out_shape = [jax.ShapeDtypeStruct((n, e), F32)] * 3
    if with_blocks:
        assert tm % MOBA_BLOCK == 0
        gpt = tm // MOBA_BLOCK
        vrows = A_HEADS * VT_ROWS
        out_specs += [row(e), pl.BlockSpec((gpt, vrows, MOBA_BLOCK), lambda i: (i, 0, 0)),
                      pl.BlockSpec((gpt, SUBLANES, e), lambda i: (i, 0, 0))]
        out_shape += [jax.ShapeDtypeStruct((n, e), BF16),
                      jax.ShapeDtypeStruct((n // MOBA_BLOCK, vrows, MOBA_BLOCK), BF16),
                      jax.ShapeDtypeStruct((n // MOBA_BLOCK, SUBLANES, e), F32)]
    return pl.pallas_call(
        _qkv_kernel,
        grid=(n // tm,),
        in_specs=[row(d), _const_spec((1, d)), _const_spec(wq.shape), _const_spec(wk.shape),
                  _const_spec(wv.shape), _const_spec((1, A_HEAD_DIM)), _const_spec((1, A_HEAD_DIM)), tab, tab],
        out_specs=out_specs,
        out_shape=out_shape,
        compiler_params=_params("parallel"),
        name="qkv_rope",
    )(x, g, wq, wk, wv, gq, gk, cos, sin)


def _rope_tables(pos):
    half = A_HEAD_DIM // 2
    inv_freq = ROPE_THETA ** (-jnp.arange(half, dtype=F32) / half)
    ang = pos.astype(F32)[:, None] * inv_freq[None, :]
    c, s = jnp.cos(ang), jnp.sin(ang)
    return jnp.concatenate([c, c], axis=-1), jnp.concatenate([-s, s], axis=-1)


MOBA_QUERY_BLOCKS = 2
MOBA_HEADS_PER_STEP = 4
MASK_BIG = 2.0 ** 100


def _top_blocks_t(scores_t, allowed):
    row = lax.broadcasted_iota(jnp.int32, scores_t.shape, 0)
    s = jnp.where(allowed, scores_t, NEG_INF)
    picked = jnp.zeros(scores_t.shape, F32)
    for _ in range(MOBA_TOPK):
        mx = jnp.max(s, axis=0, keepdims=True)
        cand = jnp.where(jnp.logical_and(s == mx, s > NEG_INF), row, scores_t.shape[0])
        hit = row == jnp.min(cand, axis=0, keepdims=True)
        picked = jnp.where(hit, 1.0, picked)
        s = jnp.where(hit, NEG_INF, s)
    return picked


def _moba_prompt_kernel(q_ref, kb_ref, vt_ref, km_ref, o_ref, sa_ref, sb_ref, *, n_blocks, hp):
    tile = pl.program_id(2)
    blk = MOBA_BLOCK
    qb_n = MOBA_QUERY_BLOCKS
    qw = qb_n * blk
    first = tile * qb_n
    scale = A_HEAD_DIM ** -0.5
    nbp = -(-n_blocks // BF16_ROWS) * BF16_ROWS
    lane = lax.broadcasted_iota(jnp.int32, (blk, LANES), 1)
    blk_i = lax.broadcasted_iota(jnp.int32, (nbp, qw), 0)
    own_q = first + lax.broadcasted_iota(jnp.int32, (nbp, qw), 1) // blk
    heads = [slice(h * A_HEAD_DIM, (h + 1) * A_HEAD_DIM) for h in range(hp)]
    vrows = [slice(h * VT_ROWS, (h + 1) * VT_ROWS) for h in range(hp)]

    q_t = [q_ref[:, sl].T for sl in heads]
    q_tb = [(qt * scale).astype(BF16) for qt in q_t]
    own_logits = [[jnp.dot(kb_ref[pl.ds(pl.multiple_of((first + d) * blk, blk), blk), sl], qb,
                           preferred_element_type=F32) for sl, qb in zip(heads, q_tb)]
                  for d in range(qb_n)]
    kms = []
    for sl in heads:
        km = km_ref[:, sl]
        if nbp > n_blocks:
            km = jnp.concatenate([km, jnp.zeros((nbp - n_blocks, A_HEAD_DIM), F32)], axis=0)
        kms.append(km)
    scores = [jnp.dot(km, qt, precision=_HIGHEST, preferred_element_type=F32) for km, qt in zip(kms, q_t)]
    picks = [_top_blocks_t(sc, blk_i < own_q) for sc in scores]
    q_ext = []
    for qb, picked in zip(q_tb, picks):
        pad = jnp.zeros((LANES - nbp, qw), BF16)
        q_ext.append(jnp.concatenate([qb, (1.0 - picked).astype(BF16), pad], axis=0))

    def stage_logits(ref, j):
        st = pl.multiple_of(j * blk, blk)
        mask_cols = jnp.where(lane == j, -MASK_BIG, 0.0).astype(BF16)
        for h, (sl, qe) in enumerate(zip(heads, q_ext)):
            k_ext = jnp.concatenate([kb_ref[pl.ds(st, blk), sl], mask_cols], axis=1)
            ref[h] = jnp.dot(k_ext, qe, preferred_element_type=F32)

    stage_logits(sb_ref, 0)

    def accumulate(j, logits, state):
        probs, stats = [], []
        for s, (m, _) in zip(logits, state):
            m_new = jnp.maximum(m, jnp.max(s, axis=0, keepdims=True))
            probs.append(jnp.exp(s - m_new).astype(BF16))
            stats.append((m_new, jnp.exp(m - m_new)))
        return tuple(
            (m_new, alpha * acc + jnp.dot(vt_ref[j, vr, :], pb, preferred_element_type=F32))
            for vr, pb, (m_new, alpha), (_, acc) in zip(vrows, probs, stats, state))

    def accumulate_staged(j, ref, state):
        return accumulate(j, tuple(ref[h] for h in range(hp)), state)

    key_i = lax.broadcasted_iota(jnp.int32, (blk, qw), 0)
    qry_i = lax.broadcasted_iota(jnp.int32, (blk, qw), 1)
    state = tuple((jnp.full((1, qw), -MASK_BIG, F32), jnp.zeros((VT_ROWS, qw), F32)) for _ in heads)
    for d in range(qb_n):
        masked = []
        for s, picked in zip(own_logits[d], picks):
            sel = jnp.sum(jnp.where(blk_i == first + d, picked, 0.0), axis=0, keepdims=True)
            later = jnp.logical_and(qry_i >= (d + 1) * blk, jnp.broadcast_to(sel, (blk, qw)) > 0.5)
            causal = jnp.logical_and(qry_i >= d * blk,
                                     jnp.logical_and(qry_i < (d + 1) * blk, key_i <= qry_i - d * blk))
            masked.append(jnp.where(jnp.logical_or(causal, later), s, -MASK_BIG))
        state = accumulate(first + d, masked, state)

    def pair(i, state):
        j = 2 * i
        stage_logits(sa_ref, j + 1)
        state = accumulate_staged(j, sb_ref, state)
        stage_logits(sb_ref, j + 2)
        return accumulate_staged(j + 1, sa_ref, state)

    state = lax.fori_loop(0, tile, pair, state)
    for sl, (_, acc) in zip(heads, state):
        o_ref[:, sl] = (acc[0:A_HEAD_DIM, :] / acc[A_HEAD_DIM:A_HEAD_DIM + 1, :]).T


def _moba_prompt(q, kb, vt, kmean):
    b, s, e = q.shape
    nb = s // MOBA_BLOCK
    hp = MOBA_HEADS_PER_STEP
    qw = MOBA_QUERY_BLOCKS * MOBA_BLOCK
    assert s % qw == 0 and nb <= LANES and A_HEADS % hp == 0 and MOBA_QUERY_BLOCKS == 2
    w = hp * A_HEAD_DIM
    tile = pl.BlockSpec((None, qw, w), lambda bi, h, qi: (bi, qi, h))
    return pl.pallas_call(
        functools.partial(_moba_prompt_kernel, n_blocks=nb, hp=hp),
        grid=(b, A_HEADS // hp, s // qw),
        in_specs=[tile, pl.BlockSpec((None, s, w), lambda bi, h, qi: (bi, 0, h)),
                  pl.BlockSpec((None, nb, hp * VT_ROWS, MOBA_BLOCK), lambda bi, h, qi: (bi, 0, h, 0)),
                  pl.BlockSpec((None, nb, w), lambda bi, h, qi: (bi, 0, h))],
        out_specs=tile,
        out_shape=jax.ShapeDtypeStruct((b, s, e), F32),
        scratch_shapes=[pltpu.VMEM((hp, MOBA_BLOCK, qw), F32)] * 2,
        compiler_params=_params("parallel", "parallel", "arbitrary"),
        name="moba_prompt",
    )(q, kb, vt, kmean)


def _moba_decode_kernel(pt_ref, q_ref, kn_ref, vn_ref, ck_ref, cv_ref, o_ref, kbuf, vbuf, sem, logit_ref,
                        *, n_pages, page_rows):
    bi = pl.program_id(0)
    n_seq = pl.num_programs(0)
    slot = lax.rem(bi, 2)
    R = A_HEADS * DEC_PAD
    scale = A_HEAD_DIM ** -0.5
    pages_per_block = MOBA_BLOCK // page_rows
    n_blocks = n_pages // pages_per_block

    def page_copy(cache_ref, buf_ref, which, seq, sl, g):
        page = pt_ref[seq * n_pages + g]
        return pltpu.make_async_copy(cache_ref.at[page], buf_ref.at[sl, g], sem.at[sl, which])

    def start_pages(seq, sl):
        for g in range(n_pages):
            page_copy(ck_ref, kbuf, 0, seq, sl, g).start()
        for g in range(n_pages):
            page_copy(cv_ref, vbuf, 1, seq, sl, g).start()

    @pl.when(bi == 0)
    def _():
        start_pages(bi, slot)

    @pl.when(bi + 1 < n_seq)
    def _():
        start_pages(bi + 1, 1 - slot)

    def rows_by_head(ref):
        return jnp.concatenate([ref[:, h * A_HEAD_DIM:(h + 1) * A_HEAD_DIM] for h in range(A_HEADS)], axis=0)

    def by_head(ref):
        return jnp.concatenate([ref[pl.ds(h, page_rows, stride=A_HEADS), :].astype(BF16) for h in range(A_HEADS)],
                               axis=0)

    q = rows_by_head(q_ref)
    qb = q.astype(BF16)

    for g in range(n_pages):
        page_copy(ck_ref, kbuf, 0, bi, slot, g).wait()
    ksum = []
    for g in range(n_pages):
        kpage = kbuf.at[slot, g]
        lg = lax.dot_general(qb, by_head(kpage), _NT, preferred_element_type=F32)
        for h in range(A_HEADS):
            hr = slice(h * DEC_PAD, (h + 1) * DEC_PAD)
            logit_ref[g, hr, :] = lg[hr, h * page_rows:(h + 1) * page_rows] * scale
        part = jnp.sum(kpage[...].reshape(page_rows, A_HEADS, A_HEAD_DIM), axis=0)
        if g % pages_per_block == 0:
            ksum.append(part)
        else:
            ksum[-1] = ksum[-1] + part

    kmean = jnp.concatenate(ksum, axis=0) * (1.0 / MOBA_BLOCK)
    nbh = n_blocks * A_HEADS
    scores_t = lax.dot_general(kmean, q, _NT, precision=_HIGHEST, preferred_element_type=F32)
    brow = lax.broadcasted_iota(jnp.int32, (nbh, R), 0)
    qcol = lax.broadcasted_iota(jnp.int32, (nbh, R), 1)
    picked_t = _top_blocks_t(scores_t, brow % A_HEADS == qcol // DEC_PAD)
    erow = lax.broadcasted_iota(jnp.int32, (nbh, n_blocks * page_rows), 0)
    ecol = lax.broadcasted_iota(jnp.int32, (nbh, n_blocks * page_rows), 1)
    expand = jnp.where(erow // A_HEADS == ecol // page_rows, 1.0, 0.0).astype(BF16)
    keep = lax.dot_general(picked_t.astype(BF16), expand, _TN, preferred_element_type=F32)
    keep_blk = [keep[:, j * page_rows:(j + 1) * page_rows] > 0.5 for j in range(n_blocks)]

    kn = rows_by_head(kn_ref).astype(BF16)
    ln = lax.dot_general(qb, kn, _NT, preferred_element_type=F32) * scale
    rn = lax.broadcasted_iota(jnp.int32, (R, R), 0)
    cn = lax.broadcasted_iota(jnp.int32, (R, R), 1)
    ok_n = jnp.logical_and(rn // DEC_PAD == cn // DEC_PAD, cn % DEC_PAD <= rn % DEC_PAD)
    ln = jnp.where(ok_n, ln, NEG_INF)

    masked = [jnp.where(keep_blk[g // pages_per_block], logit_ref[g], NEG_INF) for g in range(n_pages)]
    mm = masked[0]
    for g in range(1, n_pages):
        mm = jnp.maximum(mm, masked[g])
    m = jnp.maximum(jnp.max(ln, axis=-1, keepdims=True), jnp.max(mm, axis=-1, keepdims=True))
    pn = jnp.exp(ln - m)
    probs = [jnp.exp(s - m) for s in masked]
    lsum = probs[0]
    for g in range(1, n_pages):
        lsum = lsum + probs[g]
    l = jnp.sum(pn, axis=-1, keepdims=True) + jnp.sum(lsum, axis=-1, keepdims=True)

    for g in range(n_pages):
        page_copy(cv_ref, vbuf, 1, bi, slot, g).wait()
    acc = jnp.dot(pn.astype(BF16), rows_by_head(vn_ref).astype(BF16), preferred_element_type=F32)
    zero = jnp.zeros((DEC_PAD, page_rows), F32)
    for g in range(n_pages):
        pe = probs[g]
        p_bd = jnp.concatenate(
            [jnp.concatenate([pe[h * DEC_PAD:(h + 1) * DEC_PAD, :] if hh == h else zero for hh in range(A_HEADS)],
                             axis=1) for h in range(A_HEADS)], axis=0)
        acc = acc + jnp.dot(p_bd.astype(BF16), by_head(vbuf.at[slot, g]), preferred_element_type=F32)
    res = acc / l
    for h in range(A_HEADS):
        o_ref[:, h * A_HEAD_DIM:(h + 1) * A_HEAD_DIM] = res[h * DEC_PAD:(h + 1) * DEC_PAD, :]


def _moba_decode(q, kn, vn, cache_k, cache_v, page_table):
    b, t, e = q.shape
    n_phys, page_rows = cache_k.shape[0], cache_k.shape[1]
    n_pages = page_table.shape[1]
    assert t == DEC_PAD and MOBA_BLOCK % page_rows == 0 and n_pages % (MOBA_BLOCK // page_rows) == 0
    cols = page_rows * A_HEADS
    ck = cache_k.reshape(n_phys, cols, A_HEAD_DIM)
    cv = cache_v.reshape(n_phys, cols, A_HEAD_DIM)
    pt = page_table.reshape(-1).astype(jnp.int32)

    tok = pl.BlockSpec((None, t, e), lambda bi, pt_: (bi, 0, 0))
    hbm = pl.BlockSpec(memory_space=pl.ANY)
    grid_spec = pltpu.PrefetchScalarGridSpec(
        num_scalar_prefetch=1,
        grid=(b,),
        in_specs=[tok, tok, tok, hbm, hbm],
        out_specs=tok,
        scratch_shapes=[pltpu.VMEM((2, n_pages, cols, A_HEAD_DIM), F32),
                        pltpu.VMEM((2, n_pages, cols, A_HEAD_DIM), F32),
                        pltpu.SemaphoreType.DMA((2, 2)),
                        pltpu.VMEM((n_pages, A_HEADS * DEC_PAD, page_rows), F32)],
    )
    return pl.pallas_call(
        functools.partial(_moba_decode_kernel, n_pages=n_pages, page_rows=page_rows),
        grid_spec=grid_spec,
        out_shape=jax.ShapeDtypeStruct((b, t, e), F32),
        compiler_params=_params("arbitrary"),
        name="moba_decode",
    )(pt, q, kn, vn, ck, cv)


ROW_TILE = 512


def _trunk(x, pos0, c0, n0, m0, past, w):
    b, s, d = x.shape
    n = b * s
    sp = -(-s // DEC_PAD) * DEC_PAD
    tm = min(ROW_TILE, n // 2)
    assert n % tm == 0 and (tm % s == 0 or s % tm == 0)
    xf = x.reshape(n, d)
    r3 = lambda t: jnp.pad(t.reshape(b, s, t.shape[-1]), ((0, 0), (0, sp - s), (0, 0)))
    flat = lambda t: t[:, :s].reshape(n, t.shape[-1])

    q, k, v, o, gates = _norm_proj(xf, w["norm_mix"][0], [w["wq_m"], w["wk_m"], w["wv_m"], w["wo_m"], w["wgate_m"]], tm)
    hs, c_new, n_new, m_new = _mlstm(r3(q), r3(k), r3(v), r3(gates), w["b_gate"], c0, n0, m0, s)
    xf = _mixer_out_ffn((flat(hs), o), xf, (w["norm_h_m"], w["w_out_m"], w["norm_ffn"][0], w["wg"][0],
                                            w["wu"][0], w["wd"][0]), tm, mlstm=True)

    pos = pos0 + jnp.arange(s, dtype=jnp.int32)
    cos, sin = _rope_tables(pos)
    if s < tm:
        cos, sin = jnp.tile(cos, (tm // s, 1)), jnp.tile(sin, (tm // s, 1))
    qa, ka, va, *blocks = _qkv(xf, w["norm_mix"][1], w["wq_a"], w["wk_a"], w["wv_a"], w["norm_q_a"],
                               w["norm_k_a"], cos, sin, tm, with_blocks=past is None)
    if past is None:
        kb, vt, kmean = blocks
        nb = s // MOBA_BLOCK
        att = _moba_prompt(r3(qa), r3(kb), vt.reshape(b, nb, -1, MOBA_BLOCK), kmean[:, 0, :].reshape(b, nb, -1))
    else:
        att = _moba_decode(r3(qa), r3(ka), r3(va), *past)
    xf = _mixer_out_ffn((flat(att),), xf, (w["w_out_a"], w["norm_ffn"][1], w["wg"][1], w["wu"][1], w["wd"][1]),
                        tm, mlstm=False)

    heads = lambda t: t.reshape(b, s, A_HEADS, A_HEAD_DIM)
    return xf.reshape(b, s, d), heads(ka), heads(va), c_new, n_new, m_new


def kernel(x_prompt, x_sample, state_C, state_n, state_m, cache_k, cache_v, page_table, norm_mix, norm_ffn,
           w_in_m, b_gate_m, norm_h_m, w_out_m, w_qkv_a, norm_q_a, norm_k_a, w_out_a, w_gu, w_down):
    assert norm_mix.shape[0] == 2 and w_in_m.shape[0] == 1 and w_qkv_a.shape[0] == 1
    bf = lambda t: t.astype(BF16)
    row = lambda t: t.reshape(1, -1).astype(F32)
    w_in = w_in_m[0]
    e_a = A_HEADS * A_HEAD_DIM
    gate_w = jnp.pad(w_in[:, 2 * M_QK_W + 2 * M_V_W:], ((0, 0), (0, LANES - 2 * M_HEADS)))
    w = {
        "norm_mix": [row(norm_mix[0]), row(norm_mix[1])],
        "norm_ffn": [row(norm_ffn[0]), row(norm_ffn[1])],
        "wq_m": bf(w_in[:, :M_QK_W]), "wk_m": bf(w_in[:, M_QK_W:2 * M_QK_W]),
        "wv_m": bf(w_in[:, 2 * M_QK_W:2 * M_QK_W + M_V_W]),
        "wo_m": bf(w_in[:, 2 * M_QK_W + M_V_W:2 * M_QK_W + 2 * M_V_W]),
        "wgate_m": bf(gate_w),
        "b_gate": jnp.pad(b_gate_m[0].astype(F32), (0, LANES - 2 * M_HEADS)).reshape(1, LANES),
        "norm_h_m": row(norm_h_m[0]), "w_out_m": bf(w_out_m[0]),
        "wq_a": bf(w_qkv_a[0][:, :e_a]), "wk_a": bf(w_qkv_a[0][:, e_a:2 * e_a]), "wv_a": bf(w_qkv_a[0][:, 2 * e_a:]),
        "norm_q_a": row(norm_q_a[0]), "norm_k_a": row(norm_k_a[0]), "w_out_a": bf(w_out_a[0]),
        "wg": [bf(w_gu[i][:, :FFN_HIDDEN]) for i in range(2)],
        "wu": [bf(w_gu[i][:, FFN_HIDDEN:]) for i in range(2)],
        "wd": [bf(w_down[i]) for i in range(2)],
    }

    def state(c, nn, mm):
        bsz = c.shape[0]
        m_row = jnp.pad(mm.astype(F32), ((0, 0), (0, LANES - M_HEADS))).reshape(bsz, 1, LANES)
        return c.astype(F32), nn.astype(F32).reshape(bsz, M_HEADS, 1, M_QK_DIM), m_row

    def unstate(c, nn, mm, dt):
        bsz = c.shape[0]
        return (c[None].astype(dt), nn.reshape(1, bsz, M_HEADS, M_QK_DIM).astype(dt),
                mm.reshape(bsz, LANES)[None, :, :M_HEADS].astype(dt))

    sd = state_C.dtype
    bp, sp, _ = x_prompt.shape
    zc = jnp.zeros((bp,) + state_C.shape[2:], F32)
    zn = jnp.zeros((bp,) + state_n.shape[2:], F32)
    zm = jnp.zeros((bp,) + state_m.shape[2:], F32)
    yp, pk, pv, pc, pn, pm = _trunk(x_prompt, 0, *state(zc, zn, zm), None, w)

    past_len = page_table.shape[1] * cache_k.shape[2]
    ys, sk, sv, sc, sn, sm = _trunk(x_sample, past_len, *state(state_C[0], state_n[0], state_m[0]),
                                    (cache_k[0], cache_v[0], page_table), w)

    return (yp, ys, pk[None], pv[None], *unstate(pc, pn, pm, sd), sk[None], sv[None], *unstate(sc, sn, sm, sd))
```

```python
import functools
import math

import jax
import jax.numpy as jnp
from jax import lax
from jax.experimental import pallas as pl
from jax.experimental.pallas import tpu as pltpu

F32 = jnp.float32
BF16 = jnp.bfloat16

D_MODEL = 1024
M_HEADS = 4
M_QK_DIM = 128
M_V_DIM = 256
M_QK_W = M_HEADS * M_QK_DIM
M_V_W = M_HEADS * M_V_DIM
A_HEADS = 8
A_HEAD_DIM = 128
MOBA_BLOCK = 256
MOBA_TOPK = 3
ROPE_THETA = 10000.0
FFN_HIDDEN = 2816
EPS = 1e-6

LANES = 128
SUBLANES = 8
BF16_ROWS = 16
VT_ROWS = A_HEAD_DIM + BF16_ROWS
VMEM_LIMIT_BYTES = 56 * 1024 * 1024

MLSTM_CHUNK = 128
DEC_PAD = SUBLANES
NEG_INF = float("-inf")

_HIGHEST = lax.Precision.HIGHEST
_NT = (((1,), (1,)), ((), ()))
_TN = (((0,), (0,)), ((), ()))


def _params(*sem):
    return pltpu.CompilerParams(dimension_semantics=sem, vmem_limit_bytes=VMEM_LIMIT_BYTES)


def _const_spec(shape):
    nd = len(shape)
    return pl.BlockSpec(shape, lambda *_: (0,) * nd, pipeline_mode=pl.Buffered(1))


def _rms(x, g):
    return x * lax.rsqrt(jnp.mean(x * x, axis=-1, keepdims=True) + EPS) * g


def _norm_proj_kernel(x_ref, g_ref, *refs, n_out):
    w_refs, o_refs = refs[:n_out], refs[n_out:]
    hb = _rms(x_ref[...], g_ref[...]).astype(BF16)
    for w_ref, o_ref in zip(w_refs, o_refs):
        o_ref[...] = jnp.dot(hb, w_ref[...], preferred_element_type=F32)


def _norm_proj(x, g, ws, tm):
    n, d = x.shape
    return pl.pallas_call(
        functools.partial(_norm_proj_kernel, n_out=len(ws)),
        grid=(n // tm,),
        in_specs=[pl.BlockSpec((tm, d), lambda i: (i, 0)), _const_spec((1, d))]
        + [_const_spec(w.shape) for w in ws],
        out_specs=[pl.BlockSpec((tm, w.shape[1]), lambda i: (i, 0)) for w in ws],
        out_shape=[jax.ShapeDtypeStruct((n, w.shape[1]), F32) for w in ws],
        compiler_params=_params("parallel"),
        name="norm_proj",
    )(x, g, *ws)


def _mlstm_kernel(q_ref, k_ref, v_ref, gt_ref, bias_ref, c0_ref, n0_ref, m0_ref,
                  hs_ref, c_ref, n_ref, m_ref, *pad_refs, rows, valid, chunk, single_chunk):
    nseq = q_ref.shape[0]
    ci = pl.program_id(1)
    if single_chunk:
        c_src, n_src, m_src = c0_ref, n0_ref, m0_ref
    else:
        c_src, n_src, m_src = c_ref, n_ref, m_ref

        @pl.when(ci == 0)
        def _():
            c_ref[...] = c0_ref[...]
            n_ref[...] = n0_ref[...]
            m_ref[...] = m0_ref[...]

    if rows != chunk:
        @pl.when(jnp.logical_and(pl.program_id(0) == 0, ci == 0))
        def _():
            for p in pad_refs:
                p[...] = jnp.zeros(p.shape, p.dtype)

    L = chunk
    row_l = lax.broadcasted_iota(jnp.int32, (L, LANES), 0)
    lane_l = lax.broadcasted_iota(jnp.int32, (L, LANES), 1)
    is_ig = lane_l < M_HEADS
    real = row_l < valid
    eye8 = (lax.broadcasted_iota(jnp.int32, (SUBLANES, LANES), 0)
            == lax.broadcasted_iota(jnp.int32, (SUBLANES, LANES), 1)).astype(F32)
    col_real = lax.broadcasted_iota(jnp.int32, (SUBLANES, L), 1) < valid
    causal = lax.broadcasted_iota(jnp.int32, (L, L), 1) <= lax.broadcasted_iota(jnp.int32, (L, L), 0)
    lane_1 = lax.broadcasted_iota(jnp.int32, (1, LANES), 1)
    scale = M_QK_DIM ** -0.5

    tiles, gsels, b_colss, cols_l = [], [], [], []
    for sq in range(nseq):
        if rows == chunk:
            tiles.append((q_ref[sq], k_ref[sq], v_ref[sq]))
            gt_all = gt_ref[sq]
        else:
            qp, kp, vp, gp = pad_refs
            qp[sq, 0:rows, :], kp[sq, 0:rows, :], vp[sq, 0:rows, :], gp[sq, 0:rows, :] = (
                q_ref[sq], k_ref[sq], v_ref[sq], gt_ref[sq])
            tiles.append((qp[sq], kp[sq], vp[sq]))
            gt_all = gp[sq]
        gc = gt_all + bias_ref[...]
        lfc = jnp.where(real, jax.nn.log_sigmoid(gc), 0.0)
        igc = jnp.where(real, gc, NEG_INF)
        gsels.append(jnp.where(is_ig, igc, lfc))
        b_cols = lfc
        shift = 1
        while shift < L:
            b_cols = b_cols + jnp.where(row_l >= shift, pltpu.roll(b_cols, shift, 0), 0.0)
            shift *= 2
        b_colss.append(b_cols)
        cols_l.append(jnp.where(is_ig, jnp.where(real, gc, 0.0), b_cols))
    rows_ts = [lax.dot_general(eye8, c, _NT, precision=_HIGHEST, preferred_element_type=F32) for c in cols_l]

    ch = [(sq, h) for sq in range(nseq) for h in range(M_HEADS)]
    qs = [tiles[sq][0][0:rows, h * M_QK_DIM:(h + 1) * M_QK_DIM] * scale for sq, h in ch]
    ks = [tiles[sq][1][:, h * M_QK_DIM:(h + 1) * M_QK_DIM] for sq, h in ch]
    qbs = [q.astype(BF16) for q in qs]
    kbs = [k.astype(BF16) for k in ks]
    vbs = [tiles[sq][2][:, h * M_V_DIM:(h + 1) * M_V_DIM].astype(BF16) for sq, h in ch]
    c_olds = [c_src[sq, h] for sq, h in ch]
    n_olds = [n_src[sq, h] for sq, h in ch]
    b_cs = [b_colss[sq][:, M_HEADS + h:M_HEADS + h + 1] for sq, h in ch]
    m_prevs = [m_src[sq][:, h:h + 1] for sq, h in ch]

    qk = [lax.dot_general(qb, kb, _NT, preferred_element_type=F32) for qb, kb in zip(qbs, kbs)]
    qc = [jnp.dot(qb, c.astype(BF16), preferred_element_type=F32) for qb, c in zip(qbs, c_olds)]

    twice = lambda t: jnp.concatenate([t, t], axis=1)
    a_s, m_ts, w_inters, b_reps = [], [], [], []
    for i, (sq, h) in enumerate(ch):
        b_rep = jnp.broadcast_to(b_cs[i], (L, LANES))
        b_q = b_rep[0:rows, :]
        ig_row = jnp.where(col_real, rows_ts[sq], NEG_INF)[h:h + 1, :]
        d = jnp.where(causal[0:rows, :], b_q[:, 0:L] - rows_ts[sq][M_HEADS + h:M_HEADS + h + 1, :] + ig_row, NEG_INF)
        m_inter = b_q + m_prevs[i]
        m_t = jnp.maximum(m_inter, jnp.max(d, axis=-1, keepdims=True))
        a_s.append(qk[i] * jnp.exp(d - m_t[:, 0:L]))
        m_ts.append(m_t)
        w_inters.append(jnp.exp(m_inter - m_t))
        b_reps.append(b_rep)
    av = [jnp.dot(a.astype(BF16), vb, preferred_element_type=F32) for a, vb in zip(a_s, vbs)]

    kws, w_states = [], []
    m_new_rows = [m_src[sq] for sq in range(nseq)]
    for i, (sq, h) in enumerate(ch):
        num = twice(w_inters[i]) * qc[i] + av[i]
        den = (w_inters[i] * jnp.sum(qs[i] * n_olds[i], axis=-1, keepdims=True)
               + jnp.sum(a_s[i], axis=-1, keepdims=True))
        hval = num / twice(jnp.maximum(jnp.abs(den), jnp.exp(-m_ts[i])))
        hs_ref[sq, :, h * M_V_DIM:(h + 1) * M_V_DIM] = hval
        m_new = m_ts[i][rows - 1:rows, :]
        b_last = b_reps[i][rows - 1:rows, :]
        w_states.append(jnp.exp(b_last + m_prevs[i] - m_new))
        ig_rep = jnp.broadcast_to(gsels[sq][:, h:h + 1], (L, LANES))
        kws.append(ks[i] * jnp.exp(b_last - b_reps[i] + ig_rep - m_new))
        m_new_rows[sq] = jnp.where(lane_1 == h, m_new, m_new_rows[sq])
    kv = [lax.dot_general(kw.astype(BF16), vb, _TN, preferred_element_type=F32) for kw, vb in zip(kws, vbs)]
    for i, (sq, h) in enumerate(ch):
        c_ref[sq, h] = twice(w_states[i]) * c_olds[i] + kv[i]
        n_ref[sq, h] = w_states[i] * n_olds[i] + jnp.sum(kws[i], axis=0, keepdims=True)
    for sq in range(nseq):
        m_ref[sq] = m_new_rows[sq]


MLSTM_SEQS_PER_STEP = {True: 4, False: 8}


def _mlstm(q, k, v, gates, bias, c0, n0, m0, valid):
    b, s, _ = q.shape
    assert MLSTM_CHUNK <= LANES
    nseq = math.gcd(b, MLSTM_SEQS_PER_STEP[s >= MLSTM_CHUNK])
    if s >= MLSTM_CHUNK:
        rows, nc, chunk = MLSTM_CHUNK, s // MLSTM_CHUNK, MLSTM_CHUNK
        assert valid == s and s % MLSTM_CHUNK == 0
        scratch = []
    else:
        rows, nc, chunk = s, 1, -(-s // BF16_ROWS) * BF16_ROWS
        scratch = [pltpu.VMEM((nseq, chunk, w), F32) for w in (M_QK_W, M_QK_W, M_V_W, LANES)]
    tok = lambda w: pl.BlockSpec((nseq, rows, w), lambda bi, ci: (bi, ci, 0))
    st_c = pl.BlockSpec((nseq, M_HEADS, M_QK_DIM, M_V_DIM), lambda bi, ci: (bi, 0, 0, 0))
    st_n = pl.BlockSpec((nseq, M_HEADS, 1, M_QK_DIM), lambda bi, ci: (bi, 0, 0, 0))
    st_m = pl.BlockSpec((nseq, 1, LANES), lambda bi, ci: (bi, 0, 0))
    return pl.pallas_call(
        functools.partial(_mlstm_kernel, rows=rows, valid=valid, chunk=chunk, single_chunk=nc == 1),
        grid=(b // nseq, nc),
        in_specs=[tok(M_QK_W), tok(M_QK_W), tok(M_V_W), tok(LANES), _const_spec((1, LANES)), st_c, st_n, st_m],
        out_specs=[tok(M_V_W), st_c, st_n, st_m],
        out_shape=[jax.ShapeDtypeStruct((b, s, M_V_W), F32), jax.ShapeDtypeStruct(c0.shape, F32),
                   jax.ShapeDtypeStruct(n0.shape, F32), jax.ShapeDtypeStruct(m0.shape, F32)],
        scratch_shapes=scratch,
        compiler_params=_params("arbitrary" if scratch else "parallel", "arbitrary"),
        name="mlstm_chunks",
    )(q, k, v, gates, bias, c0, n0, m0)


FFN_CHUNKS = 11


def _ffn_residual(x, g_ref, wg_ref, wu_ref, wd_ref):
    hb = _rms(x, g_ref[...]).astype(BF16)
    acc = x
    cw = FFN_HIDDEN // FFN_CHUNKS
    for c in range(FFN_CHUNKS):
        sl = slice(c * cw, (c + 1) * cw)
        gg = jnp.dot(hb, wg_ref[:, sl], preferred_element_type=F32)
        uu = jnp.dot(hb, wu_ref[:, sl], preferred_element_type=F32)
        act = (jax.nn.silu(gg) * uu).astype(BF16)
        acc = acc + jnp.dot(act, wd_ref[sl, :], preferred_element_type=F32)
    return acc


def _mlstm_out_ffn_kernel(hs_ref, o_ref, x_ref, gh_ref, w_ref, g_ref, wg_ref, wu_ref, wd_ref, out_ref):
    parts = []
    for h in range(M_HEADS):
        sl = slice(h * M_V_DIM, (h + 1) * M_V_DIM)
        parts.append(_rms(hs_ref[:, sl], gh_ref[:, sl]))
    y = jnp.concatenate(parts, axis=-1) * jax.nn.sigmoid(o_ref[...])
    x1 = x_ref[...] + jnp.dot(y.astype(BF16), w_ref[...], preferred_element_type=F32)
    out_ref[...] = _ffn_residual(x1, g_ref, wg_ref, wu_ref, wd_ref)


def _proj_ffn_kernel(a_ref, x_ref, w_ref, g_ref, wg_ref, wu_ref, wd_ref, out_ref):
    x1 = x_ref[...] + jnp.dot(a_ref[...].astype(BF16), w_ref[...], preferred_element_type=F32)
    out_ref[...] = _ffn_residual(x1, g_ref, wg_ref, wu_ref, wd_ref)


def _mixer_out_ffn(acts, x, consts, tm, mlstm):
    n, d = x.shape
    row = lambda a: pl.BlockSpec((tm, a.shape[1]), lambda i: (i, 0))
    return pl.pallas_call(
        _mlstm_out_ffn_kernel if mlstm else _proj_ffn_kernel,
        grid=(n // tm,),
        in_specs=[row(a) for a in acts] + [row(x)] + [_const_spec(c.shape) for c in consts],
        out_specs=row(x),
        out_shape=jax.ShapeDtypeStruct((n, d), F32),
        compiler_params=_params("parallel"),
        name="mlstm_out_ffn" if mlstm else "attn_out_ffn",
    )(*acts, x, *consts)


def _qkv_kernel(x_ref, g_ref, wq_ref, wk_ref, wv_ref, gq_ref, gk_ref, cos_ref, sin_ref,
                q_ref, k_ref, v_ref, *blk_refs):
    hb = _rms(x_ref[...], g_ref[...]).astype(BF16)
    cos, sin = cos_ref[...], sin_ref[...]

    kb_ref, vt_ref, km_ref = blk_refs if blk_refs else (None, None, None)
    tm = x_ref.shape[0]
    rt = min(tm, MOBA_BLOCK)
    cw = 2 * A_HEAD_DIM

    def norm_rope(t, gh_ref, rows):
        y = _rms(t, gh_ref[...])
        return y * cos[rows, :] + pltpu.roll(y, A_HEAD_DIM // 2, 1) * sin[rows, :]

    for j in range(tm // rt):
        rows = slice(j * rt, (j + 1) * rt)
        hr = hb[rows, :]
        for c in range(A_HEADS // 2):
            t = jnp.dot(hr, wq_ref[:, c * cw:(c + 1) * cw], preferred_element_type=F32)
            for i in range(2):
                sl = slice((2 * c + i) * A_HEAD_DIM, (2 * c + i + 1) * A_HEAD_DIM)
                q_ref[rows, sl] = norm_rope(t[:, i * A_HEAD_DIM:(i + 1) * A_HEAD_DIM], gq_ref, rows)
        for c in range(A_HEADS // 2):
            t = jnp.dot(hr, wk_ref[:, c * cw:(c + 1) * cw], preferred_element_type=F32)
            for i in range(2):
                sl = slice((2 * c + i) * A_HEAD_DIM, (2 * c + i + 1) * A_HEAD_DIM)
                k = norm_rope(t[:, i * A_HEAD_DIM:(i + 1) * A_HEAD_DIM], gk_ref, rows)
                k_ref[rows, sl] = k
                if blk_refs:
                    kb_ref[rows, sl] = k.astype(BF16)
                    mean = jnp.mean(k, axis=0, keepdims=True)
                    km_ref[j, :, sl] = jnp.broadcast_to(mean, (SUBLANES, A_HEAD_DIM))
        for c in range(A_HEADS // 2):
            t = jnp.dot(hr, wv_ref[:, c * cw:(c + 1) * cw], preferred_element_type=F32)
            v_ref[rows, c * cw:(c + 1) * cw] = t
            if blk_refs:
                for i in range(2):
                    r0 = (2 * c + i) * VT_ROWS
                    vt_ref[j, r0:r0 + A_HEAD_DIM, :] = t[:, i * A_HEAD_DIM:(i + 1) * A_HEAD_DIM].T.astype(BF16)
                    vt_ref[j, r0 + A_HEAD_DIM:r0 + VT_ROWS, :] = jnp.ones((BF16_ROWS, MOBA_BLOCK), BF16)


def _qkv(x, g, wq, wk, wv, gq, gk, cos, sin, tm, with_blocks):
    n, d = x.shape
    e = wq.shape[1]
    n_tab = cos.shape[0] // tm
    row = lambda wd: pl.BlockSpec((tm, wd), lambda i: (i, 0))
    tab = pl.BlockSpec((tm, A_HEAD_DIM), lambda i: (i % n_tab, 0))
    out_specs = [row(e), row(e), row(e)]
    out_shape = [jax.ShapeDtypeStruct((n, e), F32)] * 3
    if with_blocks:
        assert tm % MOBA_BLOCK == 0
        gpt = tm // MOBA_BLOCK
        vrows = A_HEADS * VT_ROWS
        out_specs += [row(e), pl.BlockSpec((gpt, vrows, MOBA_BLOCK), lambda i: (i, 0, 0)),
                      pl.BlockSpec((gpt, SUBLANES, e), lambda i: (i, 0, 0))]
        out_shape += [jax.ShapeDtypeStruct((n, e), BF16),
                      jax.ShapeDtypeStruct((n // MOBA_BLOCK, vrows, MOBA_BLOCK), BF16),
                      jax.ShapeDtypeStruct((n // MOBA_BLOCK, SUBLANES, e), F32)]
    return pl.pallas_call(
        _qkv_kernel,
        grid=(n // tm,),
        in_specs=[row(d), _const_spec((1, d)), _const_spec(wq.shape), _const_spec(wk.shape),
                  _const_spec(wv.shape), _const_spec((1, A_HEAD_DIM)), _const_spec((1, A_HEAD_DIM)), tab, tab],
        out_specs=out_specs,
        out_shape=out_shape,
        compiler_params=_params("parallel"),
        name="qkv_rope",
    )(x, g, wq, wk, wv, gq, gk, cos, sin)


def _rope_tables(pos):
    half = A_HEAD_DIM // 2
    inv_freq = ROPE_THETA ** (-jnp.arange(half, dtype=F32) / half)
    ang = pos.astype(F32)[:, None] * inv_freq[None, :]
    c, s = jnp.cos(ang), jnp.sin(ang)
    return jnp.concatenate([c, c], axis=-1), jnp.concatenate([-s, s], axis=-1)


MOBA_QUERY_BLOCKS = 2
MOBA_HEADS_PER_STEP = 4
MASK_BIG = 2.0 ** 100


def _top_blocks_t(scores_t, allowed):
    row = lax.broadcasted_iota(jnp.int32, scores_t.shape, 0)
    s = jnp.where(allowed, scores_t, NEG_INF)
    picked = jnp.zeros(scores_t.shape, F32)
    for _ in range(MOBA_TOPK):
        mx = jnp.max(s, axis=0, keepdims=True)
        cand = jnp.where(jnp.logical_and(s == mx, s > NEG_INF), row, scores_t.shape[0])
        hit = row == jnp.min(cand, axis=0, keepdims=True)
        picked = jnp.where(hit, 1.0, picked)
        s = jnp.where(hit, NEG_INF, s)
    return picked


def _moba_prompt_kernel(q_ref, kb_ref, vt_ref, km_ref, o_ref, sa_ref, sb_ref, *, n_blocks, hp):
    tile = pl.program_id(2)
    blk = MOBA_BLOCK
    qb_n = MOBA_QUERY_BLOCKS
    qw = qb_n * blk
    first = tile * qb_n
    scale = A_HEAD_DIM ** -0.5
    nbp = -(-n_blocks // BF16_ROWS) * BF16_ROWS
    lane = lax.broadcasted_iota(jnp.int32, (blk, LANES), 1)
    blk_i = lax.broadcasted_iota(jnp.int32, (nbp, qw), 0)
    own_q = first + lax.broadcasted_iota(jnp.int32, (nbp, qw), 1) // blk
    heads = [slice(h * A_HEAD_DIM, (h + 1) * A_HEAD_DIM) for h in range(hp)]
    vrows = [slice(h * VT_ROWS, (h + 1) * VT_ROWS) for h in range(hp)]

    q_t = [q_ref[:, sl].T for sl in heads]
    q_tb = [(qt * scale).astype(BF16) for qt in q_t]
    own_logits = [[jnp.dot(kb_ref[pl.ds(pl.multiple_of((first + d) * blk, blk), blk), sl], qb,
                           preferred_element_type=F32) for sl, qb in zip(heads, q_tb)]
                  for d in range(qb_n)]
    kms = []
    for sl in heads:
        km = km_ref[:, sl]
        if nbp > n_blocks:
            km = jnp.concatenate([km, jnp.zeros((nbp - n_blocks, A_HEAD_DIM), F32)], axis=0)
        kms.append(km)
    scores = [jnp.dot(km, qt, precision=_HIGHEST, preferred_element_type=F32) for km, qt in zip(kms, q_t)]
    picks = [_top_blocks_t(sc, blk_i < own_q) for sc in scores]
    q_ext = []
    for qb, picked in zip(q_tb, picks):
        pad = jnp.zeros((LANES - nbp, qw), BF16)
        q_ext.append(jnp.concatenate([qb, (1.0 - picked).astype(BF16), pad], axis=0))

    def stage_logits(ref, j):
        st = pl.multiple_of(j * blk, blk)
        mask_cols = jnp.where(lane == j, -MASK_BIG, 0.0).astype(BF16)
        for h, (sl, qe) in enumerate(zip(heads, q_ext)):
            k_ext = jnp.concatenate([kb_ref[pl.ds(st, blk), sl], mask_cols], axis=1)
            ref[h] = jnp.dot(k_ext, qe, preferred_element_type=F32)

    stage_logits(sb_ref, 0)

    def accumulate(j, logits, state):
        probs, stats = [], []
        for s, (m, _) in zip(logits, state):
            m_new = jnp.maximum(m, jnp.max(s, axis=0, keepdims=True))
            probs.append(jnp.exp(s - m_new).astype(BF16))
            stats.append((m_new, jnp.exp(m - m_new)))
        return tuple(
            (m_new, alpha * acc + jnp.dot(vt_ref[j, vr, :], pb, preferred_element_type=F32))
            for vr, pb, (m_new, alpha), (_, acc) in zip(vrows, probs, stats, state))

    def accumulate_staged(j, ref, state):
        return accumulate(j, tuple(ref[h] for h in range(hp)), state)

    key_i = lax.broadcasted_iota(jnp.int32, (blk, qw), 0)
    qry_i = lax.broadcasted_iota(jnp.int32, (blk, qw), 1)
    state = tuple((jnp.full((1, qw), -MASK_BIG, F32), jnp.zeros((VT_ROWS, qw), F32)) for _ in heads)
    for d in range(qb_n):
        masked = []
        for s, picked in zip(own_logits[d], picks):
            sel = jnp.sum(jnp.where(blk_i == first + d, picked, 0.0), axis=0, keepdims=True)
            later = jnp.logical_and(qry_i >= (d + 1) * blk, jnp.broadcast_to(sel, (blk, qw)) > 0.5)
            causal = jnp.logical_and(qry_i >= d * blk,
                                     jnp.logical_and(qry_i < (d + 1) * blk, key_i <= qry_i - d * blk))
            masked.append(jnp.where(jnp.logical_or(causal, later), s, -MASK_BIG))
        state = accumulate(first + d, masked, state)

    def pair(i, state):
        j = 2 * i
        stage_logits(sa_ref, j + 1)
        state = accumulate_staged(j, sb_ref, state)
        stage_logits(sb_ref, j + 2)
        return accumulate_staged(j + 1, sa_ref, state)

    state = lax.fori_loop(0, tile, pair, state)
    for sl, (_, acc) in zip(heads, state):
        o_ref[:, sl] = (acc[0:A_HEAD_DIM, :] / acc[A_HEAD_DIM:A_HEAD_DIM + 1, :]).T


def _moba_prompt(q, kb, vt, kmean):
    b, s, e = q.shape
    nb = s // MOBA_BLOCK
    hp = MOBA_HEADS_PER_STEP
    qw = MOBA_QUERY_BLOCKS * MOBA_BLOCK
    assert s % qw == 0 and nb <= LANES and A_HEADS % hp == 0 and MOBA_QUERY_BLOCKS == 2
    w = hp * A_HEAD_DIM
    tile = pl.BlockSpec((None, qw, w), lambda bi, h, qi: (bi, qi, h))
    return pl.pallas_call(
        functools.partial(_moba_prompt_kernel, n_blocks=nb, hp=hp),
        grid=(b, A_HEADS // hp, s // qw),
        in_specs=[tile, pl.BlockSpec((None, s, w), lambda bi, h, qi: (bi, 0, h)),
                  pl.BlockSpec((None, nb, hp * VT_ROWS, MOBA_BLOCK), lambda bi, h, qi: (bi, 0, h, 0)),
                  pl.BlockSpec((None, nb, w), lambda bi, h, qi: (bi, 0, h))],
        out_specs=tile,
        out_shape=jax.ShapeDtypeStruct((b, s, e), F32),
        scratch_shapes=[pltpu.VMEM((hp, MOBA_BLOCK, qw), F32)] * 2,
        compiler_params=_params("parallel", "parallel", "arbitrary"),
        name="moba_prompt",
    )(q, kb, vt, kmean)


def _moba_decode_kernel(pt_ref, q_ref, kn_ref, vn_ref, ck_ref, cv_ref, o_ref, kbuf, vbuf, sem, logit_ref,
                        *, n_pages, page_rows):
    bi = pl.program_id(0)
    n_seq = pl.num_programs(0)
    slot = lax.rem(bi, 2)
    R = A_HEADS * DEC_PAD
    scale = A_HEAD_DIM ** -0.5
    pages_per_block = MOBA_BLOCK // page_rows
    n_blocks = n_pages // pages_per_block

    def page_copy(cache_ref, buf_ref, which, seq, sl, g):
        page = pt_ref[seq * n_pages + g]
        return pltpu.make_async_copy(cache_ref.at[page], buf_ref.at[sl, g], sem.at[sl, which])

    def start_pages(seq, sl):
        for g in range(n_pages):
            page_copy(ck_ref, kbuf, 0, seq, sl, g).start()
        for g in range(n_pages):
            page_copy(cv_ref, vbuf, 1, seq, sl, g).start()

    @pl.when(bi == 0)
    def _():
        start_pages(bi, slot)

    @pl.when(bi + 1 < n_seq)
    def _():
        start_pages(bi + 1, 1 - slot)

    def rows_by_head(ref):
        return jnp.concatenate([ref[:, h * A_HEAD_DIM:(h + 1) * A_HEAD_DIM] for h in range(A_HEADS)], axis=0)

    def by_head(ref):
        return jnp.concatenate([ref[pl.ds(h, page_rows, stride=A_HEADS), :].astype(BF16) for h in range(A_HEADS)],
                               axis=0)

    q = rows_by_head(q_ref)
    qb = q.astype(BF16)

    for g in range(n_pages):
        page_copy(ck_ref, kbuf, 0, bi, slot, g).wait()
    ksum = []
    for g in range(n_pages):
        kpage = kbuf.at[slot, g]
        lg = lax.dot_general(qb, by_head(kpage), _NT, preferred_element_type=F32)
        for h in range(A_HEADS):
            hr = slice(h * DEC_PAD, (h + 1) * DEC_PAD)
            logit_ref[g, hr, :] = lg[hr, h * page_rows:(h + 1) * page_rows] * scale
        part = jnp.sum(kpage[...].reshape(page_rows, A_HEADS, A_HEAD_DIM), axis=0)
        if g % pages_per_block == 0:
            ksum.append(part)
        else:
            ksum[-1] = ksum[-1] + part

    kmean = jnp.concatenate(ksum, axis=0) * (1.0 / MOBA_BLOCK)
    nbh = n_blocks * A_HEADS
    scores_t = lax.dot_general(kmean, q, _NT, precision=_HIGHEST, preferred_element_type=F32)
    brow = lax.broadcasted_iota(jnp.int32, (nbh, R), 0)
    qcol = lax.broadcasted_iota(jnp.int32, (nbh, R), 1)
    picked_t = _top_blocks_t(scores_t, brow % A_HEADS == qcol // DEC_PAD)
    erow = lax.broadcasted_iota(jnp.int32, (nbh, n_blocks * page_rows), 0)
    ecol = lax.broadcasted_iota(jnp.int32, (nbh, n_blocks * page_rows), 1)
    expand = jnp.where(erow // A_HEADS == ecol // page_rows, 1.0, 0.0).astype(BF16)
    keep = lax.dot_general(picked_t.astype(BF16), expand, _TN, preferred_element_type=F32)
    keep_blk = [keep[:, j * page_rows:(j + 1) * page_rows] > 0.5 for j in range(n_blocks)]

    kn = rows_by_head(kn_ref).astype(BF16)
    ln = lax.dot_general(qb, kn, _NT, preferred_element_type=F32) * scale
    rn = lax.broadcasted_iota(jnp.int32, (R, R), 0)
    cn = lax.broadcasted_iota(jnp.int32, (R, R), 1)
    ok_n = jnp.logical_and(rn // DEC_PAD == cn // DEC_PAD, cn % DEC_PAD <= rn % DEC_PAD)
    ln = jnp.where(ok_n, ln, NEG_INF)

    masked = [jnp.where(keep_blk[g // pages_per_block], logit_ref[g], NEG_INF) for g in range(n_pages)]
    mm = masked[0]
    for g in range(1, n_pages):
        mm = jnp.maximum(mm, masked[g])
    m = jnp.maximum(jnp.max(ln, axis=-1, keepdims=True), jnp.max(mm, axis=-1, keepdims=True))
    pn = jnp.exp(ln - m)
    probs = [jnp.exp(s - m) for s in masked]
    lsum = probs[0]
    for g in range(1, n_pages):
        lsum = lsum + probs[g]
    l = jnp.sum(pn, axis=-1, keepdims=True) + jnp.sum(lsum, axis=-1, keepdims=True)

    for g in range(n_pages):
        page_copy(cv_ref, vbuf, 1, bi, slot, g).wait()
    acc = jnp.dot(pn.astype(BF16), rows_by_head(vn_ref).astype(BF16), preferred_element_type=F32)
    zero = jnp.zeros((DEC_PAD, page_rows), F32)
    for g in range(n_pages):
        pe = probs[g]
        p_bd = jnp.concatenate(
            [jnp.concatenate([pe[h * DEC_PAD:(h + 1) * DEC_PAD, :] if hh == h else zero for hh in range(A_HEADS)],
                             axis=1) for h in range(A_HEADS)], axis=0)
        acc = acc + jnp.dot(p_bd.astype(BF16), by_head(vbuf.at[slot, g]), preferred_element_type=F32)
    res = acc / l
    for h in range(A_HEADS):
        o_ref[:, h * A_HEAD_DIM:(h + 1) * A_HEAD_DIM] = res[h * DEC_PAD:(h + 1) * DEC_PAD, :]


def _moba_decode(q, kn, vn, cache_k, cache_v, page_table):
    b, t, e = q.shape
    n_phys, page_rows = cache_k.shape[0], cache_k.shape[1]
    n_pages = page_table.shape[1]
    assert t == DEC_PAD and MOBA_BLOCK % page_rows == 0 and n_pages % (MOBA_BLOCK // page_rows) == 0
    cols = page_rows * A_HEADS
    ck = cache_k.reshape(n_phys, cols, A_HEAD_DIM)
    cv = cache_v.reshape(n_phys, cols, A_HEAD_DIM)
    pt = page_table.reshape(-1).astype(jnp.int32)

    tok = pl.BlockSpec((None, t, e), lambda bi, pt_: (bi, 0, 0))
    hbm = pl.BlockSpec(memory_space=pl.ANY)
    grid_spec = pltpu.PrefetchScalarGridSpec(
        num_scalar_prefetch=1,
        grid=(b,),
        in_specs=[tok, tok, tok, hbm, hbm],
        out_specs=tok,
        scratch_shapes=[pltpu.VMEM((2, n_pages, cols, A_HEAD_DIM), F32),
                        pltpu.VMEM((2, n_pages, cols, A_HEAD_DIM), F32),
                        pltpu.SemaphoreType.DMA((2, 2)),
                        pltpu.VMEM((n_pages, A_HEADS * DEC_PAD, page_rows), F32)],
    )
    return pl.pallas_call(
        functools.partial(_moba_decode_kernel, n_pages=n_pages, page_rows=page_rows),
        grid_spec=grid_spec,
        out_shape=jax.ShapeDtypeStruct((b, t, e), F32),
        compiler_params=_params("arbitrary"),
        name="moba_decode",
    )(pt, q, kn, vn, ck, cv)


ROW_TILE = 512


def _trunk(x, pos0, c0, n0, m0, past, w):
    b, s, d = x.shape
    n = b * s
    sp = -(-s // DEC_PAD) * DEC_PAD
    tm = min(ROW_TILE, n // 2)
    assert n % tm == 0 and (tm % s == 0 or s % tm == 0)
    xf = x.reshape(n, d)
    r3 = lambda t: jnp.pad(t.reshape(b, s, t.shape[-1]), ((0, 0), (0, sp - s), (0, 0)))
    flat = lambda t: t[:, :s].reshape(n, t.shape[-1])

    q, k, v, o, gates = _norm_proj(xf, w["norm_mix"][0], [w["wq_m"], w["wk_m"], w["wv_m"], w["wo_m"], w["wgate_m"]], tm)
    hs, c_new, n_new, m_new = _mlstm(r3(q), r3(k), r3(v), r3(gates), w["b_gate"], c0, n0, m0, s)
    xf = _mixer_out_ffn((flat(hs), o), xf, (w["norm_h_m"], w["w_out_m"], w["norm_ffn"][0], w["wg"][0],
                                            w["wu"][0], w["wd"][0]), tm, mlstm=True)

    pos = pos0 + jnp.arange(s, dtype=jnp.int32)
    cos, sin = _rope_tables(pos)
    if s < tm:
        cos, sin = jnp.tile(cos, (tm // s, 1)), jnp.tile(sin, (tm // s, 1))
    qa, ka, va, *blocks = _qkv(xf, w["norm_mix"][1], w["wq_a"], w["wk_a"], w["wv_a"], w["norm_q_a"],
                               w["norm_k_a"], cos, sin, tm, with_blocks=past is None)
    if past is None:
        kb, vt, kmean = blocks
        nb = s // MOBA_BLOCK
        att = _moba_prompt(r3(qa), r3(kb), vt.reshape(b, nb, -1, MOBA_BLOCK), kmean[:, 0, :].reshape(b, nb, -1))
    else:
        att = _moba_decode(r3(qa), r3(ka), r3(va), *past)
    xf = _mixer_out_ffn((flat(att),), xf, (w["w_out_a"], w["norm_ffn"][1], w["wg"][1], w["wu"][1], w["wd"][1]),
                        tm, mlstm=False)

    heads = lambda t: t.reshape(b, s, A_HEADS, A_HEAD_DIM)
    return xf.reshape(b, s, d), heads(ka), heads(va), c_new, n_new, m_new


def kernel(x_prompt, x_sample, state_C, state_n, state_m, cache_k, cache_v, page_table, norm_mix, norm_ffn,
           w_in_m, b_gate_m, norm_h_m, w_out_m, w_qkv_a, norm_q_a, norm_k_a, w_out_a, w_gu, w_down):
    assert norm_mix.shape[0] == 2 and w_in_m.shape[0] == 1 and w_qkv_a.shape[0] == 1
    bf = lambda t: t.astype(BF16)
    row = lambda t: t.reshape(1, -1).astype(F32)
    w_in = w_in_m[0]
    e_a = A_HEADS * A_HEAD_DIM
    gate_w = jnp.pad(w_in[:, 2 * M_QK_W + 2 * M_V_W:], ((0, 0), (0, LANES - 2 * M_HEADS)))
    w = {
        "norm_mix": [row(norm_mix[0]), row(norm_mix[1])],
        "norm_ffn": [row(norm_ffn[0]), row(norm_ffn[1])],
        "wq_m": bf(w_in[:, :M_QK_W]), "wk_m": bf(w_in[:, M_QK_W:2 * M_QK_W]),
        "wv_m": bf(w_in[:, 2 * M_QK_W:2 * M_QK_W + M_V_W]),
        "wo_m": bf(w_in[:, 2 * M_QK_W + M_V_W:2 * M_QK_W + 2 * M_V_W]),
        "wgate_m": bf(gate_w),
        "b_gate": jnp.pad(b_gate_m[0].astype(F32), (0, LANES - 2 * M_HEADS)).reshape(1, LANES),
        "norm_h_m": row(norm_h_m[0]), "w_out_m": bf(w_out_m[0]),
        "wq_a": bf(w_qkv_a[0][:, :e_a]), "wk_a": bf(w_qkv_a[0][:, e_a:2 * e_a]), "wv_a": bf(w_qkv_a[0][:, 2 * e_a:]),
        "norm_q_a": row(norm_q_a[0]), "norm_k_a": row(norm_k_a[0]), "w_out_a": bf(w_out_a[0]),
        "wg": [bf(w_gu[i][:, :FFN_HIDDEN]) for i in range(2)],
        "wu": [bf(w_gu[i][:, FFN_HIDDEN:]) for i in range(2)],
        "wd": [bf(w_down[i]) for i in range(2)],
    }

    def state(c, nn, mm):
        bsz = c.shape[0]
        m_row = jnp.pad(mm.astype(F32), ((0, 0), (0, LANES - M_HEADS))).reshape(bsz, 1, LANES)
        return c.astype(F32), nn.astype(F32).reshape(bsz, M_HEADS, 1, M_QK_DIM), m_row

    def unstate(c, nn, mm, dt):
        bsz = c.shape[0]
        return (c[None].astype(dt), nn.reshape(1, bsz, M_HEADS, M_QK_DIM).astype(dt),
                mm.reshape(bsz, LANES)[None, :, :M_HEADS].astype(dt))

    sd = state_C.dtype
    bp, sp, _ = x_prompt.shape
    zc = jnp.zeros((bp,) + state_C.shape[2:], F32)
    zn = jnp.zeros((bp,) + state_n.shape[2:], F32)
    zm = jnp.zeros((bp,) + state_m.shape[2:], F32)
    yp, pk, pv, pc, pn, pm = _trunk(x_prompt, 0, *state(zc, zn, zm), None, w)

    past_len = page_table.shape[1] * cache_k.shape[2]
    ys, sk, sv, sc, sn, sm = _trunk(x_sample, past_len, *state(state_C[0], state_n[0], state_m[0]),
                                    (cache_k[0], cache_v[0], page_table), w)

    return (yp, ys, pk[None], pv[None], *unstate(pc, pn, pm, sd), sk[None], sv[None], *unstate(sc, sn, sm, sd))
```

```python
import functools
import math

import jax
import jax.numpy as jnp
from jax import lax
from jax.experimental import pallas as pl
from jax.experimental.pallas import tpu as pltpu

F32 = jnp.float32
BF16 = jnp.bfloat16

D_MODEL = 1024
M_HEADS = 4
M_QK_DIM = 128
M_V_DIM = 256
M_QK_W = M_HEADS * M_QK_DIM
M_V_W = M_HEADS * M_V_DIM
A_HEADS = 8
A_HEAD_DIM = 128
MOBA_BLOCK = 256
MOBA_TOPK = 3
ROPE_THETA = 10000.0
FFN_HIDDEN = 2816
EPS = 1e-6

LANES = 128
SUBLANES = 8
BF16_ROWS = 16
VT_ROWS = A_HEAD_DIM + BF16_ROWS
VMEM_LIMIT_BYTES = 56 * 1024 * 1024

MLSTM_CHUNK = 128
DEC_PAD = SUBLANES
NEG_INF = float("-inf")

_HIGHEST = lax.Precision.HIGHEST
_NT = (((1,), (1,)), ((), ()))
_TN = (((0,), (0,)), ((), ()))


def _params(*sem):
    return pltpu.CompilerParams(dimension_semantics=sem, vmem_limit_bytes=VMEM_LIMIT_BYTES)


def _const_spec(shape):
    nd = len(shape)
    return pl.BlockSpec(shape, lambda *_: (0,) * nd, pipeline_mode=pl.Buffered(1))


def _rms(x, g):
    return x * lax.rsqrt(jnp.mean(x * x, axis=-1, keepdims=True) + EPS) * g


def _norm_proj_kernel(x_ref, g_ref, *refs, n_out):
    w_refs, o_refs = refs[:n_out], refs[n_out:]
    hb = _rms(x_ref[...], g_ref[...]).astype(BF16)
    for w_ref, o_ref in zip(w_refs, o_refs):
        o_ref[...] = jnp.dot(hb, w_ref[...], preferred_element_type=F32)


def _norm_proj(x, g, ws, tm):
    n, d = x.shape
    return pl.pallas_call(
        functools.partial(_norm_proj_kernel, n_out=len(ws)),
        grid=(n // tm,),
        in_specs=[pl.BlockSpec((tm, d), lambda i: (i, 0)), _const_spec((1, d))]
        + [_const_spec(w.shape) for w in ws],
        out_specs=[pl.BlockSpec((tm, w.shape[1]), lambda i: (i, 0)) for w in ws],
        out_shape=[jax.ShapeDtypeStruct((n, w.shape[1]), F32) for w in ws],
        compiler_params=_params("parallel"),
        name="norm_proj",
    )(x, g, *ws)


def _mlstm_kernel(q_ref, k_ref, v_ref, gt_ref, bias_ref, c0_ref, n0_ref, m0_ref,
                  hs_ref, c_ref, n_ref, m_ref, *pad_refs, rows, valid, chunk, single_chunk):
    nseq = q_ref.shape[0]
    ci = pl.program_id(1)
    if single_chunk:
        c_src, n_src, m_src = c0_ref, n0_ref, m0_ref
    else:
        c_src, n_src, m_src = c_ref, n_ref, m_ref

        @pl.when(ci == 0)
        def _():
            c_ref[...] = c0_ref[...]
            n_ref[...] = n0_ref[...]
            m_ref[...] = m0_ref[...]

    if rows != chunk:
        for p in pad_refs:
            p[...] = jnp.zeros(p.shape, p.dtype)

    L = chunk
    row_l = lax.broadcasted_iota(jnp.int32, (L, LANES), 0)
    lane_l = lax.broadcasted_iota(jnp.int32, (L, LANES), 1)
    is_ig = lane_l < M_HEADS
    real = row_l < valid
    eye8 = (lax.broadcasted_iota(jnp.int32, (SUBLANES, LANES), 0)
            == lax.broadcasted_iota(jnp.int32, (SUBLANES, LANES), 1)).astype(F32)
    col_real = lax.broadcasted_iota(jnp.int32, (SUBLANES, L), 1) < valid
    causal = lax.broadcasted_iota(jnp.int32, (L, L), 1) <= lax.broadcasted_iota(jnp.int32, (L, L), 0)
    lane_1 = lax.broadcasted_iota(jnp.int32, (1, LANES), 1)
    scale = M_QK_DIM ** -0.5

    tiles, gsels, b_colss, cols_l = [], [], [], []
    for sq in range(nseq):
        if rows == chunk:
            tiles.append((q_ref[sq], k_ref[sq], v_ref[sq]))
            gt_all = gt_ref[sq]
        else:
            qp, kp, vp, gp = pad_refs
            qp[sq, 0:rows, :], kp[sq, 0:rows, :], vp[sq, 0:rows, :], gp[sq, 0:rows, :] = (
                q_ref[sq], k_ref[sq], v_ref[sq], gt_ref[sq])
            tiles.append((qp[sq], kp[sq], vp[sq]))
            gt_all = gp[sq]
        gc = gt_all + bias_ref[...]
        lfc = jnp.where(real, jax.nn.log_sigmoid(gc), 0.0)
        igc = jnp.where(real, gc, NEG_INF)
        gsels.append(jnp.where(is_ig, igc, lfc))
        b_cols = lfc
        shift = 1
        while shift < L:
            b_cols = b_cols + jnp.where(row_l >= shift, pltpu.roll(b_cols, shift, 0), 0.0)
            shift *= 2
        b_colss.append(b_cols)
        cols_l.append(jnp.where(is_ig, jnp.where(real, gc, 0.0), b_cols))
    rows_ts = [lax.dot_general(eye8, c, _NT, precision=_HIGHEST, preferred_element_type=F32) for c in cols_l]

    ch = [(sq, h) for sq in range(nseq) for h in range(M_HEADS)]
    qs = [tiles[sq][0][0:rows, h * M_QK_DIM:(h + 1) * M_QK_DIM] * scale for sq, h in ch]
    ks = [tiles[sq][1][:, h * M_QK_DIM:(h + 1) * M_QK_DIM] for sq, h in ch]
    qbs = [q.astype(BF16) for q in qs]
    kbs = [k.astype(BF16) for k in ks]
    vbs = [tiles[sq][2][:, h * M_V_DIM:(h + 1) * M_V_DIM].astype(BF16) for sq, h in ch]
    c_olds = [c_src[sq, h] for sq, h in ch]
    n_olds = [n_src[sq, h] for sq, h in ch]
    b_cs = [b_colss[sq][:, M_HEADS + h:M_HEADS + h + 1] for sq, h in ch]
    m_prevs = [m_src[sq][:, h:h + 1] for sq, h in ch]

    qk = [lax.dot_general(qb, kb, _NT, preferred_element_type=F32) for qb, kb in zip(qbs, kbs)]
    qc = [jnp.dot(qb, c.astype(BF16), preferred_element_type=F32) for qb, c in zip(qbs, c_olds)]

    twice = lambda t: jnp.concatenate([t, t], axis=1)
    a_s, m_ts, w_inters, b_reps = [], [], [], []
    for i, (sq, h) in enumerate(ch):
        b_rep = jnp.broadcast_to(b_cs[i], (L, LANES))
        b_q = b_rep[0:rows, :]
        ig_row = jnp.where(col_real, rows_ts[sq], NEG_INF)[h:h + 1, :]
        d = jnp.where(causal[0:rows, :], b_q[:, 0:L] - rows_ts[sq][M_HEADS + h:M_HEADS + h + 1, :] + ig_row, NEG_INF)
        m_inter = b_q + m_prevs[i]
        m_t = jnp.maximum(m_inter, jnp.max(d, axis=-1, keepdims=True))
        a_s.append(qk[i] * jnp.exp(d - m_t[:, 0:L]))
        m_ts.append(m_t)
        w_inters.append(jnp.exp(m_inter - m_t))
        b_reps.append(b_rep)
    av = [jnp.dot(a.astype(BF16), vb, preferred_element_type=F32) for a, vb in zip(a_s, vbs)]

    kws, w_states = [], []
    m_new_rows = [m_src[sq] for sq in range(nseq)]
    for i, (sq, h) in enumerate(ch):
        num = twice(w_inters[i]) * qc[i] + av[i]
        den = (w_inters[i] * jnp.sum(qs[i] * n_olds[i], axis=-1, keepdims=True)
               + jnp.sum(a_s[i], axis=-1, keepdims=True))
        hval = num / twice(jnp.maximum(jnp.abs(den), jnp.exp(-m_ts[i])))
        hs_ref[sq, :, h * M_V_DIM:(h + 1) * M_V_DIM] = hval
        m_new = m_ts[i][rows - 1:rows, :]
        b_last = b_reps[i][rows - 1:rows, :]
        w_states.append(jnp.exp(b_last + m_prevs[i] - m_new))
        ig_rep = jnp.broadcast_to(gsels[sq][:, h:h + 1], (L, LANES))
        kws.append(ks[i] * jnp.exp(b_last - b_reps[i] + ig_rep - m_new))
        m_new_rows[sq] = jnp.where(lane_1 == h, m_new, m_new_rows[sq])
    kv = [lax.dot_general(kw.astype(BF16), vb, _TN, preferred_element_type=F32) for kw, vb in zip(kws, vbs)]
    for i, (sq, h) in enumerate(ch):
        c_ref[sq, h] = twice(w_states[i]) * c_olds[i] + kv[i]
        n_ref[sq, h] = w_states[i] * n_olds[i] + jnp.sum(kws[i], axis=0, keepdims=True)
    for sq in range(nseq):
        m_ref[sq] = m_new_rows[sq]


MLSTM_SEQS_PER_STEP = {True: 4, False: 8}


def _mlstm(q, k, v, gates, bias, c0, n0, m0, valid):
    b, s, _ = q.shape
    assert MLSTM_CHUNK <= LANES
    nseq = math.gcd(b, MLSTM_SEQS_PER_STEP[s >= MLSTM_CHUNK])
    if s >= MLSTM_CHUNK:
        rows, nc, chunk = MLSTM_CHUNK, s // MLSTM_CHUNK, MLSTM_CHUNK
        assert valid == s and s % MLSTM_CHUNK == 0
        scratch = []
    else:
        rows, nc, chunk = s, 1, -(-s // BF16_ROWS) * BF16_ROWS
        scratch = [pltpu.VMEM((nseq, chunk, w), F32) for w in (M_QK_W, M_QK_W, M_V_W, LANES)]
    tok = lambda w: pl.BlockSpec((nseq, rows, w), lambda bi, ci: (bi, ci, 0))
    st_c = pl.BlockSpec((nseq, M_HEADS, M_QK_DIM, M_V_DIM), lambda bi, ci: (bi, 0, 0, 0))
    st_n = pl.BlockSpec((nseq, M_HEADS, 1, M_QK_DIM), lambda bi, ci: (bi, 0, 0, 0))
    st_m = pl.BlockSpec((nseq, 1, LANES), lambda bi, ci: (bi, 0, 0))
    return pl.pallas_call(
        functools.partial(_mlstm_kernel, rows=rows, valid=valid, chunk=chunk, single_chunk=nc == 1),
        grid=(b // nseq, nc),
        in_specs=[tok(M_QK_W), tok(M_QK_W), tok(M_V_W), tok(LANES), _const_spec((1, LANES)), st_c, st_n, st_m],
        out_specs=[tok(M_V_W), st_c, st_n, st_m],
        out_shape=[jax.ShapeDtypeStruct((b, s, M_V_W), F32), jax.ShapeDtypeStruct(c0.shape, F32),
                   jax.ShapeDtypeStruct(n0.shape, F32), jax.ShapeDtypeStruct(m0.shape, F32)],
        scratch_shapes=scratch,
        compiler_params=_params("parallel", "arbitrary"),
        name="mlstm_chunks",
    )(q, k, v, gates, bias, c0, n0, m0)


FFN_CHUNKS = 11


def _ffn_residual(x, g_ref, wg_ref, wu_ref, wd_ref):
    hb = _rms(x, g_ref[...]).astype(BF16)
    acc = x
    cw = FFN_HIDDEN // FFN_CHUNKS
    for c in range(FFN_CHUNKS):
        sl = slice(c * cw, (c + 1) * cw)
        gg = jnp.dot(hb, wg_ref[:, sl], preferred_element_type=F32)
        uu = jnp.dot(hb, wu_ref[:, sl], preferred_element_type=F32)
        act = (jax.nn.silu(gg) * uu).astype(BF16)
        acc = acc + jnp.dot(act, wd_ref[sl, :], preferred_element_type=F32)
    return acc


def _mlstm_out_ffn_kernel(hs_ref, o_ref, x_ref, gh_ref, w_ref, g_ref, wg_ref, wu_ref, wd_ref, out_ref):
    parts = []
    for h in range(M_HEADS):
        sl = slice(h * M_V_DIM, (h + 1) * M_V_DIM)
        parts.append(_rms(hs_ref[:, sl], gh_ref[:, sl]))
    y = jnp.concatenate(parts, axis=-1) * jax.nn.sigmoid(o_ref[...])
    x1 = x_ref[...] + jnp.dot(y.astype(BF16), w_ref[...], preferred_element_type=F32)
    out_ref[...] = _ffn_residual(x1, g_ref, wg_ref, wu_ref, wd_ref)


def _proj_ffn_kernel(a_ref, x_ref, w_ref, g_ref, wg_ref, wu_ref, wd_ref, out_ref):
    x1 = x_ref[...] + jnp.dot(a_ref[...].astype(BF16), w_ref[...], preferred_element_type=F32)
    out_ref[...] = _ffn_residual(x1, g_ref, wg_ref, wu_ref, wd_ref)


def _mixer_out_ffn(acts, x, consts, tm, mlstm):
    n, d = x.shape
    row = lambda a: pl.BlockSpec((tm, a.shape[1]), lambda i: (i, 0))
    return pl.pallas_call(
        _mlstm_out_ffn_kernel if mlstm else _proj_ffn_kernel,
        grid=(n // tm,),
        in_specs=[row(a) for a in acts] + [row(x)] + [_const_spec(c.shape) for c in consts],
        out_specs=row(x),
        out_shape=jax.ShapeDtypeStruct((n, d), F32),
        compiler_params=_params("parallel"),
        name="mlstm_out_ffn" if mlstm else "attn_out_ffn",
    )(*acts, x, *consts)


def _qkv_kernel(x_ref, g_ref, wq_ref, wk_ref, wv_ref, gq_ref, gk_ref, cos_ref, sin_ref,
                q_ref, k_ref, v_ref, *blk_refs):
    hb = _rms(x_ref[...], g_ref[...]).astype(BF16)
    cos, sin = cos_ref[...], sin_ref[...]

    kb_ref, vt_ref, km_ref = blk_refs if blk_refs else (None, None, None)
    tm = x_ref.shape[0]
    rt = min(tm, MOBA_BLOCK)
    cw = 2 * A_HEAD_DIM

    def norm_rope(t, gh_ref, rows):
        y = _rms(t, gh_ref[...])
        return y * cos[rows, :] + pltpu.roll(y, A_HEAD_DIM // 2, 1) * sin[rows, :]

    for j in range(tm // rt):
        rows = slice(j * rt, (j + 1) * rt)
        hr = hb[rows, :]
        for c in range(A_HEADS // 2):
            t = jnp.dot(hr, wq_ref[:, c * cw:(c + 1) * cw], preferred_element_type=F32)
            for i in range(2):
                sl = slice((2 * c + i) * A_HEAD_DIM, (2 * c + i + 1) * A_HEAD_DIM)
                q_ref[rows, sl] = norm_rope(t[:, i * A_HEAD_DIM:(i + 1) * A_HEAD_DIM], gq_ref, rows)
        for c in range(A_HEADS // 2):
            t = jnp.dot(hr, wk_ref[:, c * cw:(c + 1) * cw], preferred_element_type=F32)
            for i in range(2):
                sl = slice((2 * c + i) * A_HEAD_DIM, (2 * c + i + 1) * A_HEAD_DIM)
                k = norm_rope(t[:, i * A_HEAD_DIM:(i + 1) * A_HEAD_DIM], gk_ref, rows)
                k_ref[rows, sl] = k
                if blk_refs:
                    kb_ref[rows, sl] = k.astype(BF16)
                    mean = jnp.mean(k, axis=0, keepdims=True)
                    km_ref[j, :, sl] = jnp.broadcast_to(mean, (SUBLANES, A_HEAD_DIM))
        for c in range(A_HEADS // 2):
            t = jnp.dot(hr, wv_ref[:, c * cw:(c + 1) * cw], preferred_element_type=F32)
            v_ref[rows, c * cw:(c + 1) * cw] = t
            if blk_refs:
                for i in range(2):
                    r0 = (2 * c + i) * VT_ROWS
                    vt_ref[j, r0:r0 + A_HEAD_DIM, :] = t[:, i * A_HEAD_DIM:(i + 1) * A_HEAD_DIM].T.astype(BF16)
                    vt_ref[j, r0 + A_HEAD_DIM:r0 + VT_ROWS, :] = jnp.ones((BF16_ROWS, MOBA_BLOCK), BF16)


def _qkv(x, g, wq, wk, wv, gq, gk, cos, sin, tm, with_blocks):
    n, d = x.shape
    e = wq.shape[1]
    n_tab = cos.shape[0] // tm
    row = lambda wd: pl.BlockSpec((tm, wd), lambda i: (i, 0))
    tab = pl.BlockSpec((tm, A_HEAD_DIM), lambda i: (i % n_tab, 0))
    out_specs = [row(e), row(e), row(e)]
    out_shape = [jax.ShapeDtypeStruct((n, e), F32)] * 3
    if with_blocks:
        assert tm % MOBA_BLOCK == 0
        gpt = tm // MOBA_BLOCK
        vrows = A_HEADS * VT_ROWS
        out_specs += [row(e), pl.BlockSpec((gpt, vrows, MOBA_BLOCK), lambda i: (i, 0, 0)),
                      pl.BlockSpec((gpt, SUBLANES, e), lambda i: (i, 0, 0))]
        out_shape += [jax.ShapeDtypeStruct((n, e), BF16),
                      jax.ShapeDtypeStruct((n // MOBA_BLOCK, vrows, MOBA_BLOCK), BF16),
                      jax.ShapeDtypeStruct((n // MOBA_BLOCK, SUBLANES, e), F32)]
    return pl.pallas_call(
        _qkv_kernel,
        grid=(n // tm,),
        in_specs=[row(d), _const_spec((1, d)), _const_spec(wq.shape), _const_spec(wk.shape),
                  _const_spec(wv.shape), _const_spec((1, A_HEAD_DIM)), _const_spec((1, A_HEAD_DIM)), tab, tab],
        out_specs=out_specs,
        out_shape=out_shape,
        compiler_params=_params("parallel"),
        name="qkv_rope",
    )(x, g, wq, wk, wv, gq, gk, cos, sin)


def _rope_tables(pos):
    half = A_HEAD_DIM // 2
    inv_freq = ROPE_THETA ** (-jnp.arange(half, dtype=F32) / half)
    ang = pos.astype(F32)[:, None] * inv_freq[None, :]
    c, s = jnp.cos(ang), jnp.sin(ang)
    return jnp.concatenate([c, c], axis=-1), jnp.concatenate([-s, s], axis=-1)


MOBA_QUERY_BLOCKS = 2
MOBA_HEADS_PER_STEP = 4
MASK_BIG = 2.0 ** 100


def _top_blocks_t(scores_t, allowed):
    row = lax.broadcasted_iota(jnp.int32, scores_t.shape, 0)
    s = jnp.where(allowed, scores_t, NEG_INF)
    picked = jnp.zeros(scores_t.shape, F32)
    for _ in range(MOBA_TOPK):
        mx = jnp.max(s, axis=0, keepdims=True)
        cand = jnp.where(jnp.logical_and(s == mx, s > NEG_INF), row, scores_t.shape[0])
        hit = row == jnp.min(cand, axis=0, keepdims=True)
        picked = jnp.where(hit, 1.0, picked)
        s = jnp.where(hit, NEG_INF, s)
    return picked


def _moba_prompt_kernel(q_ref, kb_ref, vt_ref, km_ref, o_ref, sa_ref, sb_ref, *, n_blocks, hp):
    tile = pl.program_id(2)
    blk = MOBA_BLOCK
    qb_n = MOBA_QUERY_BLOCKS
    qw = qb_n * blk
    first = tile * qb_n
    scale = A_HEAD_DIM ** -0.5
    nbp = -(-n_blocks // BF16_ROWS) * BF16_ROWS
    lane = lax.broadcasted_iota(jnp.int32, (blk, LANES), 1)
    blk_i = lax.broadcasted_iota(jnp.int32, (nbp, qw), 0)
    own_q = first + lax.broadcasted_iota(jnp.int32, (nbp, qw), 1) // blk
    heads = [slice(h * A_HEAD_DIM, (h + 1) * A_HEAD_DIM) for h in range(hp)]
    vrows = [slice(h * VT_ROWS, (h + 1) * VT_ROWS) for h in range(hp)]

    q_t = [q_ref[:, sl].T for sl in heads]
    q_tb = [(qt * scale).astype(BF16) for qt in q_t]
    own_logits = [[jnp.dot(kb_ref[pl.ds(pl.multiple_of((first + d) * blk, blk), blk), sl], qb,
                           preferred_element_type=F32) for sl, qb in zip(heads, q_tb)]
                  for d in range(qb_n)]
    kms = []
    for sl in heads:
        km = km_ref[:, sl]
        if nbp > n_blocks:
            km = jnp.concatenate([km, jnp.zeros((nbp - n_blocks, A_HEAD_DIM), F32)], axis=0)
        kms.append(km)
    scores = [jnp.dot(km, qt, precision=_HIGHEST, preferred_element_type=F32) for km, qt in zip(kms, q_t)]
    picks = [_top_blocks_t(sc, blk_i < own_q) for sc in scores]
    q_ext = []
    for qb, picked in zip(q_tb, picks):
        pad = jnp.zeros((LANES - nbp, qw), BF16)
        q_ext.append(jnp.concatenate([qb, (1.0 - picked).astype(BF16), pad], axis=0))

    def stage_logits(ref, j):
        st = pl.multiple_of(j * blk, blk)
        mask_cols = jnp.where(lane == j, -MASK_BIG, 0.0).astype(BF16)
        for h, (sl, qe) in enumerate(zip(heads, q_ext)):
            k_ext = jnp.concatenate([kb_ref[pl.ds(st, blk), sl], mask_cols], axis=1)
            ref[h] = jnp.dot(k_ext, qe, preferred_element_type=F32)

    stage_logits(sb_ref, 0)

    def accumulate(j, logits, state):
        probs, stats = [], []
        for s, (m, _) in zip(logits, state):
            m_new = jnp.maximum(m, jnp.max(s, axis=0, keepdims=True))
            probs.append(jnp.exp(s - m_new).astype(BF16))
            stats.append((m_new, jnp.exp(m - m_new)))
        return tuple(
            (m_new, alpha * acc + jnp.dot(vt_ref[j, vr, :], pb, preferred_element_type=F32))
            for vr, pb, (m_new, alpha), (_, acc) in zip(vrows, probs, stats, state))

    def accumulate_staged(j, ref, state):
        return accumulate(j, tuple(ref[h] for h in range(hp)), state)

    key_i = lax.broadcasted_iota(jnp.int32, (blk, qw), 0)
    qry_i = lax.broadcasted_iota(jnp.int32, (blk, qw), 1)
    state = tuple((jnp.full((1, qw), -MASK_BIG, F32), jnp.zeros((VT_ROWS, qw), F32)) for _ in heads)
    for d in range(qb_n):
        masked = []
        for s, picked in zip(own_logits[d], picks):
            sel = jnp.sum(jnp.where(blk_i == first + d, picked, 0.0), axis=0, keepdims=True)
            later = jnp.logical_and(qry_i >= (d + 1) * blk, jnp.broadcast_to(sel, (blk, qw)) > 0.5)
            causal = jnp.logical_and(qry_i >= d * blk,
                                     jnp.logical_and(qry_i < (d + 1) * blk, key_i <= qry_i - d * blk))
            masked.append(jnp.where(jnp.logical_or(causal, later), s, -MASK_BIG))
        state = accumulate(first + d, masked, state)

    def pair(i, state):
        j = 2 * i
        stage_logits(sa_ref, j + 1)
        state = accumulate_staged(j, sb_ref, state)
        stage_logits(sb_ref, j + 2)
        return accumulate_staged(j + 1, sa_ref, state)

    state = lax.fori_loop(0, tile, pair, state)
    for sl, (_, acc) in zip(heads, state):
        o_ref[:, sl] = (acc[0:A_HEAD_DIM, :] / acc[A_HEAD_DIM:A_HEAD_DIM + 1, :]).T


def _moba_prompt(q, kb, vt, kmean):
    b, s, e = q.shape
    nb = s // MOBA_BLOCK
    hp = MOBA_HEADS_PER_STEP
    qw = MOBA_QUERY_BLOCKS * MOBA_BLOCK
    assert s % qw == 0 and nb <= LANES and A_HEADS % hp == 0 and MOBA_QUERY_BLOCKS == 2
    w = hp * A_HEAD_DIM
    tile = pl.BlockSpec((None, qw, w), lambda bi, h, qi: (bi, qi, h))
    return pl.pallas_call(
        functools.partial(_moba_prompt_kernel, n_blocks=nb, hp=hp),
        grid=(b, A_HEADS // hp, s // qw),
        in_specs=[tile, pl.BlockSpec((None, s, w), lambda bi, h, qi: (bi, 0, h)),
                  pl.BlockSpec((None, nb, hp * VT_ROWS, MOBA_BLOCK), lambda bi, h, qi: (bi, 0, h, 0)),
                  pl.BlockSpec((None, nb, w), lambda bi, h, qi: (bi, 0, h))],
        out_specs=tile,
        out_shape=jax.ShapeDtypeStruct((b, s, e), F32),
        scratch_shapes=[pltpu.VMEM((hp, MOBA_BLOCK, qw), F32)] * 2,
        compiler_params=_params("parallel", "parallel", "arbitrary"),
        name="moba_prompt",
    )(q, kb, vt, kmean)


def _moba_decode_kernel(pt_ref, q_ref, kn_ref, vn_ref, ck_ref, cv_ref, o_ref, kbuf, vbuf, sem, logit_ref,
                        *, n_pages, page_rows):
    bi = pl.program_id(0)
    n_seq = pl.num_programs(0)
    slot = lax.rem(bi, 2)
    R = A_HEADS * DEC_PAD
    scale = A_HEAD_DIM ** -0.5
    pages_per_block = MOBA_BLOCK // page_rows
    n_blocks = n_pages // pages_per_block

    def page_copy(cache_ref, buf_ref, which, seq, sl, g):
        page = pt_ref[seq * n_pages + g]
        return pltpu.make_async_copy(cache_ref.at[page], buf_ref.at[sl, g], sem.at[sl, which])

    def start_pages(seq, sl):
        for g in range(n_pages):
            page_copy(ck_ref, kbuf, 0, seq, sl, g).start()
        for g in range(n_pages):
            page_copy(cv_ref, vbuf, 1, seq, sl, g).start()

    @pl.when(bi == 0)
    def _():
        start_pages(bi, slot)

    @pl.when(bi + 1 < n_seq)
    def _():
        start_pages(bi + 1, 1 - slot)

    def rows_by_head(ref):
        return jnp.concatenate([ref[:, h * A_HEAD_DIM:(h + 1) * A_HEAD_DIM] for h in range(A_HEADS)], axis=0)

    def by_head(ref):
        return jnp.concatenate([ref[pl.ds(h, page_rows, stride=A_HEADS), :].astype(BF16) for h in range(A_HEADS)],
                               axis=0)

    q = rows_by_head(q_ref)
    qb = q.astype(BF16)

    for g in range(n_pages):
        page_copy(ck_ref, kbuf, 0, bi, slot, g).wait()
    ksum = []
    for g in range(n_pages):
        kpage = kbuf.at[slot, g]
        lg = lax.dot_general(qb, by_head(kpage), _NT, preferred_element_type=F32)
        for h in range(A_HEADS):
            hr = slice(h * DEC_PAD, (h + 1) * DEC_PAD)
            logit_ref[g, hr, :] = lg[hr, h * page_rows:(h + 1) * page_rows] * scale
        part = jnp.sum(kpage[...].reshape(page_rows, A_HEADS, A_HEAD_DIM), axis=0)
        if g % pages_per_block == 0:
            ksum.append(part)
        else:
            ksum[-1] = ksum[-1] + part

    kmean = jnp.concatenate(ksum, axis=0) * (1.0 / MOBA_BLOCK)
    nbh = n_blocks * A_HEADS
    scores_t = lax.dot_general(kmean, q, _NT, precision=_HIGHEST, preferred_element_type=F32)
    brow = lax.broadcasted_iota(jnp.int32, (nbh, R), 0)
    qcol = lax.broadcasted_iota(jnp.int32, (nbh, R), 1)
    picked_t = _top_blocks_t(scores_t, brow % A_HEADS == qcol // DEC_PAD)
    erow = lax.broadcasted_iota(jnp.int32, (nbh, n_blocks * page_rows), 0)
    ecol = lax.broadcasted_iota(jnp.int32, (nbh, n_blocks * page_rows), 1)
    expand = jnp.where(erow // A_HEADS == ecol // page_rows, 1.0, 0.0).astype(BF16)
    keep = lax.dot_general(picked_t.astype(BF16), expand, _TN, preferred_element_type=F32)
    keep_blk = [keep[:, j * page_rows:(j + 1) * page_rows] > 0.5 for j in range(n_blocks)]

    kn = rows_by_head(kn_ref).astype(BF16)
    ln = lax.dot_general(qb, kn, _NT, preferred_element_type=F32) * scale
    rn = lax.broadcasted_iota(jnp.int32, (R, R), 0)
    cn = lax.broadcasted_iota(jnp.int32, (R, R), 1)
    ok_n = jnp.logical_and(rn // DEC_PAD == cn // DEC_PAD, cn % DEC_PAD <= rn % DEC_PAD)
    ln = jnp.where(ok_n, ln, NEG_INF)

    masked = [jnp.where(keep_blk[g // pages_per_block], logit_ref[g], NEG_INF) for g in range(n_pages)]
    mm = masked[0]
    for g in range(1, n_pages):
        mm = jnp.maximum(mm, masked[g])
    m = jnp.maximum(jnp.max(ln, axis=-1, keepdims=True), jnp.max(mm, axis=-1, keepdims=True))
    pn = jnp.exp(ln - m)
    probs = [jnp.exp(s - m) for s in masked]
    lsum = probs[0]
    for g in range(1, n_pages):
        lsum = lsum + probs[g]
    l = jnp.sum(pn, axis=-1, keepdims=True) + jnp.sum(lsum, axis=-1, keepdims=True)

    for g in range(n_pages):
        page_copy(cv_ref, vbuf, 1, bi, slot, g).wait()
    acc = jnp.dot(pn.astype(BF16), rows_by_head(vn_ref).astype(BF16), preferred_element_type=F32)
    zero = jnp.zeros((DEC_PAD, page_rows), F32)
    for g in range(n_pages):
        pe = probs[g]
        p_bd = jnp.concatenate(
            [jnp.concatenate([pe[h * DEC_PAD:(h + 1) * DEC_PAD, :] if hh == h else zero for hh in range(A_HEADS)],
                             axis=1) for h in range(A_HEADS)], axis=0)
        acc = acc + jnp.dot(p_bd.astype(BF16), by_head(vbuf.at[slot, g]), preferred_element_type=F32)
    res = acc / l
    for h in range(A_HEADS):
        o_ref[:, h * A_HEAD_DIM:(h + 1) * A_HEAD_DIM] = res[h * DEC_PAD:(h + 1) * DEC_PAD, :]


def _moba_decode(q, kn, vn, cache_k, cache_v, page_table):
    b, t, e = q.shape
    n_phys, page_rows = cache_k.shape[0], cache_k.shape[1]
    n_pages = page_table.shape[1]
    assert t == DEC_PAD and MOBA_BLOCK % page_rows == 0 and n_pages % (MOBA_BLOCK // page_rows) == 0
    cols = page_rows * A_HEADS
    ck = cache_k.reshape(n_phys, cols, A_HEAD_DIM)
    cv = cache_v.reshape(n_phys, cols, A_HEAD_DIM)
    pt = page_table.reshape(-1).astype(jnp.int32)

    tok = pl.BlockSpec((None, t, e), lambda bi, pt_: (bi, 0, 0))
    hbm = pl.BlockSpec(memory_space=pl.ANY)
    grid_spec = pltpu.PrefetchScalarGridSpec(
        num_scalar_prefetch=1,
        grid=(b,),
        in_specs=[tok, tok, tok, hbm, hbm],
        out_specs=tok,
        scratch_shapes=[pltpu.VMEM((2, n_pages, cols, A_HEAD_DIM), F32),
                        pltpu.VMEM((2, n_pages, cols, A_HEAD_DIM), F32),
                        pltpu.SemaphoreType.DMA((2, 2)),
                        pltpu.VMEM((n_pages, A_HEADS * DEC_PAD, page_rows), F32)],
    )
    return pl.pallas_call(
        functools.partial(_moba_decode_kernel, n_pages=n_pages, page_rows=page_rows),
        grid_spec=grid_spec,
        out_shape=jax.ShapeDtypeStruct((b, t, e), F32),
        compiler_params=_params("arbitrary"),
        name="moba_decode",
    )(pt, q, kn, vn, ck, cv)


ROW_TILE = 512


def _trunk(x, pos0, c0, n0, m0, past, w):
    b, s, d = x.shape
    n = b * s
    sp = -(-s // DEC_PAD) * DEC_PAD
    tm = min(ROW_TILE, n // 2)
    assert n % tm == 0 and (tm % s == 0 or s % tm == 0)
    xf = x.reshape(n, d)
    r3 = lambda t: jnp.pad(t.reshape(b, s, t.shape[-1]), ((0, 0), (0, sp - s), (0, 0)))
    flat = lambda t: t[:, :s].reshape(n, t.shape[-1])

    q, k, v, o, gates = _norm_proj(xf, w["norm_mix"][0], [w["wq_m"], w["wk_m"], w["wv_m"], w["wo_m"], w["wgate_m"]], tm)
    hs, c_new, n_new, m_new = _mlstm(r3(q), r3(k), r3(v), r3(gates), w["b_gate"], c0, n0, m0, s)
    xf = _mixer_out_ffn((flat(hs), o), xf, (w["norm_h_m"], w["w_out_m"], w["norm_ffn"][0], w["wg"][0],
                                            w["wu"][0], w["wd"][0]), tm, mlstm=True)

    pos = pos0 + jnp.arange(s, dtype=jnp.int32)
    cos, sin = _rope_tables(pos)
    if s < tm:
        cos, sin = jnp.tile(cos, (tm // s, 1)), jnp.tile(sin, (tm // s, 1))
    qa, ka, va, *blocks = _qkv(xf, w["norm_mix"][1], w["wq_a"], w["wk_a"], w["wv_a"], w["norm_q_a"],
                               w["norm_k_a"], cos, sin, tm, with_blocks=past is None)
    if past is None:
        kb, vt, kmean = blocks
        nb = s // MOBA_BLOCK
        att = _moba_prompt(r3(qa), r3(kb), vt.reshape(b, nb, -1, MOBA_BLOCK), kmean[:, 0, :].reshape(b, nb, -1))
    else:
        att = _moba_decode(r3(qa), r3(ka), r3(va), *past)
    xf = _mixer_out_ffn((flat(att),), xf, (w["w_out_a"], w["norm_ffn"][1], w["wg"][1], w["wu"][1], w["wd"][1]),
                        tm, mlstm=False)

    heads = lambda t: t.reshape(b, s, A_HEADS, A_HEAD_DIM)
    return xf.reshape(b, s, d), heads(ka), heads(va), c_new, n_new, m_new


def kernel(x_prompt, x_sample, state_C, state_n, state_m, cache_k, cache_v, page_table, norm_mix, norm_ffn,
           w_in_m, b_gate_m, norm_h_m, w_out_m, w_qkv_a, norm_q_a, norm_k_a, w_out_a, w_gu, w_down):
    assert norm_mix.shape[0] == 2 and w_in_m.shape[0] == 1 and w_qkv_a.shape[0] == 1
    bf = lambda t: t.astype(BF16)
    row = lambda t: t.reshape(1, -1).astype(F32)
    w_in = w_in_m[0]
    e_a = A_HEADS * A_HEAD_DIM
    gate_w = jnp.pad(w_in[:, 2 * M_QK_W + 2 * M_V_W:], ((0, 0), (0, LANES - 2 * M_HEADS)))
    w = {
        "norm_mix": [row(norm_mix[0]), row(norm_mix[1])],
        "norm_ffn": [row(norm_ffn[0]), row(norm_ffn[1])],
        "wq_m": bf(w_in[:, :M_QK_W]), "wk_m": bf(w_in[:, M_QK_W:2 * M_QK_W]),
        "wv_m": bf(w_in[:, 2 * M_QK_W:2 * M_QK_W + M_V_W]),
        "wo_m": bf(w_in[:, 2 * M_QK_W + M_V_W:2 * M_QK_W + 2 * M_V_W]),
        "wgate_m": bf(gate_w),
        "b_gate": jnp.pad(b_gate_m[0].astype(F32), (0, LANES - 2 * M_HEADS)).reshape(1, LANES),
        "norm_h_m": row(norm_h_m[0]), "w_out_m": bf(w_out_m[0]),
        "wq_a": bf(w_qkv_a[0][:, :e_a]), "wk_a": bf(w_qkv_a[0][:, e_a:2 * e_a]), "wv_a": bf(w_qkv_a[0][:, 2 * e_a:]),
        "norm_q_a": row(norm_q_a[0]), "norm_k_a": row(norm_k_a[0]), "w_out_a": bf(w_out_a[0]),
        "wg": [bf(w_gu[i][:, :FFN_HIDDEN]) for i in range(2)],
        "wu": [bf(w_gu[i][:, FFN_HIDDEN:]) for i in range(2)],
        "wd": [bf(w_down[i]) for i in range(2)],
    }

    def state(c, nn, mm):
        bsz = c.shape[0]
        m_row = jnp.pad(mm.astype(F32), ((0, 0), (0, LANES - M_HEADS))).reshape(bsz, 1, LANES)
        return c.astype(F32), nn.astype(F32).reshape(bsz, M_HEADS, 1, M_QK_DIM), m_row

    def unstate(c, nn, mm, dt):
        bsz = c.shape[0]
        return (c[None].astype(dt), nn.reshape(1, bsz, M_HEADS, M_QK_DIM).astype(dt),
                mm.reshape(bsz, LANES)[None, :, :M_HEADS].astype(dt))

    sd = state_C.dtype
    bp, sp, _ = x_prompt.shape
    zc = jnp.zeros((bp,) + state_C.shape[2:], F32)
    zn = jnp.zeros((bp,) + state_n.shape[2:], F32)
    zm = jnp.zeros((bp,) + state_m.shape[2:], F32)
    yp, pk, pv, pc, pn, pm = _trunk(x_prompt, 0, *state(zc, zn, zm), None, w)

    past_len = page_table.shape[1] * cache_k.shape[2]
    ys, sk, sv, sc, sn, sm = _trunk(x_sample, past_len, *state(state_C[0], state_n[0], state_m[0]),
                                    (cache_k[0], cache_v[0], page_table), w)

    return (yp, ys, pk[None], pv[None], *unstate(pc, pn, pm, sd), sk[None], sv[None], *unstate(sc, sn, sm, sd))
```

```python
import functools
import math

import jax
import jax.numpy as jnp
from jax import lax
from jax.experimental import pallas as pl
from jax.experimental.pallas import tpu as pltpu

F32 = jnp.float32
BF16 = jnp.bfloat16

D_MODEL = 1024
M_HEADS = 4
M_QK_DIM = 128
M_V_DIM = 256
M_QK_W = M_HEADS * M_QK_DIM
M_V_W = M_HEADS * M_V_DIM
A_HEADS = 8
A_HEAD_DIM = 128
MOBA_BLOCK = 256
MOBA_TOPK = 3
ROPE_THETA = 10000.0
FFN_HIDDEN = 2816
EPS = 1e-6

LANES = 128
SUBLANES = 8
BF16_ROWS = 16
VT_ROWS = A_HEAD_DIM + BF16_ROWS
VMEM_LIMIT_BYTES = 56 * 1024 * 1024

MLSTM_CHUNK = 128
DEC_PAD = SUBLANES
NEG_INF = float("-inf")

_HIGHEST = lax.Precision.HIGHEST
_NT = (((1,), (1,)), ((), ()))
_TN = (((0,), (0,)), ((), ()))


def _params(*sem):
    return pltpu.CompilerParams(dimension_semantics=sem, vmem_limit_bytes=VMEM_LIMIT_BYTES)


def _const_spec(shape):
    nd = len(shape)
    return pl.BlockSpec(shape, lambda *_: (0,) * nd, pipeline_mode=pl.Buffered(1))


def _rms(x, g):
    return x * lax.rsqrt(jnp.mean(x * x, axis=-1, keepdims=True) + EPS) * g


def _norm_proj_kernel(x_ref, g_ref, *refs, n_out):
    w_refs, o_refs = refs[:n_out], refs[n_out:]
    hb = _rms(x_ref[...], g_ref[...]).astype(BF16)
    for w_ref, o_ref in zip(w_refs, o_refs):
        o_ref[...] = jnp.dot(hb, w_ref[...], preferred_element_type=F32)


def _norm_proj(x, g, ws, tm):
    n, d = x.shape
    return pl.pallas_call(
        functools.partial(_norm_proj_kernel, n_out=len(ws)),
        grid=(n // tm,),
        in_specs=[pl.BlockSpec((tm, d), lambda i: (i, 0)), _const_spec((1, d))]
        + [_const_spec(w.shape) for w in ws],
        out_specs=[pl.BlockSpec((tm, w.shape[1]), lambda i: (i, 0)) for w in ws],
        out_shape=[jax.ShapeDtypeStruct((n, w.shape[1]), F32) for w in ws],
        compiler_params=_params("parallel"),
        name="norm_proj",
    )(x, g, *ws)


def _mlstm_kernel(q_ref, k_ref, v_ref, gt_ref, bias_ref, c0_ref, n0_ref, m0_ref,
                  hs_ref, c_ref, n_ref, m_ref, *pad_refs, rows, valid, chunk, single_chunk):
    nseq = q_ref.shape[0]
    ci = pl.program_id(1)
    if single_chunk:
        c_src, n_src, m_src = c0_ref, n0_ref, m0_ref
    else:
        c_src, n_src, m_src = c_ref, n_ref, m_ref

        @pl.when(ci == 0)
        def _():
            c_ref[...] = c0_ref[...]
            n_ref[...] = n0_ref[...]
            m_ref[...] = m0_ref[...]

    if rows != chunk:
        @pl.when(jnp.logical_and(pl.program_id(0) == 0, ci == 0))
        def _():
            for p in pad_refs:
                p[...] = jnp.zeros(p.shape, p.dtype)

    L = chunk
    row_l = lax.broadcasted_iota(jnp.int32, (L, LANES), 0)
    lane_l = lax.broadcasted_iota(jnp.int32, (L, LANES), 1)
    is_ig = lane_l < M_HEADS
    real = row_l < valid
    eye8 = (lax.broadcasted_iota(jnp.int32, (SUBLANES, LANES), 0)
            == lax.broadcasted_iota(jnp.int32, (SUBLANES, LANES), 1)).astype(F32)
    col_real = lax.broadcasted_iota(jnp.int32, (SUBLANES, L), 1) < valid
    causal = lax.broadcasted_iota(jnp.int32, (L, L), 1) <= lax.broadcasted_iota(jnp.int32, (L, L), 0)
    lane_1 = lax.broadcasted_iota(jnp.int32, (1, LANES), 1)
    scale = M_QK_DIM ** -0.5

    tiles, gsels, b_colss, cols_l = [], [], [], []
    for sq in range(nseq):
        if rows == chunk:
            tiles.append((q_ref[sq], k_ref[sq], v_ref[sq]))
            gt_all = gt_ref[sq]
        else:
            qp, kp, vp, gp = pad_refs
            qp[sq, 0:rows, :], kp[sq, 0:rows, :], vp[sq, 0:rows, :], gp[sq, 0:rows, :] = (
                q_ref[sq], k_ref[sq], v_ref[sq], gt_ref[sq])
            tiles.append((qp[sq], kp[sq], vp[sq]))
            gt_all = gp[sq]
        gc = gt_all + bias_ref[...]
        lfc = jnp.where(real, jax.nn.log_sigmoid(gc), 0.0)
        igc = jnp.where(real, gc, NEG_INF)
        gsels.append(jnp.where(is_ig, igc, lfc))
        b_cols = lfc
        shift = 1
        while shift < L:
            b_cols = b_cols + jnp.where(row_l >= shift, pltpu.roll(b_cols, shift, 0), 0.0)
            shift *= 2
        b_colss.append(b_cols)
        cols_l.append(jnp.where(is_ig, jnp.where(real, gc, 0.0), b_cols))
    rows_ts = [lax.dot_general(eye8, c, _NT, precision=_HIGHEST, preferred_element_type=F32) for c in cols_l]

    ch = [(sq, h) for sq in range(nseq) for h in range(M_HEADS)]
    qs = [tiles[sq][0][0:rows, h * M_QK_DIM:(h + 1) * M_QK_DIM] * scale for sq, h in ch]
    ks = [tiles[sq][1][:, h * M_QK_DIM:(h + 1) * M_QK_DIM] for sq, h in ch]
    qbs = [q.astype(BF16) for q in qs]
    kbs = [k.astype(BF16) for k in ks]
    vbs = [tiles[sq][2][:, h * M_V_DIM:(h + 1) * M_V_DIM].astype(BF16) for sq, h in ch]
    c_olds = [c_src[sq, h] for sq, h in ch]
    n_olds = [n_src[sq, h] for sq, h in ch]
    b_cs = [b_colss[sq][:, M_HEADS + h:M_HEADS + h + 1] for sq, h in ch]
    m_prevs = [m_src[sq][:, h:h + 1] for sq, h in ch]

    qk = [lax.dot_general(qb, kb, _NT, preferred_element_type=F32) for qb, kb in zip(qbs, kbs)]
    qc = [jnp.dot(qb, c.astype(BF16), preferred_element_type=F32) for qb, c in zip(qbs, c_olds)]

    twice = lambda t: jnp.concatenate([t, t], axis=1)
    a_s, m_ts, w_inters, b_reps = [], [], [], []
    for i, (sq, h) in enumerate(ch):
        b_rep = jnp.broadcast_to(b_cs[i], (L, LANES))
        b_q = b_rep[0:rows, :]
        ig_row = jnp.where(col_real, rows_ts[sq], NEG_INF)[h:h + 1, :]
        d = jnp.where(causal[0:rows, :], b_q - rows_ts[sq][M_HEADS + h:M_HEADS + h + 1, :] + ig_row, NEG_INF)
        m_inter = b_q + m_prevs[i]
        m_t = jnp.maximum(m_inter, jnp.max(d, axis=-1, keepdims=True))
        a_s.append(qk[i] * jnp.exp(d - m_t))
        m_ts.append(m_t)
        w_inters.append(jnp.exp(m_inter - m_t))
        b_reps.append(b_rep)
    av = [jnp.dot(a.astype(BF16), vb, preferred_element_type=F32) for a, vb in zip(a_s, vbs)]

    kws, w_states = [], []
    m_new_rows = [m_src[sq] for sq in range(nseq)]
    for i, (sq, h) in enumerate(ch):
        num = twice(w_inters[i]) * qc[i] + av[i]
        den = (w_inters[i] * jnp.sum(qs[i] * n_olds[i], axis=-1, keepdims=True)
               + jnp.sum(a_s[i], axis=-1, keepdims=True))
        hval = num / twice(jnp.maximum(jnp.abs(den), jnp.exp(-m_ts[i])))
        hs_ref[sq, :, h * M_V_DIM:(h + 1) * M_V_DIM] = hval
        m_new = m_ts[i][rows - 1:rows, :]
        b_last = b_reps[i][rows - 1:rows, :]
        w_states.append(jnp.exp(b_last + m_prevs[i] - m_new))
        ig_rep = jnp.broadcast_to(gsels[sq][:, h:h + 1], (L, LANES))
        kws.append(ks[i] * jnp.exp(b_last - b_reps[i] + ig_rep - m_new))
        m_new_rows[sq] = jnp.where(lane_1 == h, m_new, m_new_rows[sq])
    kv = [lax.dot_general(kw.astype(BF16), vb, _TN, preferred_element_type=F32) for kw, vb in zip(kws, vbs)]
    for i, (sq, h) in enumerate(ch):
        c_ref[sq, h] = twice(w_states[i]) * c_olds[i] + kv[i]
        n_ref[sq, h] = w_states[i] * n_olds[i] + jnp.sum(kws[i], axis=0, keepdims=True)
    for sq in range(nseq):
        m_ref[sq] = m_new_rows[sq]


MLSTM_SEQS_PER_STEP = {True: 4, False: 8}


def _mlstm(q, k, v, gates, bias, c0, n0, m0, valid):
    b, s, _ = q.shape
    assert MLSTM_CHUNK == LANES
    nseq = math.gcd(b, MLSTM_SEQS_PER_STEP[s >= MLSTM_CHUNK])
    if s >= MLSTM_CHUNK:
        rows, nc = MLSTM_CHUNK, s // MLSTM_CHUNK
        assert valid == s and s % MLSTM_CHUNK == 0
        scratch = []
    else:
        rows, nc = s, 1
        scratch = [pltpu.VMEM((nseq, MLSTM_CHUNK, w), F32) for w in (M_QK_W, M_QK_W, M_V_W, LANES)]
    tok = lambda w: pl.BlockSpec((nseq, rows, w), lambda bi, ci: (bi, ci, 0))
    st_c = pl.BlockSpec((nseq, M_HEADS, M_QK_DIM, M_V_DIM), lambda bi, ci: (bi, 0, 0, 0))
    st_n = pl.BlockSpec((nseq, M_HEADS, 1, M_QK_DIM), lambda bi, ci: (bi, 0, 0, 0))
    st_m = pl.BlockSpec((nseq, 1, LANES), lambda bi, ci: (bi, 0, 0))
    return pl.pallas_call(
        functools.partial(_mlstm_kernel, rows=rows, valid=valid, chunk=MLSTM_CHUNK, single_chunk=nc == 1),
        grid=(b // nseq, nc),
        in_specs=[tok(M_QK_W), tok(M_QK_W), tok(M_V_W), tok(LANES), _const_spec((1, LANES)), st_c, st_n, st_m],
        out_specs=[tok(M_V_W), st_c, st_n, st_m],
        out_shape=[jax.ShapeDtypeStruct((b, s, M_V_W), F32), jax.ShapeDtypeStruct(c0.shape, F32),
                   jax.ShapeDtypeStruct(n0.shape, F32), jax.ShapeDtypeStruct(m0.shape, F32)],
        scratch_shapes=scratch,
        compiler_params=_params("arbitrary" if scratch else "parallel", "arbitrary"),
        name="mlstm_chunks",
    )(q, k, v, gates, bias, c0, n0, m0)


FFN_CHUNKS = 11


def _ffn_residual(x, g_ref, wg_ref, wu_ref, wd_ref):
    hb = _rms(x, g_ref[...]).astype(BF16)
    acc = x
    cw = FFN_HIDDEN // FFN_CHUNKS
    for c in range(FFN_CHUNKS):
        sl = slice(c * cw, (c + 1) * cw)
        gg = jnp.dot(hb, wg_ref[:, sl], preferred_element_type=F32)
        uu = jnp.dot(hb, wu_ref[:, sl], preferred_element_type=F32)
        act = (jax.nn.silu(gg) * uu).astype(BF16)
        acc = acc + jnp.dot(act, wd_ref[sl, :], preferred_element_type=F32)
    return acc


def _mlstm_out_ffn_kernel(hs_ref, o_ref, x_ref, gh_ref, w_ref, g_ref, wg_ref, wu_ref, wd_ref, out_ref):
    parts = []
    for h in range(M_HEADS):
        sl = slice(h * M_V_DIM, (h + 1) * M_V_DIM)
        parts.append(_rms(hs_ref[:, sl], gh_ref[:, sl]))
    y = jnp.concatenate(parts, axis=-1) * jax.nn.sigmoid(o_ref[...])
    x1 = x_ref[...] + jnp.dot(y.astype(BF16), w_ref[...], preferred_element_type=F32)
    out_ref[...] = _ffn_residual(x1, g_ref, wg_ref, wu_ref, wd_ref)


def _proj_ffn_kernel(a_ref, x_ref, w_ref, g_ref, wg_ref, wu_ref, wd_ref, out_ref):
    x1 = x_ref[...] + jnp.dot(a_ref[...].astype(BF16), w_ref[...], preferred_element_type=F32)
    out_ref[...] = _ffn_residual(x1, g_ref, wg_ref, wu_ref, wd_ref)


def _mixer_out_ffn(acts, x, consts, tm, mlstm):
    n, d = x.shape
    row = lambda a: pl.BlockSpec((tm, a.shape[1]), lambda i: (i, 0))
    return pl.pallas_call(
        _mlstm_out_ffn_kernel if mlstm else _proj_ffn_kernel,
        grid=(n // tm,),
        in_specs=[row(a) for a in acts] + [row(x)] + [_const_spec(c.shape) for c in consts],
        out_specs=row(x),
        out_shape=jax.ShapeDtypeStruct((n, d), F32),
        compiler_params=_params("parallel"),
        name="mlstm_out_ffn" if mlstm else "attn_out_ffn",
    )(*acts, x, *consts)


def _qkv_kernel(x_ref, g_ref, wq_ref, wk_ref, wv_ref, gq_ref, gk_ref, cos_ref, sin_ref,
                q_ref, k_ref, v_ref, *blk_refs):
    hb = _rms(x_ref[...], g_ref[...]).astype(BF16)
    cos, sin = cos_ref[...], sin_ref[...]

    kb_ref, vt_ref, km_ref = blk_refs if blk_refs else (None, None, None)
    tm = x_ref.shape[0]
    rt = min(tm, MOBA_BLOCK)
    cw = 2 * A_HEAD_DIM

    def norm_rope(t, gh_ref, rows):
        y = _rms(t, gh_ref[...])
        return y * cos[rows, :] + pltpu.roll(y, A_HEAD_DIM // 2, 1) * sin[rows, :]

    for j in range(tm // rt):
        rows = slice(j * rt, (j + 1) * rt)
        hr = hb[rows, :]
        for c in range(A_HEADS // 2):
            t = jnp.dot(hr, wq_ref[:, c * cw:(c + 1) * cw], preferred_element_type=F32)
            for i in range(2):
                sl = slice((2 * c + i) * A_HEAD_DIM, (2 * c + i + 1) * A_HEAD_DIM)
                q_ref[rows, sl] = norm_rope(t[:, i * A_HEAD_DIM:(i + 1) * A_HEAD_DIM], gq_ref, rows)
        for c in range(A_HEADS // 2):
            t = jnp.dot(hr, wk_ref[:, c * cw:(c + 1) * cw], preferred_element_type=F32)
            for i in range(2):
                sl = slice((2 * c + i) * A_HEAD_DIM, (2 * c + i + 1) * A_HEAD_DIM)
                k = norm_rope(t[:, i * A_HEAD_DIM:(i + 1) * A_HEAD_DIM], gk_ref, rows)
                k_ref[rows, sl] = k
                if blk_refs:
                    kb_ref[rows, sl] = k.astype(BF16)
                    mean = jnp.mean(k, axis=0, keepdims=True)
                    km_ref[j, :, sl] = jnp.broadcast_to(mean, (SUBLANES, A_HEAD_DIM))
        for c in range(A_HEADS // 2):
            t = jnp.dot(hr, wv_ref[:, c * cw:(c + 1) * cw], preferred_element_type=F32)
            v_ref[rows, c * cw:(c + 1) * cw] = t
            if blk_refs:
                for i in range(2):
                    r0 = (2 * c + i) * VT_ROWS
                    vt_ref[j, r0:r0 + A_HEAD_DIM, :] = t[:, i * A_HEAD_DIM:(i + 1) * A_HEAD_DIM].T.astype(BF16)
                    vt_ref[j, r0 + A_HEAD_DIM:r0 + VT_ROWS, :] = jnp.ones((BF16_ROWS, MOBA_BLOCK), BF16)


def _qkv(x, g, wq, wk, wv, gq, gk, cos, sin, tm, with_blocks):
    n, d = x.shape
    e = wq.shape[1]
    n_tab = cos.shape[0] // tm
    row = lambda wd: pl.BlockSpec((tm, wd), lambda i: (i, 0))
    tab = pl.BlockSpec((tm, A_HEAD_DIM), lambda i: (i % n_tab, 0))
    out_specs = [row(e), row(e), row(e)]
    out_shape = [jax.ShapeDtypeStruct((n, e), F32)] * 3
    if with_blocks:
        assert tm % MOBA_BLOCK == 0
        gpt = tm // MOBA_BLOCK
        vrows = A_HEADS * VT_ROWS
        out_specs += [row(e), pl.BlockSpec((gpt, vrows, MOBA_BLOCK), lambda i: (i, 0, 0)),
                      pl.BlockSpec((gpt, SUBLANES, e), lambda i: (i, 0, 0))]
        out_shape += [jax.ShapeDtypeStruct((n, e), BF16),
                      jax.ShapeDtypeStruct((n // MOBA_BLOCK, vrows, MOBA_BLOCK), BF16),
                      jax.ShapeDtypeStruct((n // MOBA_BLOCK, SUBLANES, e), F32)]
    return pl.pallas_call(
        _qkv_kernel,
        grid=(n // tm,),
        in_specs=[row(d), _const_spec((1, d)), _const_spec(wq.shape), _const_spec(wk.shape),
                  _const_spec(wv.shape), _const_spec((1, A_HEAD_DIM)), _const_spec((1, A_HEAD_DIM)), tab, tab],
        out_specs=out_specs,
        out_shape=out_shape,
        compiler_params=_params("parallel"),
        name="qkv_rope",
    )(x, g, wq, wk, wv, gq, gk, cos, sin)


def _rope_tables(pos):
    half = A_HEAD_DIM // 2
    inv_freq = ROPE_THETA ** (-jnp.arange(half, dtype=F32) / half)
    ang = pos.astype(F32)[:, None] * inv_freq[None, :]
    c, s = jnp.cos(ang), jnp.sin(ang)
    return jnp.concatenate([c, c], axis=-1), jnp.concatenate([-s, s], axis=-1)


MOBA_QUERY_BLOCKS = 2
MOBA_HEADS_PER_STEP = 4
MASK_BIG = 2.0 ** 100


def _top_blocks_t(scores_t, allowed):
    row = lax.broadcasted_iota(jnp.int32, scores_t.shape, 0)
    s = jnp.where(allowed, scores_t, NEG_INF)
    picked = jnp.zeros(scores_t.shape, F32)
    for _ in range(MOBA_TOPK):
        mx = jnp.max(s, axis=0, keepdims=True)
        cand = jnp.where(jnp.logical_and(s == mx, s > NEG_INF), row, scores_t.shape[0])
        hit = row == jnp.min(cand, axis=0, keepdims=True)
        picked = jnp.where(hit, 1.0, picked)
        s = jnp.where(hit, NEG_INF, s)
    return picked


def _moba_prompt_kernel(q_ref, kb_ref, vt_ref, km_ref, o_ref, sa_ref, sb_ref, *, n_blocks, hp):
    tile = pl.program_id(2)
    blk = MOBA_BLOCK
    qb_n = MOBA_QUERY_BLOCKS
    qw = qb_n * blk
    first = tile * qb_n
    scale = A_HEAD_DIM ** -0.5
    nbp = -(-n_blocks // BF16_ROWS) * BF16_ROWS
    lane = lax.broadcasted_iota(jnp.int32, (blk, LANES), 1)
    blk_i = lax.broadcasted_iota(jnp.int32, (nbp, qw), 0)
    own_q = first + lax.broadcasted_iota(jnp.int32, (nbp, qw), 1) // blk
    heads = [slice(h * A_HEAD_DIM, (h + 1) * A_HEAD_DIM) for h in range(hp)]
    vrows = [slice(h * VT_ROWS, (h + 1) * VT_ROWS) for h in range(hp)]

    q_t = [q_ref[:, sl].T for sl in heads]
    q_tb = [(qt * scale).astype(BF16) for qt in q_t]
    own_logits = [[jnp.dot(kb_ref[pl.ds(pl.multiple_of((first + d) * blk, blk), blk), sl], qb,
                           preferred_element_type=F32) for sl, qb in zip(heads, q_tb)]
                  for d in range(qb_n)]
    kms = []
    for sl in heads:
        km = km_ref[:, sl]
        if nbp > n_blocks:
            km = jnp.concatenate([km, jnp.zeros((nbp - n_blocks, A_HEAD_DIM), F32)], axis=0)
        kms.append(km)
    scores = [jnp.dot(km, qt, precision=_HIGHEST, preferred_element_type=F32) for km, qt in zip(kms, q_t)]
    picks = [_top_blocks_t(sc, blk_i < own_q) for sc in scores]
    q_ext = []
    for qb, picked in zip(q_tb, picks):
        pad = jnp.zeros((LANES - nbp, qw), BF16)
        q_ext.append(jnp.concatenate([qb, (1.0 - picked).astype(BF16), pad], axis=0))

    def stage_logits(ref, j):
        st = pl.multiple_of(j * blk, blk)
        mask_cols = jnp.where(lane == j, -MASK_BIG, 0.0).astype(BF16)
        for h, (sl, qe) in enumerate(zip(heads, q_ext)):
            k_ext = jnp.concatenate([kb_ref[pl.ds(st, blk), sl], mask_cols], axis=1)
            ref[h] = jnp.dot(k_ext, qe, preferred_element_type=F32)

    stage_logits(sb_ref, 0)

    def accumulate(j, logits, state):
        probs, stats = [], []
        for s, (m, _) in zip(logits, state):
            m_new = jnp.maximum(m, jnp.max(s, axis=0, keepdims=True))
            probs.append(jnp.exp(s - m_new).astype(BF16))
            stats.append((m_new, jnp.exp(m - m_new)))
        return tuple(
            (m_new, alpha * acc + jnp.dot(vt_ref[j, vr, :], pb, preferred_element_type=F32))
            for vr, pb, (m_new, alpha), (_, acc) in zip(vrows, probs, stats, state))

    def accumulate_staged(j, ref, state):
        return accumulate(j, tuple(ref[h] for h in range(hp)), state)

    key_i = lax.broadcasted_iota(jnp.int32, (blk, qw), 0)
    qry_i = lax.broadcasted_iota(jnp.int32, (blk, qw), 1)
    state = tuple((jnp.full((1, qw), -MASK_BIG, F32), jnp.zeros((VT_ROWS, qw), F32)) for _ in heads)
    for d in range(qb_n):
        masked = []
        for s, picked in zip(own_logits[d], picks):
            sel = jnp.sum(jnp.where(blk_i == first + d, picked, 0.0), axis=0, keepdims=True)
            later = jnp.logical_and(qry_i >= (d + 1) * blk, jnp.broadcast_to(sel, (blk, qw)) > 0.5)
            causal = jnp.logical_and(qry_i >= d * blk,
                                     jnp.logical_and(qry_i < (d + 1) * blk, key_i <= qry_i - d * blk))
            masked.append(jnp.where(jnp.logical_or(causal, later), s, -MASK_BIG))
        state = accumulate(first + d, masked, state)

    def pair(i, state):
        j = 2 * i
        stage_logits(sa_ref, j + 1)
        state = accumulate_staged(j, sb_ref, state)
        stage_logits(sb_ref, j + 2)
        return accumulate_staged(j + 1, sa_ref, state)

    state = lax.fori_loop(0, tile, pair, state)
    for sl, (_, acc) in zip(heads, state):
        o_ref[:, sl] = (acc[0:A_HEAD_DIM, :] / acc[A_HEAD_DIM:A_HEAD_DIM + 1, :]).T


def _moba_prompt(q, kb, vt, kmean):
    b, s, e = q.shape
    nb = s // MOBA_BLOCK
    hp = MOBA_HEADS_PER_STEP
    qw = MOBA_QUERY_BLOCKS * MOBA_BLOCK
    assert s % qw == 0 and nb <= LANES and A_HEADS % hp == 0 and MOBA_QUERY_BLOCKS == 2
    w = hp * A_HEAD_DIM
    tile = pl.BlockSpec((None, qw, w), lambda bi, h, qi: (bi, qi, h))
    return pl.pallas_call(
        functools.partial(_moba_prompt_kernel, n_blocks=nb, hp=hp),
        grid=(b, A_HEADS // hp, s // qw),
        in_specs=[tile, pl.BlockSpec((None, s, w), lambda bi, h, qi: (bi, 0, h)),
                  pl.BlockSpec((None, nb, hp * VT_ROWS, MOBA_BLOCK), lambda bi, h, qi: (bi, 0, h, 0)),
                  pl.BlockSpec((None, nb, w), lambda bi, h, qi: (bi, 0, h))],
        out_specs=tile,
        out_shape=jax.ShapeDtypeStruct((b, s, e), F32),
        scratch_shapes=[pltpu.VMEM((hp, MOBA_BLOCK, qw), F32)] * 2,
        compiler_params=_params("parallel", "parallel", "arbitrary"),
        name="moba_prompt",
    )(q, kb, vt, kmean)


def _moba_decode_kernel(pt_ref, q_ref, kn_ref, vn_ref, ck_ref, cv_ref, o_ref, kbuf, vbuf, sem, logit_ref,
                        *, n_pages, page_rows):
    bi = pl.program_id(0)
    n_seq = pl.num_programs(0)
    slot = lax.rem(bi, 2)
    R = A_HEADS * DEC_PAD
    scale = A_HEAD_DIM ** -0.5
    pages_per_block = MOBA_BLOCK // page_rows
    n_blocks = n_pages // pages_per_block

    def page_copy(cache_ref, buf_ref, which, seq, sl, g):
        page = pt_ref[seq * n_pages + g]
        return pltpu.make_async_copy(cache_ref.at[page], buf_ref.at[sl, g], sem.at[sl, which])

    def start_pages(seq, sl):
        for g in range(n_pages):
            page_copy(ck_ref, kbuf, 0, seq, sl, g).start(priority=g % 2)
        for g in range(n_pages):
            page_copy(cv_ref, vbuf, 1, seq, sl, g).start(priority=g % 2)

    @pl.when(bi == 0)
    def _():
        start_pages(bi, slot)

    @pl.when(bi + 1 < n_seq)
    def _():
        start_pages(bi + 1, 1 - slot)

    def rows_by_head(ref):
        return jnp.concatenate([ref[:, h * A_HEAD_DIM:(h + 1) * A_HEAD_DIM] for h in range(A_HEADS)], axis=0)

    def by_head(ref):
        return jnp.concatenate([ref[pl.ds(h, page_rows, stride=A_HEADS), :].astype(BF16) for h in range(A_HEADS)],
                               axis=0)

    q = rows_by_head(q_ref)
    qb = q.astype(BF16)

    for g in range(n_pages):
        page_copy(ck_ref, kbuf, 0, bi, slot, g).wait()
    ksum = []
    for g in range(n_pages):
        kpage = kbuf.at[slot, g]
        lg = lax.dot_general(qb, by_head(kpage), _NT, preferred_element_type=F32)
        for h in range(A_HEADS):
            hr = slice(h * DEC_PAD, (h + 1) * DEC_PAD)
            logit_ref[g, hr, :] = lg[hr, h * page_rows:(h + 1) * page_rows] * scale
        part = jnp.sum(kpage[...].reshape(page_rows, A_HEADS, A_HEAD_DIM), axis=0)
        if g % pages_per_block == 0:
            ksum.append(part)
        else:
            ksum[-1] = ksum[-1] + part

    kmean = jnp.concatenate(ksum, axis=0) * (1.0 / MOBA_BLOCK)
    nbh = n_blocks * A_HEADS
    scores_t = lax.dot_general(kmean, q, _NT, precision=_HIGHEST, preferred_element_type=F32)
    brow = lax.broadcasted_iota(jnp.int32, (nbh, R), 0)
    qcol = lax.broadcasted_iota(jnp.int32, (nbh, R), 1)
    picked_t = _top_blocks_t(scores_t, brow % A_HEADS == qcol // DEC_PAD)
    erow = lax.broadcasted_iota(jnp.int32, (nbh, n_blocks * page_rows), 0)
    ecol = lax.broadcasted_iota(jnp.int32, (nbh, n_blocks * page_rows), 1)
    expand = jnp.where(erow // A_HEADS == ecol // page_rows, 1.0, 0.0).astype(BF16)
    keep = lax.dot_general(picked_t.astype(BF16), expand, _TN, preferred_element_type=F32)
    keep_blk = [keep[:, j * page_rows:(j + 1) * page_rows] > 0.5 for j in range(n_blocks)]

    kn = rows_by_head(kn_ref).astype(BF16)
    ln = lax.dot_general(qb, kn, _NT, preferred_element_type=F32) * scale
    rn = lax.broadcasted_iota(jnp.int32, (R, R), 0)
    cn = lax.broadcasted_iota(jnp.int32, (R, R), 1)
    ok_n = jnp.logical_and(rn // DEC_PAD == cn // DEC_PAD, cn % DEC_PAD <= rn % DEC_PAD)
    ln = jnp.where(ok_n, ln, NEG_INF)

    masked = [jnp.where(keep_blk[g // pages_per_block], logit_ref[g], NEG_INF) for g in range(n_pages)]
    mm = masked[0]
    for g in range(1, n_pages):
        mm = jnp.maximum(mm, masked[g])
    m = jnp.maximum(jnp.max(ln, axis=-1, keepdims=True), jnp.max(mm, axis=-1, keepdims=True))
    pn = jnp.exp(ln - m)
    probs = [jnp.exp(s - m) for s in masked]
    lsum = probs[0]
    for g in range(1, n_pages):
        lsum = lsum + probs[g]
    l = jnp.sum(pn, axis=-1, keepdims=True) + jnp.sum(lsum, axis=-1, keepdims=True)

    for g in range(n_pages):
        page_copy(cv_ref, vbuf, 1, bi, slot, g).wait()
    acc = jnp.dot(pn.astype(BF16), rows_by_head(vn_ref).astype(BF16), preferred_element_type=F32)
    zero = jnp.zeros((DEC_PAD, page_rows), F32)
    for g in range(n_pages):
        pe = probs[g]
        p_bd = jnp.concatenate(
            [jnp.concatenate([pe[h * DEC_PAD:(h + 1) * DEC_PAD, :] if hh == h else zero for hh in range(A_HEADS)],
                             axis=1) for h in range(A_HEADS)], axis=0)
        acc = acc + jnp.dot(p_bd.astype(BF16), by_head(vbuf.at[slot, g]), preferred_element_type=F32)
    res = acc / l
    for h in range(A_HEADS):
        o_ref[:, h * A_HEAD_DIM:(h + 1) * A_HEAD_DIM] = res[h * DEC_PAD:(h + 1) * DEC_PAD, :]


def _moba_decode(q, kn, vn, cache_k, cache_v, page_table):
    b, t, e = q.shape
    n_phys, page_rows = cache_k.shape[0], cache_k.shape[1]
    n_pages = page_table.shape[1]
    assert t == DEC_PAD and MOBA_BLOCK % page_rows == 0 and n_pages % (MOBA_BLOCK // page_rows) == 0
    cols = page_rows * A_HEADS
    ck = cache_k.reshape(n_phys, cols, A_HEAD_DIM)
    cv = cache_v.reshape(n_phys, cols, A_HEAD_DIM)
    pt = page_table.reshape(-1).astype(jnp.int32)

    tok = pl.BlockSpec((None, t, e), lambda bi, pt_: (bi, 0, 0))
    hbm = pl.BlockSpec(memory_space=pl.ANY)
    grid_spec = pltpu.PrefetchScalarGridSpec(
        num_scalar_prefetch=1,
        grid=(b,),
        in_specs=[tok, tok, tok, hbm, hbm],
        out_specs=tok,
        scratch_shapes=[pltpu.VMEM((2, n_pages, cols, A_HEAD_DIM), F32),
                        pltpu.VMEM((2, n_pages, cols, A_HEAD_DIM), F32),
                        pltpu.SemaphoreType.DMA((2, 2)),
                        pltpu.VMEM((n_pages, A_HEADS * DEC_PAD, page_rows), F32)],
    )
    return pl.pallas_call(
        functools.partial(_moba_decode_kernel, n_pages=n_pages, page_rows=page_rows),
        grid_spec=grid_spec,
        out_shape=jax.ShapeDtypeStruct((b, t, e), F32),
        compiler_params=_params("arbitrary"),
        name="moba_decode",
    )(pt, q, kn, vn, ck, cv)


ROW_TILE = 512


def _trunk(x, pos0, c0, n0, m0, past, w):
    b, s, d = x.shape
    n = b * s
    sp = -(-s // DEC_PAD) * DEC_PAD
    tm = min(ROW_TILE, n // 2)
    assert n % tm == 0 and (tm % s == 0 or s % tm == 0)
    xf = x.reshape(n, d)
    r3 = lambda t: jnp.pad(t.reshape(b, s, t.shape[-1]), ((0, 0), (0, sp - s), (0, 0)))
    flat = lambda t: t[:, :s].reshape(n, t.shape[-1])

    q, k, v, o, gates = _norm_proj(xf, w["norm_mix"][0], [w["wq_m"], w["wk_m"], w["wv_m"], w["wo_m"], w["wgate_m"]], tm)
    hs, c_new, n_new, m_new = _mlstm(r3(q), r3(k), r3(v), r3(gates), w["b_gate"], c0, n0, m0, s)
    xf = _mixer_out_ffn((flat(hs), o), xf, (w["norm_h_m"], w["w_out_m"], w["norm_ffn"][0], w["wg"][0],
                                            w["wu"][0], w["wd"][0]), tm, mlstm=True)

    pos = pos0 + jnp.arange(s, dtype=jnp.int32)
    cos, sin = _rope_tables(pos)
    if s < tm:
        cos, sin = jnp.tile(cos, (tm // s, 1)), jnp.tile(sin, (tm // s, 1))
    qa, ka, va, *blocks = _qkv(xf, w["norm_mix"][1], w["wq_a"], w["wk_a"], w["wv_a"], w["norm_q_a"],
                               w["norm_k_a"], cos, sin, tm, with_blocks=past is None)
    if past is None:
        kb, vt, kmean = blocks
        nb = s // MOBA_BLOCK
        att = _moba_prompt(r3(qa), r3(kb), vt.reshape(b, nb, -1, MOBA_BLOCK), kmean[:, 0, :].reshape(b, nb, -1))
    else:
        att = _moba_decode(r3(qa), r3(ka), r3(va), *past)
    xf = _mixer_out_ffn((flat(att),), xf, (w["w_out_a"], w["norm_ffn"][1], w["wg"][1], w["wu"][1], w["wd"][1]),
                        tm, mlstm=False)

    heads = lambda t: t.reshape(b, s, A_HEADS, A_HEAD_DIM)
    return xf.reshape(b, s, d), heads(ka), heads(va), c_new, n_new, m_new


def kernel(x_prompt, x_sample, state_C, state_n, state_m, cache_k, cache_v, page_table, norm_mix, norm_ffn,
           w_in_m, b_gate_m, norm_h_m, w_out_m, w_qkv_a, norm_q_a, norm_k_a, w_out_a, w_gu, w_down):
    assert norm_mix.shape[0] == 2 and w_in_m.shape[0] == 1 and w_qkv_a.shape[0] == 1
    bf = lambda t: t.astype(BF16)
    row = lambda t: t.reshape(1, -1).astype(F32)
    w_in = w_in_m[0]
    e_a = A_HEADS * A_HEAD_DIM
    gate_w = jnp.pad(w_in[:, 2 * M_QK_W + 2 * M_V_W:], ((0, 0), (0, LANES - 2 * M_HEADS)))
    w = {
        "norm_mix": [row(norm_mix[0]), row(norm_mix[1])],
        "norm_ffn": [row(norm_ffn[0]), row(norm_ffn[1])],
        "wq_m": bf(w_in[:, :M_QK_W]), "wk_m": bf(w_in[:, M_QK_W:2 * M_QK_W]),
        "wv_m": bf(w_in[:, 2 * M_QK_W:2 * M_QK_W + M_V_W]),
        "wo_m": bf(w_in[:, 2 * M_QK_W + M_V_W:2 * M_QK_W + 2 * M_V_W]),
        "wgate_m": bf(gate_w),
        "b_gate": jnp.pad(b_gate_m[0].astype(F32), (0, LANES - 2 * M_HEADS)).reshape(1, LANES),
        "norm_h_m": row(norm_h_m[0]), "w_out_m": bf(w_out_m[0]),
        "wq_a": bf(w_qkv_a[0][:, :e_a]), "wk_a": bf(w_qkv_a[0][:, e_a:2 * e_a]), "wv_a": bf(w_qkv_a[0][:, 2 * e_a:]),
        "norm_q_a": row(norm_q_a[0]), "norm_k_a": row(norm_k_a[0]), "w_out_a": bf(w_out_a[0]),
        "wg": [bf(w_gu[i][:, :FFN_HIDDEN]) for i in range(2)],
        "wu": [bf(w_gu[i][:, FFN_HIDDEN:]) for i in range(2)],
        "wd": [bf(w_down[i]) for i in range(2)],
    }

    def state(c, nn, mm):
        bsz = c.shape[0]
        m_row = jnp.pad(mm.astype(F32), ((0, 0), (0, LANES - M_HEADS))).reshape(bsz, 1, LANES)
        return c.astype(F32), nn.astype(F32).reshape(bsz, M_HEADS, 1, M_QK_DIM), m_row

    def unstate(c, nn, mm, dt):
        bsz = c.shape[0]
        return (c[None].astype(dt), nn.reshape(1, bsz, M_HEADS, M_QK_DIM).astype(dt),
                mm.reshape(bsz, LANES)[None, :, :M_HEADS].astype(dt))

    sd = state_C.dtype
    bp, sp, _ = x_prompt.shape
    zc = jnp.zeros((bp,) + state_C.shape[2:], F32)
    zn = jnp.zeros((bp,) + state_n.shape[2:], F32)
    zm = jnp.zeros((bp,) + state_m.shape[2:], F32)
    yp, pk, pv, pc, pn, pm = _trunk(x_prompt, 0, *state(zc, zn, zm), None, w)

    past_len = page_table.shape[1] * cache_k.shape[2]
    ys, sk, sv, sc, sn, sm = _trunk(x_sample, past_len, *state(state_C[0], state_n[0], state_m[0]),
                                    (cache_k[0], cache_v[0], page_table), w)

    return (yp, ys, pk[None], pv[None], *unstate(pc, pn, pm, sd), sk[None], sv[None], *unstate(sc, sn, sm, sd))
```
